```python
import math
import jax
import jax.numpy as jnp
from jax import lax
import numpy as np

D_MODEL = 1024
BATCH = 4
SEQ = 8192
DEPTH = 2

CTX_LEN = 256
GRID_W = 64

DN_HEADS = 4
DN_DK = 64
DN_DV = 64
DN_CONV = 5
DN_CHUNK = 64
DT_MIN = 0.001
DT_MAX = 0.1

S5_WIDTH = 256
S5_GROUP = 16
S5_GROUPS = S5_WIDTH // S5_GROUP
S5_STATE = 64

DA_HEADS = 4
DA_DK = 32
DA_DV = 2 * DA_DK
DA_QBLOCK = 128
ROPE_BASE = 10000.0

GLA_HEADS = 4
GLA_DK = 32
GLA_DV = 64
GLA_GATE_RANK = 16
GLA_TAU = 16.0
GLA_CHUNK = 16

N_BRANCH = 4
BRANCH_WIDTH = 256

N_EXPERTS = 16
N_GROUPS = 4
EXPERTS_PER_GROUP = N_EXPERTS // N_GROUPS
TOP_K = 2
D_EXPERT = 512
MOE_BLOCK = 128

NORM_EPS = 1e-6

PROJ_LAYOUT = (
    ("dn_qkv", DN_HEADS * (2 * DN_DK + DN_DV)),
    ("dn_z", DN_HEADS * DN_DV),
    ("dn_a", 2 * DN_HEADS),
    ("dn_b", 2 * DN_HEADS),
    ("s5_u", S5_WIDTH),
    ("da_q", DA_HEADS * 2 * DA_DK),
    ("da_k", DA_HEADS * 2 * DA_DK),
    ("da_v", DA_HEADS * DA_DV),
    ("gla_q", GLA_HEADS * GLA_DK),
    ("gla_k", GLA_HEADS * GLA_DK),
    ("gla_v", GLA_HEADS * GLA_DV),
    ("gla_r", GLA_HEADS * GLA_DV),
    ("gla_gate", 2 * GLA_GATE_RANK),
)
D_IN = sum(w for _, w in PROJ_LAYOUT)

kernel_name = "hybrid_flow_deltanet_s5_diffattn_gla_moe"

F32 = jnp.float32


def rmsnorm(x, g, eps=NORM_EPS):
    xf = x.astype(F32)
    y = xf * lax.rsqrt(jnp.mean(xf * xf, axis=-1, keepdims=True) + eps)
    return (y * g.astype(F32)).astype(x.dtype)


def l2norm(x, eps=1e-6):
    xf = x.astype(F32)
    return xf * lax.rsqrt(jnp.sum(xf * xf, axis=-1, keepdims=True) + eps)


def modulate(xn, shift, scale):
    return xn * (1.0 + scale) + shift


def split_proj(z):
    out, off = {}, 0
    for name, width in PROJ_LAYOUT:
        out[name] = z[..., off:off + width]
        off += width
    return out


def joint_order(ctx_part, lat_part, reverse):
    if reverse:
        ctx_part, lat_part = jnp.flip(ctx_part, 1), jnp.flip(lat_part, 1)
    return jnp.concatenate([ctx_part, lat_part], axis=1)


def split_order(y, n_ctx, reverse):
    y_ctx, y_lat = y[:, :n_ctx], y[:, n_ctx:]
    if reverse:
        y_ctx, y_lat = jnp.flip(y_ctx, 1), jnp.flip(y_lat, 1)
    return y_ctx, y_lat


def centred_dwconv(x, w):
    k = w.shape[0]
    return lax.conv_general_dilated(
        x, w[:, None, :].astype(x.dtype), window_strides=(1,), padding=[(k // 2, k // 2)],
        dimension_numbers=("NWC", "WIO", "NWC"), feature_group_count=x.shape[-1])


def axial_rope_tables(n_rows, head_dim):
    t = jnp.arange(n_rows * GRID_W)
    row = (t // GRID_W).astype(F32)
    col = (t % GRID_W).astype(F32)
    n_freq = head_dim // 4
    inv = ROPE_BASE ** (-jnp.arange(n_freq, dtype=F32) / n_freq)
    ang = jnp.concatenate([row[:, None] * inv, col[:, None] * inv], axis=-1)
    return jnp.cos(ang), jnp.sin(ang)


def apply_rope(x, cos, sin):
    half = x.shape[-1] // 2
    cos, sin = cos[:, None, None, :], sin[:, None, None, :]
    x1, x2 = x[..., :half], x[..., half:]
    return jnp.concatenate([x1 * cos - x2 * sin, x1 * sin + x2 * cos], axis=-1).astype(x.dtype)


def gated_delta_rule_chunked(q, k, v, g, beta):
    B, T, H, dk = q.shape
    dv = v.shape[-1]
    C = DN_CHUNK
    n = T // C

    def to_chunks(t):
        return jnp.moveaxis(t.astype(F32), 1, 2).reshape((B, H, n, C) + t.shape[3:])

    q = to_chunks(q) * dk ** -0.5
    k, v, beta = to_chunks(k), to_chunks(v), to_chunks(beta)
    g = jnp.cumsum(to_chunks(g), axis=-1)
    incl = jnp.tril(jnp.ones((C, C), bool))
    strict = jnp.tril(jnp.ones((C, C), bool), -1)
    decay = jnp.where(incl, jnp.exp(jnp.where(incl, g[..., :, None] - g[..., None, :], 0.0)), 0.0)
    kb = k * beta[..., None]
    lower = jnp.where(strict, jnp.einsum("bhnid,bhnjd->bhnij", kb, k) * decay, 0.0)
    rhs = jnp.concatenate([v * beta[..., None], kb * jnp.exp(g)[..., None]], axis=-1)
    sol = lax.linalg.triangular_solve(lower + jnp.eye(C, dtype=F32), rhs,
                                      left_side=True, lower=True, unit_diagonal=True)
    u, w = sol[..., :dv], sol[..., dv:]
    a_intra = jnp.where(incl, jnp.einsum("bhnid,bhnjd->bhnij", q, k) * decay, 0.0)
    q_dec = q * jnp.exp(g)[..., None]
    k_dec = k * jnp.exp(g[..., -1:] - g)[..., None]
    g_last = jnp.exp(g[..., -1])

    def step(state, xs):
        qd, kd, u_i, w_i, a_i, gl = xs
        v_new = u_i - w_i @ state
        o = qd @ state + a_i @ v_new
        state = state * gl[..., None, None] + jnp.einsum("bhcd,bhcv->bhdv", kd, v_new)
        return state, o

    xs = tuple(jnp.moveaxis(t, 2, 0) for t in (q_dec, k_dec, u, w, a_intra, g_last))
    _, o = lax.scan(step, jnp.zeros((B, H, dk, dv), F32), xs)
    return jnp.moveaxis(o, 0, 2).reshape(B, H, T, dv).transpose(0, 2, 1, 3)


def deltanet_mixer(p_lat, p_ctx, conv_w, a_log, dt_bias, norm_g, with_ctx):
    n_ctx = p_ctx["dn_qkv"].shape[1]

    def prep(p):
        qkv = jax.nn.silu(centred_dwconv(p["dn_qkv"], conv_w))
        b_, t_ = qkv.shape[:2]
        q, k, v = jnp.split(qkv, [DN_HEADS * DN_DK, 2 * DN_HEADS * DN_DK], axis=-1)
        q = l2norm(q.reshape(b_, t_, DN_HEADS, DN_DK))
        k = l2norm(k.reshape(b_, t_, DN_HEADS, DN_DK))
        return q, k, v.reshape(b_, t_, DN_HEADS, DN_DV)

    ql, kl, vl = prep(p_lat)
    qc, kc, vc = prep(p_ctx)
    o_lat, o_ctx = 0.0, 0.0
    for d in range(2):
        rev = d == 1
        cols = slice(d * DN_HEADS, (d + 1) * DN_HEADS)

        def gates(p):
            a = p["dn_a"][..., cols].astype(F32)
            b = p["dn_b"][..., cols].astype(F32)
            g = -jnp.exp(a_log[d].astype(F32)) * jax.nn.softplus(a + dt_bias[d].astype(F32))
            return g, jax.nn.sigmoid(b)

        gl, bl = gates(p_lat)
        gc, bc = gates(p_ctx)
        o = gated_delta_rule_chunked(joint_order(qc, ql, rev), joint_order(kc, kl, rev),
                                     joint_order(vc, vl, rev), joint_order(gc, gl, rev),
                                     joint_order(bc, bl, rev))
        oc, ol = split_order(o, n_ctx, rev)
        o_lat, o_ctx = o_lat + ol, o_ctx + oc

    def finish(o, p):
        z = p["dn_z"].reshape(o.shape).astype(F32)
        return (rmsnorm(o, norm_g) * jax.nn.silu(z)).reshape(o.shape[0], o.shape[1], DN_HEADS * DN_DV)

    return finish(o_lat, p_lat), (finish(o_ctx, p_ctx) if with_ctx else None)


def complex_affine_combine(e1, e2):
    a1r, a1i, b1r, b1i = e1
    a2r, a2i, b2r, b2i = e2
    return (a2r * a1r - a2i * a1i, a2r * a1i + a2i * a1r,
            a2r * b1r - a2i * b1i + b2r, a2r * b1i + a2i * b1r + b2i)


def s5_mixer(u_lat, u_ctx, a_re, a_im, log_dt, b_re, b_im, c_re, c_im, d_skip, glu_w, glu_b, with_ctx):
    n_ctx = u_ctx.shape[1]

    def grp(u):
        return u.astype(F32).reshape(u.shape[0], u.shape[1], S5_GROUPS, S5_GROUP)

    ul, uc = grp(u_lat), grp(u_ctx)
    br, bi = b_re.astype(F32), b_im.astype(F32)
    y_lat, y_ctx = 0.0, 0.0
    for d in range(2):
        rev = d == 1
        lam_re, lam_im = a_re[d].astype(F32), a_im[d].astype(F32)
        dt = jnp.exp(log_dt[d].astype(F32))[:, None]
        mag = jnp.exp(lam_re * dt)
        ab_re, ab_im = mag * jnp.cos(lam_im * dt), mag * jnp.sin(lam_im * dt)
        den = lam_re * lam_re + lam_im * lam_im
        f_re = ((ab_re - 1.0) * lam_re + ab_im * lam_im) / den
        f_im = (ab_im * lam_re - (ab_re - 1.0) * lam_im) / den
        bb_re = f_re[..., None] * br - f_im[..., None] * bi
        bb_im = f_re[..., None] * bi + f_im[..., None] * br
        u = joint_order(uc, ul, rev)
        bu_re = jnp.einsum("btgc,gpc->btgp", u, bb_re)
        bu_im = jnp.einsum("btgc,gpc->btgp", u, bb_im)
        a_r = jnp.broadcast_to(ab_re, bu_re.shape)
        a_i = jnp.broadcast_to(ab_im, bu_re.shape)
        _, _, x_re, x_im = lax.associative_scan(complex_affine_combine, (a_r, a_i, bu_re, bu_im), axis=1)
        y = (jnp.einsum("btgp,gcp->btgc", x_re, c_re[d].astype(F32))
             - jnp.einsum("btgp,gcp->btgc", x_im, c_im[d].astype(F32)))
        yc, yl = split_order(y, n_ctx, rev)
        y_lat, y_ctx = y_lat + yl, y_ctx + yc

    def finish(y, u):
        y = y.reshape(y.shape[0], y.shape[1], S5_WIDTH) + d_skip.astype(F32) * u.astype(F32)
        z = jax.nn.gelu(y)
        return z * jax.nn.sigmoid(z @ glu_w.astype(F32) + glu_b.astype(F32))

    return finish(y_lat, u_lat), (finish(y_ctx, u_ctx) if with_ctx else None)


def diff_core(q, k, v, lam):
    s = jnp.einsum("bhmqd,bhmkd->bhmqk", q, k).astype(F32) * (q.shape[-1] ** -0.5)
    p = jax.nn.softmax(s, axis=-1)
    a = p[:, :, 0] - lam * p[:, :, 1]
    return jnp.einsum("bhqk,bhkv->bhqv", a, v.astype(F32))


def diff_attention(q_lat, k_lat, v_lat, q_ctx, k_ctx, v_ctx, lam_vec, norm_g, lam_init, cos, sin, with_ctx):
    b_, s_ = q_lat.shape[:2]

    def heads(t):
        return t.reshape(t.shape[0], t.shape[1], DA_HEADS, 2, DA_DK)

    def to_bhm(t):
        return t.transpose(0, 2, 3, 1, 4)

    def vheads(t):
        return t.reshape(t.shape[0], t.shape[1], DA_HEADS, DA_DV).transpose(0, 2, 1, 3)

    lf = lam_vec.astype(F32)
    lam = jnp.exp(jnp.sum(lf[0] * lf[1])) - jnp.exp(jnp.sum(lf[2] * lf[3])) + lam_init
    ql = apply_rope(heads(q_lat), cos, sin)
    kl = apply_rope(heads(k_lat), cos, sin)
    kc = to_bhm(heads(k_ctx))
    vc = vheads(v_ctx)
    k_all = jnp.concatenate([to_bhm(kl), kc], axis=3)
    v_all = jnp.concatenate([vheads(v_lat), vc], axis=2)
    nb = s_ // DA_QBLOCK
    qb = jnp.moveaxis(to_bhm(ql).reshape(b_, DA_HEADS, 2, nb, DA_QBLOCK, DA_DK), 3, 0)
    o = lax.map(lambda qblk: diff_core(qblk, k_all, v_all, lam), qb)
    o = jnp.moveaxis(o, 0, 2).reshape(b_, DA_HEADS, s_, DA_DV).transpose(0, 2, 1, 3)

    def finish(o):
        o = rmsnorm(o, norm_g, eps=1e-5) * (1.0 - lam_init)
        return o.reshape(o.shape[0], o.shape[1], DA_HEADS * DA_DV)

    y_ctx = None
    if with_ctx:
        oc = diff_core(to_bhm(heads(q_ctx)), kc, vc, lam).transpose(0, 2, 1, 3)
        y_ctx = finish(oc)
    return finish(o), y_ctx


def gla_chunked(q, k, v, log_a):
    B, T, H, dk = q.shape
    dv = v.shape[-1]
    C = GLA_CHUNK
    n = T // C

    def to_chunks(t):
        return jnp.moveaxis(t.astype(F32), 1, 2).reshape(B, H, n, C, t.shape[-1])

    q = to_chunks(q) * dk ** -0.5
    k, v = to_chunks(k), to_chunks(v)
    b = jnp.cumsum(to_chunks(log_a), axis=3)
    incl = jnp.tril(jnp.ones((C, C), bool))[..., None]
    rel = b[..., :, None, :] - b[..., None, :, :]
    decay = jnp.exp(jnp.where(incl, rel, -jnp.inf))
    a_intra = jnp.einsum("bhnid,bhnjd,bhnijd->bhnij", q, k, decay)
    q_dec = q * jnp.exp(b)
    k_dec = k * jnp.exp(b[..., -1:, :] - b)
    a_last = jnp.exp(b[..., -1, :])

    def step(state, xs):
        qd, kd, v_i, a_i, al = xs
        o = qd @ state + a_i @ v_i
        state = state * al[..., :, None] + jnp.einsum("bhcd,bhcv->bhdv", kd, v_i)
        return state, o

    xs = tuple(jnp.moveaxis(t, 2, 0) for t in (q_dec, k_dec, v, a_intra, a_last))
    _, o = lax.scan(step, jnp.zeros((B, H, dk, dv), F32), xs)
    return jnp.moveaxis(o, 0, 2).reshape(B, H, T, dv).transpose(0, 2, 1, 3)


def gla_mixer(p_lat, p_ctx, w_gate2, b_gate2, norm_g, with_ctx):
    n_ctx = p_ctx["gla_q"].shape[1]

    def heads(t, dim):
        return t.reshape(t.shape[0], t.shape[1], GLA_HEADS, dim)

    ql, kl, vl = heads(p_lat["gla_q"], GLA_DK), heads(p_lat["gla_k"], GLA_DK), heads(p_lat["gla_v"], GLA_DV)
    qc, kc, vc = heads(p_ctx["gla_q"], GLA_DK), heads(p_ctx["gla_k"], GLA_DK), heads(p_ctx["gla_v"], GLA_DV)
    o_lat, o_ctx = 0.0, 0.0
    for d in range(2):
        rev = d == 1

        def log_forget(p):
            low = p["gla_gate"][..., d * GLA_GATE_RANK:(d + 1) * GLA_GATE_RANK].astype(F32)
            z = low @ w_gate2[d].astype(F32) + b_gate2[d].astype(F32)
            return heads(jax.nn.log_sigmoid(z) / GLA_TAU, GLA_DK)

        o = gla_chunked(joint_order(qc, ql, rev), joint_order(kc, kl, rev), joint_order(vc, vl, rev),
                        joint_order(log_forget(p_ctx), log_forget(p_lat), rev))
        oc, ol = split_order(o, n_ctx, rev)
        o_lat, o_ctx = o_lat + ol, o_ctx + oc

    def finish(o, p):
        r = heads(p["gla_r"], GLA_DV).astype(F32)
        return (rmsnorm(o, norm_g) * jax.nn.silu(r)).reshape(o.shape[0], o.shape[1], GLA_HEADS * GLA_DV)

    return finish(o_lat, p_lat), (finish(o_ctx, p_ctx) if with_ctx else None)


def merge_branches(a, ys, w_branch, w_gate, b_gate, w_out):
    dm = a.shape[-1]
    acc = 0.0
    for i, y in enumerate(ys):
        gate = jax.nn.sigmoid(a @ w_gate[:, i * dm:(i + 1) * dm] + b_gate[i * dm:(i + 1) * dm])
        acc = acc + gate * (y.astype(a.dtype) @ w_branch[i])
    return acc @ w_out


def moe_ffn(h, w_router, b_router, w_e_gate, w_e_up, w_e_down):
    n_tok, dm = h.shape
    scores = jax.nn.sigmoid((h @ w_router).astype(F32))
    grouped = (scores + b_router.astype(F32)).reshape(n_tok, N_GROUPS, EXPERTS_PER_GROUP)
    group_score = jnp.sum(lax.top_k(grouped, TOP_K)[0], axis=-1)
    best = jnp.argmax(group_score, axis=-1)
    cand = grouped[jnp.arange(n_tok), best]
    _, local = lax.top_k(cand, TOP_K)
    expert = (best[:, None] * EXPERTS_PER_GROUP + local).astype(jnp.int32)
    weight = jnp.take_along_axis(scores, expert, axis=1)
    weight = weight / jnp.sum(weight, axis=-1, keepdims=True)
    n_pair = n_tok * TOP_K
    flat_e = expert.reshape(n_pair)
    order = jnp.argsort(flat_e)
    counts = jnp.bincount(flat_e, length=N_EXPERTS)
    padded = (counts + MOE_BLOCK - 1) // MOE_BLOCK * MOE_BLOCK
    pad_end = jnp.cumsum(padded)
    start = jnp.cumsum(counts) - counts
    sorted_e = flat_e[order]
    slot_sorted = pad_end[sorted_e] - padded[sorted_e] + jnp.arange(n_pair) - start[sorted_e]
    slot = jnp.zeros((n_pair,), jnp.int32).at[order].set(slot_sorted.astype(jnp.int32))
    n_blocks = -(-n_pair // MOE_BLOCK) + N_EXPERTS
    token_of_pair = jnp.arange(n_pair) // TOP_K
    buf = jnp.zeros((n_blocks * MOE_BLOCK, dm), h.dtype).at[slot].set(h[token_of_pair])
    block_expert = jnp.minimum(
        jnp.searchsorted(pad_end, jnp.arange(n_blocks) * MOE_BLOCK, side="right"), N_EXPERTS - 1)

    def expert_block(args):
        rows, e = args
        hid = jax.nn.silu(rows @ w_e_gate[e]) * (rows @ w_e_up[e])
        return hid @ w_e_down[e]

    out = lax.map(expert_block, (buf.reshape(n_blocks, MOE_BLOCK, dm), block_expert)).reshape(-1, dm)
    pair_out = out[slot].reshape(n_tok, TOP_K, dm)
    return jnp.einsum("nk,nkd->nd", weight.astype(h.dtype), pair_out)


def setup_inputs(seed: int = 0) -> dict:
    key = jax.random.key(seed)
    keys = iter(jax.random.split(key, 48))

    def normal(shape, scale):
        return jax.random.normal(next(keys), shape, F32) * scale

    def gain(shape):
        return 1.0 + normal(shape, 0.05)

    def uniform(shape, lo, hi):
        return jax.random.uniform(next(keys), shape, F32, minval=lo, maxval=hi)

    dm, ly, ne, fe = D_MODEL, DEPTH, N_EXPERTS, D_EXPERT
    g5, p5, n5 = S5_GROUPS, S5_STATE, S5_GROUP
    dn_dt = jnp.exp(uniform((ly, 2, DN_HEADS), math.log(DT_MIN), math.log(DT_MAX)))
    return {
        "x": normal((BATCH, SEQ, dm), 1.0),
        "c": normal((BATCH, dm), 1.0),
        "ctx": normal((BATCH, CTX_LEN, dm), 1.0),
        "c_ctx": normal((dm,), 1.0),
        "w_mod": normal((ly, dm, 6 * dm), 0.5 * dm ** -0.5),
        "b_mod": normal((ly, 6 * dm), 0.01),
        "norm1_g": gain((ly, dm)),
        "norm2_g": gain((ly, dm)),
        "w_in": normal((ly, dm, D_IN), dm ** -0.5),
        "dn_conv": normal((ly, DN_CONV, DN_HEADS * (2 * DN_DK + DN_DV)), DN_CONV ** -0.5),
        "dn_a_log": jnp.log(uniform((ly, 2, DN_HEADS), 1.0, 16.0)),
        "dn_dt_bias": dn_dt + jnp.log(-jnp.expm1(-dn_dt)),
        "dn_norm_g": gain((ly, DN_DV)),
        "s5_a_re": -0.5 + normal((ly, 2, g5, p5), 0.01),
        "s5_a_im": math.pi * jnp.arange(p5, dtype=F32) + normal((ly, 2, g5, p5), 0.01),
        "s5_log_dt": uniform((ly, 2, g5), math.log(DT_MIN), math.log(DT_MAX)),
        "s5_b_re": normal((ly, g5, p5, n5), (2 * n5) ** -0.5),
        "s5_b_im": normal((ly, g5, p5, n5), (2 * n5) ** -0.5),
        "s5_c_re": normal((ly, 2, g5, n5, p5), (2 * p5) ** -0.5),
        "s5_c_im": normal((ly, 2, g5, n5, p5), (2 * p5) ** -0.5),
        "s5_d": normal((ly, S5_WIDTH), 1.0),
        "s5_glu_w": normal((ly, S5_WIDTH, S5_WIDTH), S5_WIDTH ** -0.5),
        "s5_glu_b": normal((ly, S5_WIDTH), 0.01),
        "da_lambda": normal((ly, 4, DA_DK), 0.1),
        "da_norm_g": gain((ly, DA_DV)),
        "gla_w_gate": normal((ly, 2, GLA_GATE_RANK, GLA_HEADS * GLA_DK), GLA_GATE_RANK ** -0.5),
        "gla_b_gate": normal((ly, 2, GLA_HEADS * GLA_DK), 0.1),
        "gla_norm_g": gain((ly, GLA_DV)),
        "w_branch": normal((ly, N_BRANCH, BRANCH_WIDTH, dm), BRANCH_WIDTH ** -0.5),
        "w_gate": normal((ly, dm, N_BRANCH * dm), dm ** -0.5),
        "b_gate": normal((ly, N_BRANCH * dm), 0.01),
        "w_out": normal((ly, dm, dm), dm ** -0.5),
        "w_router": normal((dm, ne), dm ** -0.5),
        "b_router": normal((ne,), 0.01),
        "w_e_gate": normal((ly, ne, dm, fe), dm ** -0.5),
        "w_e_up": normal((ly, ne, dm, fe), dm ** -0.5),
        "w_e_down": normal((ly, ne, fe, dm), fe ** -0.5),
        "final_g": gain((dm,)),
    }


def reference(x, c, ctx, c_ctx, w_mod, b_mod, norm1_g, norm2_g, w_in,
              dn_conv, dn_a_log, dn_dt_bias, dn_norm_g,
              s5_a_re, s5_a_im, s5_log_dt, s5_b_re, s5_b_im, s5_c_re, s5_c_im, s5_d, s5_glu_w, s5_glu_b,
              da_lambda, da_norm_g,
              gla_w_gate, gla_b_gate, gla_norm_g,
              w_branch, w_gate, b_gate, w_out,
              w_router, b_router, w_e_gate, w_e_up, w_e_down, final_g):
    b_, s_, dm = x.shape
    n_ctx = ctx.shape[1]
    ROWS = s_ // GRID_W
    cos, sin = axial_rope_tables(ROWS, DA_DK)
    sc_lat = jax.nn.silu(c)[:, None, :]
    sc_ctx = jax.nn.silu(c_ctx)[None, None, :]
    h_lat, h_ctx = x, ctx
    for layer in range(DEPTH):
        with_ctx = layer < DEPTH - 1
        lam_init = 0.8 - 0.6 * math.exp(-0.3 * layer)
        mod_lat = jnp.split(sc_lat @ w_mod[layer] + b_mod[layer], 6, axis=-1)
        mod_ctx = jnp.split(sc_ctx @ w_mod[layer] + b_mod[layer], 6, axis=-1)

        a_lat = modulate(rmsnorm(h_lat, norm1_g[layer]), mod_lat[0], mod_lat[1])
        a_ctx = modulate(rmsnorm(h_ctx, norm1_g[layer]), mod_ctx[0], mod_ctx[1])
        p_lat = split_proj(a_lat @ w_in[layer])
        p_ctx = split_proj(a_ctx @ w_in[layer])
        y_a = deltanet_mixer(p_lat, p_ctx, dn_conv[layer], dn_a_log[layer], dn_dt_bias[layer],
                             dn_norm_g[layer], with_ctx)
        y_b = s5_mixer(p_lat["s5_u"], p_ctx["s5_u"], s5_a_re[layer], s5_a_im[layer], s5_log_dt[layer],
                       s5_b_re[layer], s5_b_im[layer], s5_c_re[layer], s5_c_im[layer], s5_d[layer],
                       s5_glu_w[layer], s5_glu_b[layer], with_ctx)
        y_c = diff_attention(p_lat["da_q"], p_lat["da_k"], p_lat["da_v"],
                             p_ctx["da_q"], p_ctx["da_k"], p_ctx["da_v"],
                             da_lambda[layer], da_norm_g[layer], lam_init, cos, sin, with_ctx)
        y_d = gla_mixer(p_lat, p_ctx, gla_w_gate[layer], gla_b_gate[layer], gla_norm_g[layer], with_ctx)
        merge_w = (w_branch[layer], w_gate[layer], b_gate[layer], w_out[layer])
        h_lat = h_lat + mod_lat[2] * merge_branches(a_lat, (y_a[0], y_b[0], y_c[0], y_d[0]), *merge_w)

        expert_w = (w_e_gate[layer], w_e_up[layer], w_e_down[layer])
        f_lat = modulate(rmsnorm(h_lat, norm2_g[layer]), mod_lat[3], mod_lat[4])
        if with_ctx:
            h_ctx = h_ctx + mod_ctx[2] * merge_branches(a_ctx, (y_a[1], y_b[1], y_c[1], y_d[1]), *merge_w)
            f_ctx = modulate(rmsnorm(h_ctx, norm2_g[layer]), mod_ctx[3], mod_ctx[4])
            tokens = jnp.concatenate([f_ctx.reshape(-1, dm), f_lat.reshape(-1, dm)], axis=0)
            ffn = moe_ffn(tokens, w_router, b_router, *expert_w)
            h_ctx = h_ctx + mod_ctx[5] * ffn[:b_ * n_ctx].reshape(b_, n_ctx, dm)
            h_lat = h_lat + mod_lat[5] * ffn[b_ * n_ctx:].reshape(b_, s_, dm)
        else:
            ffn = moe_ffn(f_lat.reshape(-1, dm), w_router, b_router, *expert_w)
            h_lat = h_lat + mod_lat[5] * ffn.reshape(b_, s_, dm)
    return rmsnorm(h_lat, final_g)
```

```python
import functools
import math

import jax
import jax.numpy as jnp
from jax import lax
from jax.experimental import pallas as pl
from jax.experimental.pallas import tpu as pltpu

F32 = jnp.float32
BF16 = jnp.bfloat16
HI = lax.Precision.HIGHEST
SDS = jax.ShapeDtypeStruct

GRID_W = 64
DN_HEADS, DN_DK, DN_DV, DN_CONV, DN_CHUNK = 4, 64, 64, 5, 64
S5_WIDTH, S5_GROUP, S5_STATE = 256, 16, 64
S5_GROUPS = S5_WIDTH // S5_GROUP
S5_CHUNK = 16
DA_HEADS, DA_DK, DA_DV = 4, 32, 64
ROPE_BASE = 10000.0
GLA_HEADS, GLA_DK, GLA_DV, GLA_RANK, GLA_TAU, GLA_CHUNK = 4, 32, 64, 16, 16.0, 16
N_EXPERTS, N_GROUPS, TOP_K = 16, 4, 2
EPG = N_EXPERTS // N_GROUPS
NORM_EPS = 1e-6
VMEM_LIMIT = 56 * 1024 * 1024

SEG = dict(dn_qkv=(0, 768), dn_z=(768, 256), s5_u=(1024, 256), da_qkv=(1280, 768),
           gla_qkv=(2048, 512), gla_r=(2560, 256), small=(2816, 128))
D_PROJ = 2944
SM_DN_A, SM_DN_B, SM_GLA = 0, 8, 16


def _cp(*sem):
    return pltpu.CompilerParams(dimension_semantics=sem, vmem_limit_bytes=VMEM_LIMIT)


def _nt(a, b, **kw):
    return lax.dot_general(a, b, (((1,), (1,)), ((), ())), preferred_element_type=F32, **kw)


def _tn(a, b, **kw):
    return lax.dot_general(a, b, (((0,), (0,)), ((), ())), preferred_element_type=F32, **kw)


def _dot(a, b, **kw):
    return jnp.dot(a, b, preferred_element_type=F32, **kw)


def _bdot(a, b):
    return jnp.dot(a.astype(BF16), b.astype(BF16), preferred_element_type=F32)


def _iota(shape, dim):
    return lax.broadcasted_iota(jnp.int32, shape, dim)


def _block_ones(n, blk_shift):
    r, c = _iota((n, n), 0), _iota((n, n), 1)
    return ((r >> blk_shift) == (c >> blk_shift)).astype(F32)


def _head_rms(o, gain, eps, ones64):
    ms = _dot(o * o, ones64, precision=HI) * (1.0 / 64.0)
    return o * lax.rsqrt(ms + eps) * gain


def _mod_body(c_ref, w_ref, b_ref, o_ref):
    c = c_ref[...]
    s = c * jax.nn.sigmoid(c)
    o_ref[0] = _dot(s, w_ref[0], precision=HI) + b_ref[0]


def _mod_call(cvec, w_mod, b_mod):
    n_layer, d, d6 = w_mod.shape
    tn = 1536
    return pl.pallas_call(
        _mod_body, grid=(n_layer, d6 // tn),
        in_specs=[pl.BlockSpec((8, d), lambda l, j: (0, 0)),
                  pl.BlockSpec((1, d, tn), lambda l, j: (l, 0, j)),
                  pl.BlockSpec((1, 1, tn), lambda l, j: (l, 0, j))],
        out_specs=pl.BlockSpec((1, 8, tn), lambda l, j: (l, 0, j)),
        out_shape=SDS((n_layer, 8, d6), F32), compiler_params=_cp("parallel", "parallel"),
        name="adaln_mod")(cvec, w_mod, b_mod.reshape(n_layer, 1, d6))


def _modulated_norm(x, gain, mv, is_ctx, k_shift, k_scale):
    y = x * lax.rsqrt(jnp.mean(x * x, axis=-1, keepdims=True) + NORM_EPS) * gain
    shift = jnp.where(is_ctx, mv[8 + k_shift:9 + k_shift], mv[k_shift:k_shift + 1])
    scale = jnp.where(is_ctx, mv[8 + k_scale:9 + k_scale], mv[k_scale:k_scale + 1])
    return y * (1.0 + scale) + shift


def _proj_body(n_ctx, tm, h_ref, mv_ref, g_ref, w_ref, a_ref, *out_refs):
    t = pl.program_id(1)
    row = t * tm + _iota((tm, 1), 0)
    a = _modulated_norm(h_ref[0], g_ref[...], mv_ref[0], row < n_ctx, 0, 1)
    ab = a.astype(BF16)
    a_ref[0] = ab
    p = _dot(ab, w_ref[...])
    for ref, (off, width) in zip(out_refs, SEG.values()):
        ref[0] = p[:, off:off + width]


def _proj_call(h, modv, gain, w_perm, n_ctx):
    b, t, d = h.shape
    tm = 384
    assert t % tm == 0
    outs = [SDS((b, t, d), BF16)] + [SDS((b, t, w), F32) for _, w in SEG.values()]
    out_specs = [pl.BlockSpec((1, tm, d), lambda i, j: (i, j, 0))] + [
        pl.BlockSpec((1, tm, w), lambda i, j: (i, j, 0)) for _, w in SEG.values()]
    return pl.pallas_call(
        functools.partial(_proj_body, n_ctx, tm), grid=(b, t // tm),
        in_specs=[pl.BlockSpec((1, tm, d), lambda i, j: (i, j, 0)),
                  pl.BlockSpec((1, 16, d), lambda i, j: (i, 0, 0)),
                  pl.BlockSpec((1, d), lambda i, j: (0, 0)),
                  pl.BlockSpec((d, D_PROJ), lambda i, j: (0, 0))],
        out_specs=out_specs, out_shape=outs, compiler_params=_cp("parallel", "parallel"),
        name="norm_proj")(h, modv, gain, w_perm)


def _dn_prep_body(nct, ntile, tm, x_ref, pv_ref, nx_ref, sm_ref, w_ref, alog_ref, dtb_ref,
                  q_ref, k_ref, v_ref, gb_ref):
    t = pl.program_id(1)
    first = jnp.logical_or(t == 0, t == nct)
    last = jnp.logical_or(t == nct - 1, t == ntile - 1)
    x = x_ref[0]
    pv = jnp.where(first, 0.0, pv_ref[0])
    nx = jnp.where(last, 0.0, nx_ref[0])
    w = w_ref[...]
    row = _iota((tm, 1), 0)
    xm1 = jnp.where(row == 0, pv[7:8], pltpu.roll(x, 1, 0))
    xm2 = jnp.where(row == 0, pv[6:7], jnp.where(row == 1, pv[7:8], pltpu.roll(x, 2, 0)))
    xp1 = jnp.where(row == tm - 1, nx[0:1], pltpu.roll(x, tm - 1, 0))
    xp2 = jnp.where(row == tm - 2, nx[0:1], jnp.where(row == tm - 1, nx[1:2], pltpu.roll(x, tm - 2, 0)))
    acc = w[0:1] * xm2 + w[1:2] * xm1 + w[2:3] * x + w[3:4] * xp1 + w[4:5] * xp2
    y = acc * jax.nn.sigmoid(acc)
    ones64 = _block_ones(256, 6)
    q, k = y[:, 0:256], y[:, 256:512]
    qs = _dot(q * q, ones64, precision=HI)
    ks = _dot(k * k, ones64, precision=HI)
    q_ref[0] = q * lax.rsqrt(qs + 1e-6) * (DN_DK ** -0.5)
    k_ref[0] = k * lax.rsqrt(ks + 1e-6)
    v_ref[0] = y[:, 512:768]
    sm = sm_ref[0]
    lane = _iota((1, 128), 1)
    g = -jnp.exp(alog_ref[...]) * jax.nn.softplus(sm + dtb_ref[...])
    be = jax.nn.sigmoid(sm)
    gb_ref[0] = jnp.where(lane < 8, g, jnp.where(lane < 16, be, 0.0))


def _dn_prep_call(qkv, small, conv_w, a_log, dt_bias, n_ctx):
    b, t, c = qkv.shape
    tm = 256
    assert n_ctx % tm == 0 and t % tm == 0
    nct, ntile, r8 = n_ctx // tm, t // tm, tm // 8
    w8 = jnp.zeros((8, c), F32).at[:DN_CONV].set(conv_w)
    alog = jnp.zeros((1, 128), F32).at[0, :8].set(a_log.reshape(-1))
    dtb = jnp.zeros((1, 128), F32).at[0, :8].set(dt_bias.reshape(-1))
    full = lambda shape: pl.BlockSpec(shape, lambda i, j: (0,) * len(shape))
    o256 = pl.BlockSpec((1, tm, 256), lambda i, j: (i, j, 0))
    return pl.pallas_call(
        functools.partial(_dn_prep_body, nct, ntile, tm), grid=(b, ntile),
        in_specs=[pl.BlockSpec((1, tm, c), lambda i, j: (i, j, 0)),
                  pl.BlockSpec((1, 8, c), lambda i, j: (i, jnp.maximum(j * r8 - 1, 0), 0)),
                  pl.BlockSpec((1, 8, c), lambda i, j: (i, jnp.minimum((j + 1) * r8, t // 8 - 1), 0)),
                  pl.BlockSpec((1, tm, 128), lambda i, j: (i, j, 0)),
                  full((8, c)), full((1, 128)), full((1, 128))],
        out_specs=[o256, o256, o256, pl.BlockSpec((1, tm, 128), lambda i, j: (i, j, 0))],
        out_shape=[SDS((b, t, 256), F32)] * 3 + [SDS((b, t, 128), F32)],
        compiler_params=_cp("parallel", "parallel"), name="dn_prep")(qkv, qkv, qkv, small, w8, alog, dtb)


def _dn_scan_body(q_ref, k_ref, v_ref, gb_ref, grow_ref, eg_ref, eb_ref, o_ref, s_ref):
    c = DN_CHUNK

    @pl.when(pl.program_id(1) == 0)
    def _():
        s_ref[...] = jnp.zeros_like(s_ref)

    q, k, v, gb = q_ref[0], k_ref[0], v_ref[0], gb_ref[0]
    gexp = _dot(gb, eg_ref[0], precision=HI)
    bexp = _dot(gb, eb_ref[0], precision=HI)
    tril = (_iota((c, c), 1) <= _iota((c, c), 0)).astype(F32)
    gc = _dot(tril, gexp, precision=HI)
    gl = gc[c - 1:c, :]
    rr, cc = _iota((256, 256), 0), _iota((256, 256), 1)
    same_head = (rr >> 6) == (cc >> 6)
    triu_bd = jnp.logical_and(same_head, (rr & 63) <= (cc & 63)).astype(F32)
    gcr = _dot(grow_ref[0, 0], triu_bd, precision=HI)[0:1, :]
    i_idx = _iota((c, 256), 0)
    j_idx = _iota((c, 256), 1) & 63
    lane_h = _iota((c, 256), 1) >> 6
    incl, strict = i_idx >= j_idx, i_idx > j_idx
    decay = jnp.where(incl, jnp.exp(jnp.where(incl, gc - gcr, 0.0)), 0.0)

    def stack4(x):
        return jnp.concatenate([jnp.where(lane_h == h, x, 0.0) for h in range(DN_HEADS)], axis=0).astype(BF16)

    def to_bd(xcat):
        return jnp.where(same_head, jnp.concatenate([xcat] * DN_HEADS, axis=0), 0.0).astype(BF16)

    kb = k * bexp
    k4 = stack4(k)
    lower = jnp.where(strict, _nt(kb.astype(BF16), k4) * decay, 0.0)
    a_intra = jnp.where(incl, _nt(q.astype(BF16), k4) * decay, 0.0)
    p = (i_idx == j_idx).astype(F32) - lower
    m = _dot(lower.astype(BF16), to_bd(lower))
    for it in range(5):
        mb = to_bd(m)
        p = p + _dot(p.astype(BF16), mb)
        if it < 4:
            m = _dot(m.astype(BF16), mb)
    egc = jnp.exp(gc)
    pb = p.astype(BF16)
    u = _dot(pb, stack4(v * bexp))
    w = _dot(pb, stack4(kb * egc))
    s = s_ref[...]
    sb = s.astype(BF16)
    v_new = u - _dot(w.astype(BF16), sb)
    o_ref[0] = _dot((q * egc).astype(BF16), sb) + _dot(a_intra.astype(BF16), stack4(v_new))
    k_dec = k * jnp.exp(gl - gc)
    upd = _tn(k_dec.astype(BF16), v_new.astype(BF16))
    s_ref[...] = s * jnp.exp(gl) + jnp.where(same_head, upd, 0.0)


def _dn_scan_call(q2, k2, v2, gb2, grow, eg, eb, b):
    bb, t, _ = q2.shape
    c = DN_CHUNK
    n = t // c
    blk = lambda w: pl.BlockSpec((1, c, w), lambda i, j: (i, j, 0))
    return pl.pallas_call(
        _dn_scan_body, grid=(bb, n),
        in_specs=[blk(256), blk(256), blk(256), blk(128),
                  pl.BlockSpec((1, 1, 8, 256), lambda i, j: (i, j, 0, 0)),
                  pl.BlockSpec((1, 128, 256), lambda i, j: (i // b, 0, 0)),
                  pl.BlockSpec((1, 128, 256), lambda i, j: (i // b, 0, 0))],
        out_specs=blk(256), out_shape=SDS((bb, t, 256), F32),
        scratch_shapes=[pltpu.VMEM((256, 256), F32)],
        compiler_params=_cp("parallel", "arbitrary"), name="dn_scan")(q2, k2, v2, gb2, grow, eg, eb)


def _row_is_rev(shape):
    return ((_iota(shape, 0) >> 2) & 1) == 1


def _s5_in_body(nb, u_ref, mb_ref, o_ref):
    r = _dot(u_ref[0], mb_ref[0])
    rev = _row_is_rev((r.shape[0], 128)) if nb == 4 else None
    o_ref[...] = jnp.where(rev, r[:, 128:], r[:, :128])


def _s5_scan_body(nsteps, bc_ref, al_ref, x_ref, st_ref):
    @pl.when(pl.program_id(0) == 0)
    def _():
        st_ref[...] = jnp.zeros_like(st_ref)

    ar, ai = al_ref[0], al_ref[1]
    lanes = ar.shape[1]
    is_re = (_iota((8, lanes), 1) & 127) < 64

    def step(i, x):
        rows = pl.ds(pl.multiple_of(i * 8, 8), 8)
        x_ref[rows, :] = x
        swapped = jnp.where(is_re, pltpu.roll(x, lanes - 64, 1), pltpu.roll(x, 64, 1))
        return x * ar + swapped * ai + bc_ref[rows, :]

    st_ref[...] = lax.fori_loop(0, nsteps, step, st_ref[...])


def _s5_out_body(u_ref, x_ref, mi_ref, mc_ref, y_ref):
    ri = _dot(u_ref[0], mi_ref[0])
    rc = _dot(x_ref[...].astype(BF16), mc_ref[0])
    r = ri + rc
    rev = _row_is_rev((r.shape[0], 256))
    y_ref[0] = jnp.where(rev, r[:, 256:], r[:, :256])


def _s5_call(u5, m_b, m_i, m_c, al, nb):
    g, r8, _ = u5.shape
    assert nb == 4
    bc = pl.pallas_call(
        functools.partial(_s5_in_body, nb), grid=(g,),
        in_specs=[pl.BlockSpec((1, r8, 256), lambda i: (i, 0, 0)),
                  pl.BlockSpec((1, 256, 256), lambda i: (i, 0, 0))],
        out_specs=pl.BlockSpec((r8, 128), lambda i: (0, i)),
        out_shape=SDS((r8, g * 128), F32), compiler_params=_cp("parallel"), name="s5_in")(u5, m_b)
    n = r8 // 8
    nsteps = 1
    for cand in (66, 64, 48, 33, 32, 24, 16, 12, 11, 8, 6, 4, 3, 2):
        if n % cand == 0:
            nsteps = cand
            break
    xprev = pl.pallas_call(
        functools.partial(_s5_scan_body, nsteps), grid=(n // nsteps,),
        in_specs=[pl.BlockSpec((nsteps * 8, g * 128), lambda i: (i, 0)),
                  pl.BlockSpec((2, 8, g * 128), lambda i: (0, 0, 0))],
        out_specs=pl.BlockSpec((nsteps * 8, g * 128), lambda i: (i, 0)),
        out_shape=SDS((r8, g * 128), F32), scratch_shapes=[pltpu.VMEM((8, g * 128), F32)],
        compiler_params=_cp("arbitrary"), name="s5_scan")(bc, al)
    return pl.pallas_call(
        _s5_out_body, grid=(g,),
        in_specs=[pl.BlockSpec((1, r8, 256), lambda i: (i, 0, 0)),
                  pl.BlockSpec((r8, 128), lambda i: (0, i)),
                  pl.BlockSpec((1, 256, 512), lambda i: (i, 0, 0)),
                  pl.BlockSpec((1, 128, 512), lambda i: (i, 0, 0))],
        out_specs=pl.BlockSpec((1, r8, 256), lambda i: (i, 0, 0)),
        out_shape=SDS((g, r8, 256), F32), compiler_params=_cp("parallel"), name="s5_out")(u5, xprev, m_i, m_c)


def _s5_operators(a_re, a_im, log_dt, b_re, b_im, c_re, c_im):
    L = S5_CHUNK
    G, P, C = S5_GROUPS, S5_STATE, S5_GROUP
    dt = jnp.exp(log_dt)[:, :, None]

    def apow(l):
        mag = jnp.exp(a_re * dt * l)
        return mag * jnp.cos(a_im * dt * l), mag * jnp.sin(a_im * dt * l)

    ab_re, ab_im = apow(1.0)
    den = a_re * a_re + a_im * a_im
    f_re = ((ab_re - 1.0) * a_re + ab_im * a_im) / den
    f_im = (ab_im * a_re - (ab_re - 1.0) * a_im) / den
    bb_re = f_re[..., None] * b_re[None] - f_im[..., None] * b_im[None]
    bb_im = f_re[..., None] * b_im[None] + f_im[..., None] * b_re[None]
    ls = jnp.arange(L + 1, dtype=F32)
    mag = jnp.exp(a_re[..., None] * dt[..., None] * ls)
    ang = a_im[..., None] * dt[..., None] * ls
    pw_re, pw_im = mag * jnp.cos(ang), mag * jnp.sin(ang)
    ca_re = c_re[..., None] * pw_re[:, :, None] - c_im[..., None] * pw_im[:, :, None]
    ca_im = c_re[..., None] * pw_im[:, :, None] + c_im[..., None] * pw_re[:, :, None]
    kern = (jnp.einsum("dgopl,dgpc->dglco", ca_re, bb_re, precision=HI)
            - jnp.einsum("dgopl,dgpc->dglco", ca_im, bb_im, precision=HI))
    j_in, j_out = jnp.arange(L)[:, None], jnp.arange(L)[None, :]
    lag = j_out - j_in
    m_i = jnp.where((lag >= 0)[None, None, :, None, :, None],
                    kern[:, :, jnp.clip(lag, 0, L)].transpose(0, 1, 2, 4, 3, 5), 0.0)
    m_i = m_i.reshape(2, G, L * C, L * C)
    rp_re, rp_im = pw_re[..., L - 1 - jnp.arange(L)], pw_im[..., L - 1 - jnp.arange(L)]
    mb_re = rp_re[..., None] * bb_re[:, :, :, None] - rp_im[..., None] * bb_im[:, :, :, None]
    mb_im = rp_re[..., None] * bb_im[:, :, :, None] + rp_im[..., None] * bb_re[:, :, :, None]
    m_b = jnp.concatenate([mb_re.transpose(0, 1, 3, 4, 2), mb_im.transpose(0, 1, 3, 4, 2)], axis=-1)
    m_b = m_b.reshape(2, G, L * C, 2 * P)
    w_re, w_im = ca_re[..., 1:], ca_im[..., 1:]
    m_c = jnp.concatenate([w_re.transpose(0, 1, 3, 4, 2), -w_im.transpose(0, 1, 3, 4, 2)], axis=2)
    m_c = m_c.reshape(2, G, 2 * P, L * C)
    al_re = jnp.concatenate([pw_re[..., L], pw_re[..., L]], axis=-1).reshape(2, G * 2 * P)
    al_im = jnp.concatenate([-pw_im[..., L], pw_im[..., L]], axis=-1).reshape(2, G * 2 * P)
    cat_dir = lambda m: jnp.concatenate([m[0], m[1]], axis=-1)
    return cat_dir(m_b).astype(BF16), cat_dir(m_i).astype(BF16), cat_dir(m_c).astype(BF16), al_re, al_im


def _da_prep_body(x_ref, cos_ref, sa_ref, sb_ref, q_ref, k_ref, v_ref):
    x = x_ref[0]
    cos, sa, sb = cos_ref[...], sa_ref[...], sb_ref[...]

    def rope(z):
        return z * cos + pltpu.roll(z, 240, 1) * sa + pltpu.roll(z, 16, 1) * sb

    q_ref[0] = (rope(x[:, 0:256]) * (DA_DK ** -0.5)).astype(BF16)
    k_ref[0] = rope(x[:, 256:512]).astype(BF16)
    v_ref[0] = x[:, 512:768].astype(BF16)


def _da_prep_call(qkv, cos, sa, sb):
    b, t, _ = qkv.shape
    tm = 768
    tab = pl.BlockSpec((tm, 256), lambda i, j: (j, 0))
    o = pl.BlockSpec((1, tm, 256), lambda i, j: (i, j, 0))
    return pl.pallas_call(
        _da_prep_body, grid=(b, t // tm),
        in_specs=[pl.BlockSpec((1, tm, 768), lambda i, j: (i, j, 0)), tab, tab, tab],
        out_specs=[o, o, o], out_shape=[SDS((b, t, 256), BF16)] * 3,
        compiler_params=_cp("parallel", "parallel"), name="da_prep")(qkv, cos, sa, sb)


def _attn_body(nkb, tk, tq, post, q_ref, k_ref, v_ref, lam_ref, g_ref, o_ref, acc_ref, m_ref, l_ref):
    q = q_ref[0]
    lane = _iota((tq, 256), 1)
    lane_h = lane >> 6
    acc_ref[...] = jnp.zeros_like(acc_ref)
    m_ref[...] = jnp.full_like(m_ref, -1e30)
    l_ref[...] = jnp.zeros_like(l_ref)
    qm = [jnp.where((lane >> 5) == hm, q, jnp.zeros_like(q)) for hm in range(8)]

    def kblock(i, carry):
        rows = pl.ds(pl.multiple_of(i * tk, 8), tk)
        kb = k_ref[0, rows, :]
        vb = v_ref[0, rows, :]
        for mp in range(2):
            upd = jnp.zeros((tq, 256), F32)
            alpha_l = jnp.zeros((tq, 256), F32)
            for h in range(DA_HEADS):
                hm = 2 * h + mp
                s = _nt(qm[hm], kb)
                m_old = m_ref[:, hm:hm + 1]
                m_new = jnp.maximum(m_old, jnp.max(s, axis=1, keepdims=True))
                alpha = jnp.exp(m_old - m_new)
                p = jnp.exp(s - m_new)
                l_ref[:, hm:hm + 1] = l_ref[:, hm:hm + 1] * alpha + jnp.sum(p, axis=1, keepdims=True)
                m_ref[:, hm:hm + 1] = m_new
                pv = _dot(p.astype(BF16), vb)
                upd = jnp.where(lane_h == h, pv, upd)
                alpha_l = jnp.where(lane_h == h, alpha, alpha_l)
            acc_ref[mp] = acc_ref[mp] * alpha_l + upd
        return carry

    lax.fori_loop(0, nkb, kblock, 0)
    outs = []
    for mp in range(2):
        linv = jnp.zeros((tq, 256), F32)
        for h in range(DA_HEADS):
            linv = jnp.where(lane_h == h, 1.0 / l_ref[:, 2 * h + mp:2 * h + mp + 1], linv)
        outs.append(acc_ref[mp] * linv)
    o = outs[0] - lam_ref[...] * outs[1]
    o_ref[0] = _head_rms(o, g_ref[...], 1e-5, _block_ones(256, 6)) * post


def _attn_call(q, k, v, lam_row, gain, post, q_off, n_q, n_keys, tq, tk):
    b = q.shape[0]
    assert n_keys % tk == 0 and n_q % tq == 0 and q_off % tq == 0
    qo = q_off // tq
    return pl.pallas_call(
        functools.partial(_attn_body, n_keys // tk, tk, tq, post), grid=(b, n_q // tq),
        in_specs=[pl.BlockSpec((1, tq, 256), lambda i, j: (i, j + qo, 0)),
                  pl.BlockSpec((1, n_keys, 256), lambda i, j: (i, 0, 0)),
                  pl.BlockSpec((1, n_keys, 256), lambda i, j: (i, 0, 0)),
                  pl.BlockSpec((1, 256), lambda i, j: (0, 0)),
                  pl.BlockSpec((1, 256), lambda i, j: (0, 0))],
        out_specs=pl.BlockSpec((1, tq, 256), lambda i, j: (i, j, 0)),
        out_shape=SDS((b, n_q, 256), F32),
        scratch_shapes=[pltpu.VMEM((2, tq, 256), F32), pltpu.VMEM((tq, 128), F32), pltpu.VMEM((tq, 128), F32)],
        compiler_params=_cp("parallel", "parallel"), name="diff_attn")(q, k, v, lam_row, gain)


def _gla_body(nchunk, qkv_ref, sm_ref, wg_ref, bg_ref, o_ref, st_ref):
    c = GLA_CHUNK

    @pl.when(pl.program_id(1) == 0)
    def _():
        st_ref[...] = jnp.zeros_like(st_ref)

    tril = (_iota((c, c), 1) <= _iota((c, c), 0)).astype(F32)
    row = _iota((c, 128), 0)
    head_ones = ((_iota((128, 256), 0) >> 5) == (_iota((128, 256), 1) >> 6)).astype(BF16)
    st_mask = (_iota((256, 128), 0) >> 6) == (_iota((256, 128), 1) >> 5)
    wg, bg = wg_ref[0], bg_ref[0]

    def chunk(ci, st):
        rows = pl.ds(pl.multiple_of(ci * c, c), c)
        x = qkv_ref[0, rows, :]
        q, k, v = x[:, 0:128] * (GLA_DK ** -0.5), x[:, 128:256], x[:, 256:512]
        z = _dot(sm_ref[0, rows, :], wg, precision=HI) + bg
        la = jax.nn.log_sigmoid(z) * (1.0 / GLA_TAU)
        b = _dot(tril, la, precision=HI)
        bl = b[c - 1:c, :]
        parts = []
        for j in range(c):
            ok = row >= j
            e = jnp.where(ok, jnp.exp(jnp.where(ok, b - b[j:j + 1, :], 0.0)), 0.0)
            parts.append(e * q * k[j:j + 1, :])
        sb = _dot(jnp.concatenate(parts, axis=0).astype(BF16), head_ones)
        o = _nt((q * jnp.exp(b)).astype(BF16), st.astype(BF16))
        for j in range(c):
            o = o + sb[j * c:(j + 1) * c, :] * v[j:j + 1, :]
        o_ref[0, rows, :] = o
        upd = _tn(v.astype(BF16), (k * jnp.exp(bl - b)).astype(BF16))
        return st * jnp.exp(bl) + jnp.where(st_mask, upd, 0.0)

    st_ref[...] = lax.fori_loop(0, nchunk, chunk, st_ref[...])


def _gla_call(qkv2, small2, wg2, bg2, b):
    bb, t, _ = qkv2.shape
    tm = 256
    assert t % tm == 0
    return pl.pallas_call(
        functools.partial(_gla_body, tm // GLA_CHUNK), grid=(bb, t // tm),
        in_specs=[pl.BlockSpec((1, tm, 512), lambda i, j: (i, j, 0)),
                  pl.BlockSpec((1, tm, 128), lambda i, j: (i, j, 0)),
                  pl.BlockSpec((1, 128, 128), lambda i, j: (i // b, 0, 0)),
                  pl.BlockSpec((1, 1, 128), lambda i, j: (i // b, 0, 0))],
        out_specs=pl.BlockSpec((1, tm, 256), lambda i, j: (i, j, 0)),
        out_shape=SDS((bb, t, 256), F32), scratch_shapes=[pltpu.VMEM((256, 128), F32)],
        compiler_params=_cp("parallel", "arbitrary"), name="gla_scan")(qkv2, small2, wg2, bg2)


def _merge_body(n_ctx, tm, a_ref, h_ref, mv_ref, dnf_ref, dnr_ref, dnz_ref, s5y_ref, s5yr_ref, s5u_ref,
                da_ref, glf_ref, glr_ref, glz_ref, vec_ref, gluw_ref, wgate_ref, bgate_ref, wbr_ref,
                wout_ref, n2g_ref, wr_ref, br_ref, h_out, f_out, wc_out):
    t = pl.program_id(1)
    d = h_ref.shape[2]
    row = t * tm + _iota((tm, 1), 0)
    is_ctx = row < n_ctx
    mv = mv_ref[0]
    vec = vec_ref[...]
    ones64 = _block_ones(256, 6)
    z = dnz_ref[0]
    ya = _head_rms(dnf_ref[0] + dnr_ref[0], vec[0:1], NORM_EPS, ones64) * (z * jax.nn.sigmoid(z))
    u = s5u_ref[0]
    zz = jax.nn.gelu(s5y_ref[0] + s5yr_ref[0] + vec[2:3] * u)
    yb = zz * jax.nn.sigmoid(_bdot(zz, gluw_ref[...]) + vec[3:4])
    r = glz_ref[0]
    yd = _head_rms(glf_ref[0] + glr_ref[0], vec[1:2], NORM_EPS, ones64) * (r * jax.nn.sigmoid(r))
    a = a_ref[0]
    acc = jnp.zeros((tm, d), F32)
    for i, y in enumerate((ya, yb, da_ref[0], yd)):
        gate = jax.nn.sigmoid(_dot(a, wgate_ref[:, i * d:(i + 1) * d]) + bgate_ref[:, i * d:(i + 1) * d])
        acc = acc + gate * _dot(y.astype(BF16), wbr_ref[i])
    out = _dot(acc.astype(BF16), wout_ref[...])
    gate2 = jnp.where(is_ctx, mv[10:11], mv[2:3])
    h_new = h_ref[0] + gate2 * out
    h_out[0] = h_new
    f = _modulated_norm(h_new, n2g_ref[...], mv, is_ctx, 3, 4)
    f_out[0] = f.astype(BF16)
    sc = jax.nn.sigmoid(_nt(wr_ref[...], f, precision=HI))
    gr = sc + br_ref[...]
    rows_ = [gr[e:e + 1, :] for e in range(N_EXPERTS)]
    best_v, best_g = None, None
    for g in range(N_GROUPS):
        x = rows_[g * EPG:(g + 1) * EPG]
        top2 = None
        for i in range(EPG):
            for j in range(i + 1, EPG):
                s_ij = x[i] + x[j]
                top2 = s_ij if top2 is None else jnp.maximum(top2, s_ij)
        if best_v is None:
            best_v, best_g = top2, jnp.zeros_like(top2, dtype=jnp.int32)
        else:
            take = top2 > best_v
            best_v = jnp.where(take, top2, best_v)
            best_g = jnp.where(take, g, best_g)
    cand = []
    for kk in range(EPG):
        ck = rows_[kk]
        for g in range(1, N_GROUPS):
            ck = jnp.where(best_g == g, rows_[g * EPG + kk], ck)
        cand.append(ck)
    v1, i1 = cand[0], jnp.zeros_like(best_g)
    for kk in range(1, EPG):
        take = cand[kk] > v1
        v1 = jnp.where(take, cand[kk], v1)
        i1 = jnp.where(take, kk, i1)
    v2, i2 = jnp.full_like(v1, -jnp.inf), jnp.zeros_like(best_g)
    for kk in range(EPG):
        take = jnp.logical_and(i1 != kk, cand[kk] > v2)
        v2 = jnp.where(take, cand[kk], v2)
        i2 = jnp.where(take, kk, i2)
    e1, e2 = best_g * EPG + i1, best_g * EPG + i2
    e_iota = _iota((N_EXPERTS, tm), 0)
    s1 = jnp.sum(jnp.where(e_iota == e1, sc, 0.0), axis=0, keepdims=True)
    s2 = jnp.sum(jnp.where(e_iota == e2, sc, 0.0), axis=0, keepdims=True)
    tot = s1 + s2
    wt = jnp.where(e_iota == e1, s1 / tot, jnp.where(e_iota == e2, s2 / tot, 0.0))
    eye = (_iota((N_EXPERTS, N_EXPERTS), 0) == _iota((N_EXPERTS, N_EXPERTS), 1)).astype(F32)
    wc_out[0] = _tn(wt, eye, precision=HI)


def _merge_call(a, h, modv, dn_o2, dn_or, dn_z, s5_y, s5_yr, s5_u, da_y, gl_o2, gl_or, gl_r, vec, glu_w,
                w_gate, b_gate, w_branch, w_out, n2g, w_rt, b_rt, n_ctx):
    b, t, d = h.shape
    tm = 256
    assert t % tm == 0
    tok = lambda w: pl.BlockSpec((1, tm, w), lambda i, j: (i, j, 0))
    full = lambda *shape: pl.BlockSpec(shape, lambda i, j: (0,) * len(shape))
    return pl.pallas_call(
        functools.partial(_merge_body, n_ctx, tm), grid=(b, t // tm),
        in_specs=[tok(d), tok(d), pl.BlockSpec((1, 16, d), lambda i, j: (i, 0, 0)),
                  tok(256), tok(256), tok(256), tok(256), tok(256), tok(256), tok(256), tok(256), tok(256),
                  tok(256), full(8, 256), full(256, 256), full(d, 4 * d), full(1, 4 * d),
                  full(4, 256, d), full(d, d), full(1, d), full(N_EXPERTS, d), full(N_EXPERTS, 1)],
        out_specs=[tok(d), tok(d), tok(N_EXPERTS)],
        out_shape=[SDS((b, t, d), F32), SDS((b, t, d), BF16), SDS((b, t, N_EXPERTS), F32)],
        compiler_params=_cp("parallel", "parallel"), name="merge_router")(
            a, h, modv, dn_o2, dn_or, dn_z, s5_y, s5_yr, s5_u, da_y, gl_o2, gl_or, gl_r, vec, glu_w,
            w_gate, b_gate, w_branch, w_out, n2g, w_rt, b_rt)


def _moe_body(n_ctx, tt, final, f_ref, wc_ref, h_ref, mv_ref, wg_ref, wu_ref, wd_ref, fg_ref, o_ref, acc_ref):
    e = pl.program_id(2)

    @pl.when(e == 0)
    def _():
        acc_ref[...] = jnp.zeros_like(acc_ref)

    x = f_ref[0]
    wc = wc_ref[0]
    wcol = jnp.sum(jnp.where(_iota(wc.shape, 1) == e, wc, 0.0), axis=1, keepdims=True)
    g = _dot(x, wg_ref[0])
    hid = g * jax.nn.sigmoid(g) * _dot(x, wu_ref[0])
    acc_ref[...] += _dot(jnp.where(wcol != 0.0, hid * wcol, 0.0).astype(BF16), wd_ref[0])

    @pl.when(e == N_EXPERTS - 1)
    def _():
        row = pl.program_id(1) * tt + _iota((tt, 1), 0)
        mv = mv_ref[0]
        gate = jnp.where(row < n_ctx, mv[13:14], mv[5:6])
        hn = h_ref[0] + gate * acc_ref[...]
        if final:
            hn = hn * lax.rsqrt(jnp.mean(hn * hn, axis=-1, keepdims=True) + NORM_EPS) * fg_ref[...]
        o_ref[0] = hn


def _moe_call(f, wc, h, modv, weg, weu, wed, final_g, n_ctx, final):
    b, t, d = h.shape
    fe = weg.shape[2]
    tt = next(c for c in (1408, 1024, 768, 512, 256, 128, 64, 32, 16, 8) if t % c == 0)
    tok = lambda w: pl.BlockSpec((1, tt, w), lambda i, j, e: (i, j, 0))
    return pl.pallas_call(
        functools.partial(_moe_body, n_ctx, tt, final), grid=(b, t // tt, N_EXPERTS),
        in_specs=[tok(d), tok(N_EXPERTS), tok(d), pl.BlockSpec((1, 16, d), lambda i, j, e: (i, 0, 0)),
                  pl.BlockSpec((1, d, fe), lambda i, j, e: (e, 0, 0)),
                  pl.BlockSpec((1, d, fe), lambda i, j, e: (e, 0, 0)),
                  pl.BlockSpec((1, fe, d), lambda i, j, e: (e, 0, 0)),
                  pl.BlockSpec((1, d), lambda i, j, e: (0, 0))],
        out_specs=tok(d), out_shape=SDS((b, t, d), F32), scratch_shapes=[pltpu.VMEM((tt, d), F32)],
        compiler_params=_cp("parallel", "parallel", "arbitrary"), name="moe_experts")(
            f, wc, h, modv, weg, weu, wed, final_g)


def _seg_flip(x, n_ctx):
    return jnp.concatenate([jnp.flip(x[:, :n_ctx], 1), jnp.flip(x[:, n_ctx:], 1)], axis=1)


def _rope_tables(n_ctx, s):
    t = jnp.arange(s)
    rowp, colp = (t // GRID_W).astype(F32), (t % GRID_W).astype(F32)
    n_freq = DA_DK // 4
    inv = ROPE_BASE ** (-jnp.arange(n_freq, dtype=F32) / n_freq)
    ang = jnp.concatenate([rowp[:, None] * inv, colp[:, None] * inv], axis=-1)
    cos, sin = jnp.cos(ang), jnp.sin(ang)
    zero = jnp.zeros_like(sin)
    cos32 = jnp.concatenate([cos, cos], axis=-1)
    sa32 = jnp.concatenate([-sin, zero], axis=-1)
    sb32 = jnp.concatenate([zero, sin], axis=-1)
    def full(tab, ctx_val):
        tab = jnp.tile(tab, (1, 8))
        return jnp.concatenate([jnp.full((n_ctx, 256), ctx_val, F32), tab], axis=0)
    return full(cos32, 1.0), full(sa32, 0.0), full(sb32, 0.0)


def _perm_w_in(w_in):
    offs, o = {}, 0
    for name, width in (("dn_qkv", 768), ("dn_z", 256), ("dn_a", 8), ("dn_b", 8), ("s5_u", 256), ("da_q", 256),
                        ("da_k", 256), ("da_v", 256), ("gla_q", 128), ("gla_k", 128), ("gla_v", 256),
                        ("gla_r", 256), ("gla_gate", 32)):
        offs[name] = (o, width)
        o += width
    col = lambda n: w_in[:, offs[n][0]:offs[n][0] + offs[n][1]]
    pad = jnp.zeros((w_in.shape[0], 128 - 48), w_in.dtype)
    return jnp.concatenate([col(n) for n in ("dn_qkv", "dn_z", "s5_u", "da_q", "da_k", "da_v", "gla_q", "gla_k",
                                              "gla_v", "gla_r", "dn_a", "dn_b", "gla_gate")] + [pad], axis=1)


def kernel(x, c, ctx, c_ctx, w_mod, b_mod, norm1_g, norm2_g, w_in, dn_conv, dn_a_log, dn_dt_bias, dn_norm_g,
           s5_a_re, s5_a_im, s5_log_dt, s5_b_re, s5_b_im, s5_c_re, s5_c_im, s5_d, s5_glu_w, s5_glu_b,
           da_lambda, da_norm_g, gla_w_gate, gla_b_gate, gla_norm_g, w_branch, w_gate, b_gate, w_out,
           w_router, b_router, w_e_gate, w_e_up, w_e_down, final_g):
    b, s, d = x.shape
    n_ctx = ctx.shape[1]
    t = n_ctx + s
    depth = w_mod.shape[0]
    assert b + 1 <= 8 and b == 4

    cvec = jnp.zeros((8, d), F32).at[:b].set(c).at[b].set(c_ctx)
    mod_all = _mod_call(cvec, w_mod, b_mod).reshape(depth, 8, 6, d)
    cos_t, sa_t, sb_t = _rope_tables(n_ctx, s)
    h = jnp.concatenate([ctx, x], axis=1)

    lane256 = jnp.arange(256) // 64
    eg = jnp.stack([(jnp.arange(128)[:, None] == (SM_DN_A + dd * 4 + lane256)[None, :]).astype(F32) for dd in range(2)])
    eb = jnp.stack([(jnp.arange(128)[:, None] == (SM_DN_B + dd * 4 + lane256)[None, :]).astype(F32) for dd in range(2)])

    for layer in range(depth):
        lam_init = 0.8 - 0.6 * math.exp(-0.3 * layer)
        ml = mod_all[layer]
        modv = jnp.zeros((b, 16, d), F32).at[:, 0:6].set(ml[:b]).at[:, 8:14].set(jnp.broadcast_to(ml[b], (b, 6, d)))

        a, dn_qkv, dn_z, s5_u, da_qkv, gla_qkv, gla_r, small = _proj_call(
            h, modv, norm1_g[layer][None, :], _perm_w_in(w_in[layer]).astype(BF16), n_ctx)

        dq, dk_, dv, gb = _dn_prep_call(dn_qkv, small, dn_conv[layer], dn_a_log[layer], dn_dt_bias[layer], n_ctx)
        both = lambda z: jnp.concatenate([z, _seg_flip(z, n_ctx)], axis=0)
        gb2 = both(gb)
        g_dir = jnp.concatenate([gb2[:b, :, 0:4], gb2[b:, :, 4:8]], axis=0)
        nck = t // DN_CHUNK
        grow = g_dir.reshape(2 * b, nck, DN_CHUNK, 4).transpose(0, 1, 3, 2).reshape(2 * b, nck, 1, 256)
        grow = jnp.concatenate([grow, jnp.zeros((2 * b, nck, 7, 256), F32)], axis=2)
        dn_o2 = _dn_scan_call(both(dq), both(dk_), both(dv), gb2, grow, eg, eb, b)
        dn_or = _seg_flip(dn_o2[b:], n_ctx)

        m_b, m_i, m_c, al_re, al_im = _s5_operators(
            s5_a_re[layer], s5_a_im[layer], s5_log_dt[layer], s5_b_re[layer], s5_b_im[layer],
            s5_c_re[layer], s5_c_im[layer])
        n5 = t // S5_CHUNK
        u2 = jnp.stack([s5_u, _seg_flip(s5_u, n_ctx)])
        u5 = u2.reshape(2, b, n5, S5_CHUNK, S5_GROUPS, S5_GROUP).transpose(4, 2, 0, 1, 3, 5)
        u5 = u5.reshape(S5_GROUPS, n5 * 2 * b, 256).astype(BF16)
        al = jnp.stack([jnp.repeat(al_re, b, axis=0), jnp.repeat(al_im, b, axis=0)])
        y5 = _s5_call(u5, m_b, m_i, m_c, al, b)
        y5 = y5.reshape(S5_GROUPS, n5, 2, b, S5_CHUNK, S5_GROUP).transpose(2, 3, 1, 4, 0, 5).reshape(2, b, t, 256)
        s5_y, s5_yr = y5[0], _seg_flip(y5[1], n_ctx)

        lf = da_lambda[layer]
        lam = jnp.exp(jnp.sum(lf[0] * lf[1])) - jnp.exp(jnp.sum(lf[2] * lf[3])) + lam_init
        lam_row = jnp.full((1, 256), lam, F32)
        da_gain = jnp.tile(da_norm_g[layer], DA_HEADS)[None, :]
        aq, ak, av = _da_prep_call(da_qkv, cos_t, sa_t, sb_t)
        da_lat = _attn_call(aq, ak, av, lam_row, da_gain, 1.0 - lam_init, n_ctx, s, t, 256, 768)
        if layer < depth - 1:
            da_ctx = _attn_call(aq, ak, av, lam_row, da_gain, 1.0 - lam_init, 0, n_ctx, n_ctx, 256, 256)
        else:
            da_ctx = jnp.zeros((b, n_ctx, 256), F32)
        da_y = jnp.concatenate([da_ctx, da_lat], axis=1)

        wg2 = jnp.zeros((2, 128, 128), F32)
        for dd in range(2):
            wg2 = wg2.at[dd, SM_GLA + dd * GLA_RANK:SM_GLA + (dd + 1) * GLA_RANK].set(gla_w_gate[layer, dd])
        gl_o2 = _gla_call(both(gla_qkv), both(small), wg2, gla_b_gate[layer][:, None, :], b)
        gl_or = _seg_flip(gl_o2[b:], n_ctx)

        vec = jnp.zeros((8, 256), F32)
        vec = vec.at[0].set(jnp.tile(dn_norm_g[layer], DN_HEADS)).at[1].set(jnp.tile(gla_norm_g[layer], GLA_HEADS))
        vec = vec.at[2].set(s5_d[layer]).at[3].set(s5_glu_b[layer])
        h, f, wc = _merge_call(
            a, h, modv, dn_o2, dn_or, dn_z, s5_y, s5_yr, s5_u, da_y, gl_o2, gl_or, gla_r, vec,
            s5_glu_w[layer].astype(BF16), w_gate[layer].astype(BF16), b_gate[layer][None, :],
            w_branch[layer].astype(BF16), w_out[layer].astype(BF16), norm2_g[layer][None, :],
            w_router.T, b_router[:, None], n_ctx)

        h = _moe_call(f, wc, h, modv, w_e_gate[layer].astype(BF16), w_e_up[layer].astype(BF16),
                      w_e_down[layer].astype(BF16), final_g[None, :], n_ctx, layer == depth - 1)
    return h[:, n_ctx:]
```

```python
import functools
import math

import jax
import jax.numpy as jnp
from jax import lax
from jax.experimental import pallas as pl
from jax.experimental.pallas import tpu as pltpu

F32 = jnp.float32
BF16 = jnp.bfloat16
HI = lax.Precision.HIGHEST
SDS = jax.ShapeDtypeStruct

GRID_W = 64
DN_HEADS, DN_DK, DN_DV, DN_CONV, DN_CHUNK = 4, 64, 64, 5, 64
S5_WIDTH, S5_GROUP, S5_STATE = 256, 16, 64
S5_GROUPS = S5_WIDTH // S5_GROUP
S5_CHUNK = 16
DA_HEADS, DA_DK, DA_DV = 4, 32, 64
ROPE_BASE = 10000.0
GLA_HEADS, GLA_DK, GLA_DV, GLA_RANK, GLA_TAU, GLA_CHUNK = 4, 32, 64, 16, 16.0, 16
N_EXPERTS, N_GROUPS, TOP_K = 16, 4, 2
EPG = N_EXPERTS // N_GROUPS
NORM_EPS = 1e-6
VMEM_LIMIT = 56 * 1024 * 1024

SEG = dict(dn_qkv=(0, 768), dn_z=(768, 256), s5_u=(1024, 256), da_qkv=(1280, 768),
           gla_qkv=(2048, 512), gla_r=(2560, 256), small=(2816, 128))
D_PROJ = 2944
SM_DN_A, SM_DN_B, SM_GLA = 0, 8, 16


def _cp(*sem):
    return pltpu.CompilerParams(dimension_semantics=sem, vmem_limit_bytes=VMEM_LIMIT)


def _nt(a, b, **kw):
    return lax.dot_general(a, b, (((1,), (1,)), ((), ())), preferred_element_type=F32, **kw)


def _tn(a, b, **kw):
    return lax.dot_general(a, b, (((0,), (0,)), ((), ())), preferred_element_type=F32, **kw)


def _dot(a, b, **kw):
    return jnp.dot(a, b, preferred_element_type=F32, **kw)


def _bdot(a, b):
    return jnp.dot(a.astype(BF16), b.astype(BF16), preferred_element_type=F32)


def _iota(shape, dim):
    return lax.broadcasted_iota(jnp.int32, shape, dim)


def _block_ones(n, blk_shift):
    r, c = _iota((n, n), 0), _iota((n, n), 1)
    return ((r >> blk_shift) == (c >> blk_shift)).astype(F32)


def _head_rms(o, gain, eps, ones64):
    ms = _dot(o * o, ones64, precision=HI) * (1.0 / 64.0)
    return o * lax.rsqrt(ms + eps) * gain


def _mod_body(c_ref, w_ref, b_ref, o_ref):
    c = c_ref[...]
    s = c * jax.nn.sigmoid(c)
    o_ref[0] = _dot(s, w_ref[0], precision=HI) + b_ref[0]


def _mod_call(cvec, w_mod, b_mod):
    n_layer, d, d6 = w_mod.shape
    tn = 1536
    return pl.pallas_call(
        _mod_body, grid=(n_layer, d6 // tn),
        in_specs=[pl.BlockSpec((8, d), lambda l, j: (0, 0)),
                  pl.BlockSpec((1, d, tn), lambda l, j: (l, 0, j)),
                  pl.BlockSpec((1, 1, tn), lambda l, j: (l, 0, j))],
        out_specs=pl.BlockSpec((1, 8, tn), lambda l, j: (l, 0, j)),
        out_shape=SDS((n_layer, 8, d6), F32), compiler_params=_cp("parallel", "parallel"),
        name="adaln_mod")(cvec, w_mod, b_mod.reshape(n_layer, 1, d6))


def _modulated_norm(x, gain, mv, is_ctx, k_shift, k_scale):
    y = x * lax.rsqrt(jnp.mean(x * x, axis=-1, keepdims=True) + NORM_EPS) * gain
    shift = jnp.where(is_ctx, mv[8 + k_shift:9 + k_shift], mv[k_shift:k_shift + 1])
    scale = jnp.where(is_ctx, mv[8 + k_scale:9 + k_scale], mv[k_scale:k_scale + 1])
    return y * (1.0 + scale) + shift


def _proj_body(n_ctx, tm, h_ref, mv_ref, g_ref, w_ref, a_ref, *out_refs):
    t = pl.program_id(1)
    row = t * tm + _iota((tm, 1), 0)
    a = _modulated_norm(h_ref[0], g_ref[...], mv_ref[0], row < n_ctx, 0, 1)
    ab = a.astype(BF16)
    a_ref[0] = ab
    p = _dot(ab, w_ref[...])
    for ref, (off, width) in zip(out_refs, SEG.values()):
        ref[0] = p[:, off:off + width]


def _proj_call(h, modv, gain, w_perm, n_ctx):
    b, t, d = h.shape
    tm = 384
    assert t % tm == 0
    outs = [SDS((b, t, d), BF16)] + [SDS((b, t, w), F32) for _, w in SEG.values()]
    out_specs = [pl.BlockSpec((1, tm, d), lambda i, j: (i, j, 0))] + [
        pl.BlockSpec((1, tm, w), lambda i, j: (i, j, 0)) for _, w in SEG.values()]
    return pl.pallas_call(
        functools.partial(_proj_body, n_ctx, tm), grid=(b, t // tm),
        in_specs=[pl.BlockSpec((1, tm, d), lambda i, j: (i, j, 0)),
                  pl.BlockSpec((1, 16, d), lambda i, j: (i, 0, 0)),
                  pl.BlockSpec((1, d), lambda i, j: (0, 0)),
                  pl.BlockSpec((d, D_PROJ), lambda i, j: (0, 0))],
        out_specs=out_specs, out_shape=outs, compiler_params=_cp("parallel", "parallel"),
        name="norm_proj")(h, modv, gain, w_perm)


def _dn_prep_body(nct, ntile, tm, x_ref, pv_ref, nx_ref, sm_ref, w_ref, alog_ref, dtb_ref,
                  q_ref, k_ref, v_ref, gb_ref):
    t = pl.program_id(1)
    first = jnp.logical_or(t == 0, t == nct)
    last = jnp.logical_or(t == nct - 1, t == ntile - 1)
    x = x_ref[0]
    pv = jnp.where(first, 0.0, pv_ref[0])
    nx = jnp.where(last, 0.0, nx_ref[0])
    w = w_ref[...]
    row = _iota((tm, 1), 0)
    xm1 = jnp.where(row == 0, pv[7:8], pltpu.roll(x, 1, 0))
    xm2 = jnp.where(row == 0, pv[6:7], jnp.where(row == 1, pv[7:8], pltpu.roll(x, 2, 0)))
    xp1 = jnp.where(row == tm - 1, nx[0:1], pltpu.roll(x, tm - 1, 0))
    xp2 = jnp.where(row == tm - 2, nx[0:1], jnp.where(row == tm - 1, nx[1:2], pltpu.roll(x, tm - 2, 0)))
    acc = w[0:1] * xm2 + w[1:2] * xm1 + w[2:3] * x + w[3:4] * xp1 + w[4:5] * xp2
    y = acc * jax.nn.sigmoid(acc)
    ones64 = _block_ones(256, 6)
    q, k = y[:, 0:256], y[:, 256:512]
    qs = _dot(q * q, ones64, precision=HI)
    ks = _dot(k * k, ones64, precision=HI)
    q_ref[0] = q * lax.rsqrt(qs + 1e-6) * (DN_DK ** -0.5)
    k_ref[0] = k * lax.rsqrt(ks + 1e-6)
    v_ref[0] = y[:, 512:768]
    sm = sm_ref[0]
    lane = _iota((1, 128), 1)
    g = -jnp.exp(alog_ref[...]) * jax.nn.softplus(sm + dtb_ref[...])
    be = jax.nn.sigmoid(sm)
    gb_ref[0] = jnp.where(lane < 8, g, jnp.where(lane < 16, be, 0.0))


def _dn_prep_call(qkv, small, conv_w, a_log, dt_bias, n_ctx):
    b, t, c = qkv.shape
    tm = 256
    assert n_ctx % tm == 0 and t % tm == 0
    nct, ntile, r8 = n_ctx // tm, t // tm, tm // 8
    w8 = jnp.zeros((8, c), F32).at[:DN_CONV].set(conv_w)
    alog = jnp.zeros((1, 128), F32).at[0, :8].set(a_log.reshape(-1))
    dtb = jnp.zeros((1, 128), F32).at[0, :8].set(dt_bias.reshape(-1))
    full = lambda shape: pl.BlockSpec(shape, lambda i, j: (0,) * len(shape))
    o256 = pl.BlockSpec((1, tm, 256), lambda i, j: (i, j, 0))
    return pl.pallas_call(
        functools.partial(_dn_prep_body, nct, ntile, tm), grid=(b, ntile),
        in_specs=[pl.BlockSpec((1, tm, c), lambda i, j: (i, j, 0)),
                  pl.BlockSpec((1, 8, c), lambda i, j: (i, jnp.maximum(j * r8 - 1, 0), 0)),
                  pl.BlockSpec((1, 8, c), lambda i, j: (i, jnp.minimum((j + 1) * r8, t // 8 - 1), 0)),
                  pl.BlockSpec((1, tm, 128), lambda i, j: (i, j, 0)),
                  full((8, c)), full((1, 128)), full((1, 128))],
        out_specs=[o256, o256, o256, pl.BlockSpec((1, tm, 128), lambda i, j: (i, j, 0))],
        out_shape=[SDS((b, t, 256), F32)] * 3 + [SDS((b, t, 128), F32)],
        compiler_params=_cp("parallel", "parallel"), name="dn_prep")(qkv, qkv, qkv, small, w8, alog, dtb)


def _scan_order(rev, step, n_first, n_total):
    back = jnp.where(step < n_first, n_first - 1 - step, n_total - 1 - (step - n_first))
    return jnp.where(rev, back, step)


def _dn_scan_body(nb, q_ref, k_ref, v_ref, gb_ref, eg_ref, eb_ref, o_ref, s_ref):
    c = DN_CHUNK
    rev = pl.program_id(0) >= nb

    @pl.when(pl.program_id(1) == 0)
    def _():
        s_ref[...] = jnp.zeros_like(s_ref)

    q, k, v, gb = q_ref[0], k_ref[0], v_ref[0], gb_ref[0]
    gexp = _dot(gb, eg_ref[0], precision=HI)
    bexp = _dot(gb, eb_ref[0], precision=HI)
    after = jnp.where(rev, _iota((c, c), 1) - _iota((c, c), 0), _iota((c, c), 0) - _iota((c, c), 1))
    gc = _dot((after >= 0).astype(F32), gexp, precision=HI)
    gl = jnp.where(rev, gc[0:1, :], gc[c - 1:c, :])
    rr, cc = _iota((256, 256), 0), _iota((256, 256), 1)
    same_head = (rr >> 6) == (cc >> 6)
    i_idx = _iota((c, 256), 0)
    j_idx = _iota((c, 256), 1) & 63
    lane_h = _iota((c, 256), 1) >> 6
    d_ij = jnp.where(rev, j_idx - i_idx, i_idx - j_idx)
    incl, strict = d_ij >= 0, d_ij > 0
    gcr = jnp.sum(jnp.where(d_ij <= 0, gexp, 0.0), axis=0, keepdims=True)
    decay = jnp.where(incl, jnp.exp(jnp.where(incl, gc - gcr, 0.0)), 0.0)

    def stack4(x):
        return jnp.concatenate([jnp.where(lane_h == h, x, 0.0) for h in range(DN_HEADS)], axis=0).astype(BF16)

    def to_bd(xcat):
        return jnp.where(same_head, jnp.concatenate([xcat] * DN_HEADS, axis=0), 0.0).astype(BF16)

    kb = k * bexp
    k4 = stack4(k)
    lower = jnp.where(strict, _nt(kb.astype(BF16), k4) * decay, 0.0)
    a_intra = jnp.where(incl, _nt(q.astype(BF16), k4) * decay, 0.0)
    p = (i_idx == j_idx).astype(F32) - lower
    m = _dot(lower.astype(BF16), to_bd(lower))
    for it in range(5):
        mb = to_bd(m)
        p = p + _dot(p.astype(BF16), mb)
        if it < 4:
            m = _dot(m.astype(BF16), mb)
    egc = jnp.exp(gc)
    pb = p.astype(BF16)
    u = _dot(pb, stack4(v * bexp))
    w = _dot(pb, stack4(kb * egc))
    s = s_ref[...]
    sb = s.astype(BF16)
    v_new = u - _dot(w.astype(BF16), sb)
    o_ref[0] = _dot((q * egc).astype(BF16), sb) + _dot(a_intra.astype(BF16), stack4(v_new))
    k_dec = k * jnp.exp(gl - gc)
    upd = _tn(k_dec.astype(BF16), v_new.astype(BF16))
    s_ref[...] = s * jnp.exp(gl) + jnp.where(same_head, upd, 0.0)


def _dn_scan_call(q, k, v, gb, eg, eb, n_ctx):
    b, t, _ = q.shape
    c = DN_CHUNK
    n, nc = t // c, n_ctx // c
    chunk = lambda i, j: _scan_order(i >= b, j, nc, n)
    blk = lambda w: pl.BlockSpec((1, c, w), lambda i, j: (i % b, chunk(i, j), 0))
    return pl.pallas_call(
        functools.partial(_dn_scan_body, b), grid=(2 * b, n),
        in_specs=[blk(256), blk(256), blk(256), blk(128),
                  pl.BlockSpec((1, 128, 256), lambda i, j: (i // b, 0, 0)),
                  pl.BlockSpec((1, 128, 256), lambda i, j: (i // b, 0, 0))],
        out_specs=pl.BlockSpec((None, 1, c, 256), lambda i, j: (i // b, i % b, chunk(i, j), 0)),
        out_shape=SDS((2, b, t, 256), F32),
        scratch_shapes=[pltpu.VMEM((256, 256), F32)],
        compiler_params=_cp("parallel", "arbitrary"), name="dn_scan")(q, k, v, gb, eg, eb)


def _s5_in_body(*refs):
    u_refs, mb_ref, o_ref = refs[:S5_CHUNK], refs[S5_CHUNK], refs[S5_CHUNK + 1]
    acc = None
    for j in range(S5_CHUNK):
        part = _dot(u_refs[j][...].astype(BF16), mb_ref[0, j])
        acc = part if acc is None else acc + part
    o_ref[0] = acc


def _s5_scan_body(n_sub, n_sub_ctx, bc_ref, al_ref, x_ref):
    rev = pl.program_id(0) == 1
    lanes = bc_ref.shape[1]
    ar, ai = al_ref[0:1, :], al_ref[1:2, :]
    is_re = (_iota((1, lanes), 1) & 127) < 64

    def sub(s, st):
        rows = pl.ds(pl.multiple_of(_scan_order(rev, s, n_sub_ctx, n_sub) * 8, 8), 8)
        blk = bc_ref[rows, :]
        seen = []
        for kk in range(8):
            seen.append(st)
            inp = jnp.where(rev, blk[7 - kk:8 - kk], blk[kk:kk + 1])
            swapped = jnp.where(is_re, pltpu.roll(st, lanes - 64, 1), pltpu.roll(st, 64, 1))
            st = st * ar + swapped * ai + inp
        x_ref[rows, :] = jnp.concatenate([jnp.where(rev, seen[7 - r], seen[r]) for r in range(8)], axis=0)
        return st

    lax.fori_loop(0, n_sub, sub, jnp.zeros((1, lanes), F32))


def _s5_out_body(*refs):
    L = S5_CHUNK
    u_refs, x_ref, kbd_ref, mc_ref, y_ref, acc_ref = refs[:L], refs[L], refs[L + 1], refs[L + 2], refs[L + 3], refs[L + 4]
    rev = pl.program_id(0) == 1
    jo = pl.program_id(2)
    acc_ref[...] = _dot(x_ref[0].astype(BF16), mc_ref[0])
    for j in range(L):
        @pl.when(jnp.where(rev, j >= jo, j <= jo))
        def _(j=j):
            acc_ref[...] += _dot(u_refs[j][...].astype(BF16), kbd_ref[0, jnp.abs(jo - j)])
    y_ref[...] = acc_ref[...]


def _s5_call(u, m_b, kbd, m_c, al, n_ctx):
    b, t, w = u.shape
    L = S5_CHUNK
    n = t // L
    r = b * n
    tr = n
    lanes = m_b.shape[-1]
    u3 = u.reshape(r, L * w)
    ublk = [pl.BlockSpec((tr, w), (lambda d, i, k, j=j: (i, j))) for j in range(L)]
    nt = 512
    bc = pl.pallas_call(
        _s5_in_body, grid=(2, r // tr, lanes // nt),
        in_specs=ublk + [pl.BlockSpec((1, L, w, nt), lambda d, i, k: (d, 0, 0, k))],
        out_specs=pl.BlockSpec((1, tr, nt), lambda d, i, k: (d, i, k)),
        out_shape=SDS((2, r, lanes), F32), compiler_params=_cp("parallel", "parallel", "parallel"),
        name="s5_in")(*([u3] * L), m_b)
    assert n % 8 == 0 and (n_ctx // L) % 8 == 0
    xprev = pl.pallas_call(
        functools.partial(_s5_scan_body, n // 8, n_ctx // L // 8), grid=(2, b),
        in_specs=[pl.BlockSpec((None, None, n, lanes), lambda d, i: (d, i, 0, 0)),
                  pl.BlockSpec((None, 8, lanes), lambda d, i: (d, 0, 0))],
        out_specs=pl.BlockSpec((None, None, n, lanes), lambda d, i: (d, i, 0, 0)),
        out_shape=SDS((2, b, n, lanes), F32), compiler_params=_cp("parallel", "parallel"),
        name="s5_scan")(bc.reshape(2, b, n, lanes), al)
    y = pl.pallas_call(
        _s5_out_body, grid=(2, r // tr, L),
        in_specs=ublk + [pl.BlockSpec((1, tr, lanes), lambda d, i, k: (d, i, 0)),
                         pl.BlockSpec((1, L, w, w), lambda d, i, k: (d, 0, 0, 0)),
                         pl.BlockSpec((1, lanes, w), lambda d, i, k: (d, 0, k))],
        out_specs=pl.BlockSpec((None, tr, w), lambda d, i, k: (d, i, k)),
        out_shape=SDS((2, r, L * w), F32), scratch_shapes=[pltpu.VMEM((tr, w), F32)],
        compiler_params=_cp("parallel", "parallel", "arbitrary"), name="s5_out")(
            *([u3] * L), xprev.reshape(2, r, lanes), kbd, m_c)
    return y.reshape(2, b, t, w)


def _s5_operators(a_re, a_im, log_dt, b_re, b_im, c_re, c_im):
    L = S5_CHUNK
    G, P, C = S5_GROUPS, S5_STATE, S5_GROUP
    dt = jnp.exp(log_dt)[:, :, None]
    ls = jnp.arange(L + 1, dtype=F32)
    mag = jnp.exp(a_re[..., None] * dt[..., None] * ls)
    ang = a_im[..., None] * dt[..., None] * ls
    pw_re, pw_im = mag * jnp.cos(ang), mag * jnp.sin(ang)
    ab_re, ab_im = pw_re[..., 1], pw_im[..., 1]
    den = a_re * a_re + a_im * a_im
    f_re = ((ab_re - 1.0) * a_re + ab_im * a_im) / den
    f_im = (ab_im * a_re - (ab_re - 1.0) * a_im) / den
    bb_re = f_re[..., None] * b_re[None] - f_im[..., None] * b_im[None]
    bb_im = f_re[..., None] * b_im[None] + f_im[..., None] * b_re[None]
    eye = jnp.eye(G, dtype=F32)
    ca_re = c_re[..., None] * pw_re[:, :, None] - c_im[..., None] * pw_im[:, :, None]
    ca_im = c_re[..., None] * pw_im[:, :, None] + c_im[..., None] * pw_re[:, :, None]
    kern = (jnp.einsum("dgopl,dgpc->dglco", ca_re, bb_re, precision=HI)
            - jnp.einsum("dgopl,dgpc->dglco", ca_im, bb_im, precision=HI))
    kbd = jnp.einsum("dglco,gh->dlgcho", kern[:, :, :L], eye).reshape(2, L, G * C, G * C)
    q_re, q_im = pw_re[..., :L], pw_im[..., :L]
    mb_re = q_re[..., None] * bb_re[:, :, :, None] - q_im[..., None] * bb_im[:, :, :, None]
    mb_im = q_re[..., None] * bb_im[:, :, :, None] + q_im[..., None] * bb_re[:, :, :, None]
    mbp = jnp.concatenate([mb_re.transpose(0, 1, 3, 4, 2), mb_im.transpose(0, 1, 3, 4, 2)], axis=-1)
    mb_j = jnp.stack([mbp[0][:, ::-1], mbp[1]])
    m_b = jnp.einsum("dgjcq,gh->djgchq", mb_j, eye).reshape(2, L, G * C, G * 2 * P)
    w_re, w_im = ca_re[..., 1:], ca_im[..., 1:]
    mcp = jnp.concatenate([w_re.transpose(0, 1, 3, 4, 2), -w_im.transpose(0, 1, 3, 4, 2)], axis=2)
    mc_j = jnp.stack([mcp[0], mcp[1][:, :, ::-1]])
    m_c = jnp.einsum("dgqjc,gh->dgqjhc", mc_j, eye).reshape(2, G * 2 * P, L * G * C)
    al_re = jnp.concatenate([pw_re[..., L], pw_re[..., L]], axis=-1).reshape(2, 1, G * 2 * P)
    al_im = jnp.concatenate([-pw_im[..., L], pw_im[..., L]], axis=-1).reshape(2, 1, G * 2 * P)
    al = jnp.concatenate([al_re, al_im, jnp.zeros((2, 6, G * 2 * P), F32)], axis=1)
    return m_b.astype(BF16), kbd.astype(BF16), m_c.astype(BF16), al


def _da_prep_body(x_ref, cos_ref, sa_ref, sb_ref, q_ref, k_ref, v_ref):
    x = x_ref[0]
    cos, sa, sb = cos_ref[...], sa_ref[...], sb_ref[...]

    def rope(z):
        return z * cos + pltpu.roll(z, 240, 1) * sa + pltpu.roll(z, 16, 1) * sb

    q_ref[0] = (rope(x[:, 0:256]) * (DA_DK ** -0.5)).astype(BF16)
    k_ref[0] = rope(x[:, 256:512]).astype(BF16)
    v_ref[0] = x[:, 512:768].astype(BF16)


def _da_prep_call(qkv, cos, sa, sb):
    b, t, _ = qkv.shape
    tm = 768
    tab = pl.BlockSpec((tm, 256), lambda i, j: (j, 0))
    o = pl.BlockSpec((1, tm, 256), lambda i, j: (i, j, 0))
    return pl.pallas_call(
        _da_prep_body, grid=(b, t // tm),
        in_specs=[pl.BlockSpec((1, tm, 768), lambda i, j: (i, j, 0)), tab, tab, tab],
        out_specs=[o, o, o], out_shape=[SDS((b, t, 256), BF16)] * 3,
        compiler_params=_cp("parallel", "parallel"), name="da_prep")(qkv, cos, sa, sb)


def _attn_body(nkb, tk, tq, post, q_ref, k_ref, v_ref, lam_ref, g_ref, o_ref, acc_ref, m_ref, l_ref):
    q = q_ref[0]
    lane = _iota((tq, 256), 1)
    lane_h = lane >> 6
    acc_ref[...] = jnp.zeros_like(acc_ref)
    m_ref[...] = jnp.full_like(m_ref, -1e30)
    l_ref[...] = jnp.zeros_like(l_ref)
    qm = [jnp.where((lane >> 5) == hm, q, jnp.zeros_like(q)) for hm in range(8)]

    def kblock(i, carry):
        rows = pl.ds(pl.multiple_of(i * tk, 8), tk)
        kb = k_ref[0, rows, :]
        vb = v_ref[0, rows, :]
        for mp in range(2):
            upd = jnp.zeros((tq, 256), F32)
            alpha_l = jnp.zeros((tq, 256), F32)
            for h in range(DA_HEADS):
                hm = 2 * h + mp
                s = _nt(qm[hm], kb)
                m_old = m_ref[:, hm:hm + 1]
                m_new = jnp.maximum(m_old, jnp.max(s, axis=1, keepdims=True))
                alpha = jnp.exp(m_old - m_new)
                p = jnp.exp(s - m_new)
                l_ref[:, hm:hm + 1] = l_ref[:, hm:hm + 1] * alpha + jnp.sum(p, axis=1, keepdims=True)
                m_ref[:, hm:hm + 1] = m_new
                pv = _dot(p.astype(BF16), vb)
                upd = jnp.where(lane_h == h, pv, upd)
                alpha_l = jnp.where(lane_h == h, alpha, alpha_l)
            acc_ref[mp] = acc_ref[mp] * alpha_l + upd
        return carry

    lax.fori_loop(0, nkb, kblock, 0)
    outs = []
    for mp in range(2):
        linv = jnp.zeros((tq, 256), F32)
        for h in range(DA_HEADS):
            linv = jnp.where(lane_h == h, 1.0 / l_ref[:, 2 * h + mp:2 * h + mp + 1], linv)
        outs.append(acc_ref[mp] * linv)
    o = outs[0] - lam_ref[...] * outs[1]
    o_ref[0] = _head_rms(o, g_ref[...], 1e-5, _block_ones(256, 6)) * post


def _attn_call(q, k, v, lam_row, gain, post, q_off, n_q, n_keys, tq, tk):
    b = q.shape[0]
    assert n_keys % tk == 0 and n_q % tq == 0 and q_off % tq == 0
    qo = q_off // tq
    return pl.pallas_call(
        functools.partial(_attn_body, n_keys // tk, tk, tq, post), grid=(b, n_q // tq),
        in_specs=[pl.BlockSpec((1, tq, 256), lambda i, j: (i, j + qo, 0)),
                  pl.BlockSpec((1, n_keys, 256), lambda i, j: (i, 0, 0)),
                  pl.BlockSpec((1, n_keys, 256), lambda i, j: (i, 0, 0)),
                  pl.BlockSpec((1, 256), lambda i, j: (0, 0)),
                  pl.BlockSpec((1, 256), lambda i, j: (0, 0))],
        out_specs=pl.BlockSpec((1, tq, 256), lambda i, j: (i, j, 0)),
        out_shape=SDS((b, n_q, 256), F32),
        scratch_shapes=[pltpu.VMEM((2, tq, 256), F32), pltpu.VMEM((tq, 128), F32), pltpu.VMEM((tq, 128), F32)],
        compiler_params=_cp("parallel", "parallel"), name="diff_attn")(q, k, v, lam_row, gain)


def _gla_body(nb, nchunk, qkv_ref, sm_ref, wg_ref, bg_ref, o_ref, st_ref):
    c = GLA_CHUNK
    rev = pl.program_id(0) >= nb

    @pl.when(pl.program_id(1) == 0)
    def _():
        st_ref[...] = jnp.zeros_like(st_ref)

    after = jnp.where(rev, _iota((c, c), 1) - _iota((c, c), 0), _iota((c, c), 0) - _iota((c, c), 1))
    tri = (after >= 0).astype(F32)
    row = _iota((c, 128), 0)
    head_ones = ((_iota((128, 256), 0) >> 5) == (_iota((128, 256), 1) >> 6)).astype(BF16)
    st_mask = (_iota((256, 128), 0) >> 6) == (_iota((256, 128), 1) >> 5)
    wg, bg = wg_ref[0], bg_ref[0]

    def chunk(step, st):
        ci = jnp.where(rev, nchunk - 1 - step, step)
        rows = pl.ds(pl.multiple_of(ci * c, c), c)
        x = qkv_ref[0, rows, :]
        q, k, v = x[:, 0:128] * (GLA_DK ** -0.5), x[:, 128:256], x[:, 256:512]
        z = _dot(sm_ref[0, rows, :], wg, precision=HI) + bg
        la = jax.nn.log_sigmoid(z) * (1.0 / GLA_TAU)
        b = _dot(tri, la, precision=HI)
        bl = jnp.where(rev, b[0:1, :], b[c - 1:c, :])
        parts = []
        for j in range(c):
            ok = jnp.where(rev, j - row, row - j) >= 0
            e = jnp.where(ok, jnp.exp(jnp.where(ok, b - b[j:j + 1, :], 0.0)), 0.0)
            parts.append(e * q * k[j:j + 1, :])
        sb = _dot(jnp.concatenate(parts, axis=0).astype(BF16), head_ones)
        o = _nt((q * jnp.exp(b)).astype(BF16), st.astype(BF16))
        for j in range(c):
            o = o + sb[j * c:(j + 1) * c, :] * v[j:j + 1, :]
        o_ref[0, rows, :] = o
        upd = _tn(v.astype(BF16), (k * jnp.exp(bl - b)).astype(BF16))
        return st * jnp.exp(bl) + jnp.where(st_mask, upd, 0.0)

    st_ref[...] = lax.fori_loop(0, nchunk, chunk, st_ref[...])


def _gla_call(qkv, small, wg2, bg2, n_ctx):
    b, t, _ = qkv.shape
    tm = 256
    assert t % tm == 0 and n_ctx % tm == 0
    blk = lambda i, j: _scan_order(i >= b, j, n_ctx // tm, t // tm)
    return pl.pallas_call(
        functools.partial(_gla_body, b, tm // GLA_CHUNK), grid=(2 * b, t // tm),
        in_specs=[pl.BlockSpec((1, tm, 512), lambda i, j: (i % b, blk(i, j), 0)),
                  pl.BlockSpec((1, tm, 128), lambda i, j: (i % b, blk(i, j), 0)),
                  pl.BlockSpec((1, 128, 128), lambda i, j: (i // b, 0, 0)),
                  pl.BlockSpec((1, 1, 128), lambda i, j: (i // b, 0, 0))],
        out_specs=pl.BlockSpec((None, 1, tm, 256), lambda i, j: (i // b, i % b, blk(i, j), 0)),
        out_shape=SDS((2, b, t, 256), F32), scratch_shapes=[pltpu.VMEM((256, 128), F32)],
        compiler_params=_cp("parallel", "arbitrary"), name="gla_scan")(qkv, small, wg2, bg2)


def _merge_body(n_ctx, tm, a_ref, h_ref, mv_ref, dnf_ref, dnr_ref, dnz_ref, s5y_ref, s5yr_ref, s5u_ref,
                da_ref, glf_ref, glr_ref, glz_ref, vec_ref, gluw_ref, wgate_ref, bgate_ref, wbr_ref,
                wout_ref, n2g_ref, wr_ref, br_ref, h_out, f_out, wc_out):
    t = pl.program_id(1)
    d = h_ref.shape[2]
    row = t * tm + _iota((tm, 1), 0)
    is_ctx = row < n_ctx
    mv = mv_ref[0]
    vec = vec_ref[...]
    ones64 = _block_ones(256, 6)
    z = dnz_ref[0]
    ya = _head_rms(dnf_ref[0] + dnr_ref[0], vec[0:1], NORM_EPS, ones64) * (z * jax.nn.sigmoid(z))
    u = s5u_ref[0]
    zz = jax.nn.gelu(s5y_ref[0] + s5yr_ref[0] + vec[2:3] * u)
    yb = zz * jax.nn.sigmoid(_bdot(zz, gluw_ref[...]) + vec[3:4])
    r = glz_ref[0]
    yd = _head_rms(glf_ref[0] + glr_ref[0], vec[1:2], NORM_EPS, ones64) * (r * jax.nn.sigmoid(r))
    a = a_ref[0]
    acc = jnp.zeros((tm, d), F32)
    for i, y in enumerate((ya, yb, da_ref[0], yd)):
        gate = jax.nn.sigmoid(_dot(a, wgate_ref[:, i * d:(i + 1) * d]) + bgate_ref[:, i * d:(i + 1) * d])
        acc = acc + gate * _dot(y.astype(BF16), wbr_ref[i])
    out = _dot(acc.astype(BF16), wout_ref[...])
    gate2 = jnp.where(is_ctx, mv[10:11], mv[2:3])
    h_new = h_ref[0] + gate2 * out
    h_out[0] = h_new
    f = _modulated_norm(h_new, n2g_ref[...], mv, is_ctx, 3, 4)
    f_out[0] = f.astype(BF16)
    sc = jax.nn.sigmoid(_nt(wr_ref[...], f, precision=HI))
    gr = sc + br_ref[...]
    rows_ = [gr[e:e + 1, :] for e in range(N_EXPERTS)]
    best_v, best_g = None, None
    for g in range(N_GROUPS):
        x = rows_[g * EPG:(g + 1) * EPG]
        top2 = None
        for i in range(EPG):
            for j in range(i + 1, EPG):
                s_ij = x[i] + x[j]
                top2 = s_ij if top2 is None else jnp.maximum(top2, s_ij)
        if best_v is None:
            best_v, best_g = top2, jnp.zeros_like(top2, dtype=jnp.int32)
        else:
            take = top2 > best_v
            best_v = jnp.where(take, top2, best_v)
            best_g = jnp.where(take, g, best_g)
    cand = []
    for kk in range(EPG):
        ck = rows_[kk]
        for g in range(1, N_GROUPS):
            ck = jnp.where(best_g == g, rows_[g * EPG + kk], ck)
        cand.append(ck)
    v1, i1 = cand[0], jnp.zeros_like(best_g)
    for kk in range(1, EPG):
        take = cand[kk] > v1
        v1 = jnp.where(take, cand[kk], v1)
        i1 = jnp.where(take, kk, i1)
    v2, i2 = jnp.full_like(v1, -jnp.inf), jnp.zeros_like(best_g)
    for kk in range(EPG):
        take = jnp.logical_and(i1 != kk, cand[kk] > v2)
        v2 = jnp.where(take, cand[kk], v2)
        i2 = jnp.where(take, kk, i2)
    e1, e2 = best_g * EPG + i1, best_g * EPG + i2
    e_iota = _iota((N_EXPERTS, tm), 0)
    s1 = jnp.sum(jnp.where(e_iota == e1, sc, 0.0), axis=0, keepdims=True)
    s2 = jnp.sum(jnp.where(e_iota == e2, sc, 0.0), axis=0, keepdims=True)
    tot = s1 + s2
    wt = jnp.where(e_iota == e1, s1 / tot, jnp.where(e_iota == e2, s2 / tot, 0.0))
    eye = (_iota((N_EXPERTS, N_EXPERTS), 0) == _iota((N_EXPERTS, N_EXPERTS), 1)).astype(F32)
    wc_out[0] = _tn(wt, eye, precision=HI)


def _merge_call(a, h, modv, dn_o, dn_z, s5_y, s5_u, da_y, gl_o, gl_r, vec, glu_w,
                w_gate, b_gate, w_branch, w_out, n2g, w_rt, b_rt, n_ctx):
    b, t, d = h.shape
    tm = 256
    assert t % tm == 0
    tok = lambda w: pl.BlockSpec((1, tm, w), lambda i, j: (i, j, 0))
    fwd = pl.BlockSpec((None, 1, tm, 256), lambda i, j: (0, i, j, 0))
    bwd = pl.BlockSpec((None, 1, tm, 256), lambda i, j: (1, i, j, 0))
    full = lambda *shape: pl.BlockSpec(shape, lambda i, j: (0,) * len(shape))
    return pl.pallas_call(
        functools.partial(_merge_body, n_ctx, tm), grid=(b, t // tm),
        in_specs=[tok(d), tok(d), pl.BlockSpec((1, 16, d), lambda i, j: (i, 0, 0)),
                  fwd, bwd, tok(256), fwd, bwd, tok(256), tok(256), fwd, bwd,
                  tok(256), full(8, 256), full(256, 256), full(d, 4 * d), full(1, 4 * d),
                  full(4, 256, d), full(d, d), full(1, d), full(N_EXPERTS, d), full(N_EXPERTS, 1)],
        out_specs=[tok(d), tok(d), tok(N_EXPERTS)],
        out_shape=[SDS((b, t, d), F32), SDS((b, t, d), BF16), SDS((b, t, N_EXPERTS), F32)],
        compiler_params=_cp("parallel", "parallel"), name="merge_router")(
            a, h, modv, dn_o, dn_o, dn_z, s5_y, s5_y, s5_u, da_y, gl_o, gl_o, gl_r, vec, glu_w,
            w_gate, b_gate, w_branch, w_out, n2g, w_rt, b_rt)


def _moe_body(n_ctx, tt, final, f_ref, wc_ref, h_ref, mv_ref, wg_ref, wu_ref, wd_ref, fg_ref, o_ref, acc_ref):
    e = pl.program_id(2)

    @pl.when(e == 0)
    def _():
        acc_ref[...] = jnp.zeros_like(acc_ref)

    x = f_ref[0]
    wc = wc_ref[0]
    wcol = jnp.sum(jnp.where(_iota(wc.shape, 1) == e, wc, 0.0), axis=1, keepdims=True)
    g = _dot(x, wg_ref[0])
    hid = g * jax.nn.sigmoid(g) * _dot(x, wu_ref[0])
    acc_ref[...] += _dot(jnp.where(wcol != 0.0, hid * wcol, 0.0).astype(BF16), wd_ref[0])

    @pl.when(e == N_EXPERTS - 1)
    def _():
        row = pl.program_id(1) * tt + _iota((tt, 1), 0)
        mv = mv_ref[0]
        gate = jnp.where(row < n_ctx, mv[13:14], mv[5:6])
        hn = h_ref[0] + gate * acc_ref[...]
        if final:
            hn = hn * lax.rsqrt(jnp.mean(hn * hn, axis=-1, keepdims=True) + NORM_EPS) * fg_ref[...]
        o_ref[0] = hn


def _moe_call(f, wc, h, modv, weg, weu, wed, final_g, n_ctx, final):
    b, t, d = h.shape
    fe = weg.shape[2]
    tt = next(c for c in (1408, 1024, 768, 512, 256, 128, 64, 32, 16, 8) if t % c == 0)
    tok = lambda w: pl.BlockSpec((1, tt, w), lambda i, j, e: (i, j, 0))
    return pl.pallas_call(
        functools.partial(_moe_body, n_ctx, tt, final), grid=(b, t // tt, N_EXPERTS),
        in_specs=[tok(d), tok(N_EXPERTS), tok(d), pl.BlockSpec((1, 16, d), lambda i, j, e: (i, 0, 0)),
                  pl.BlockSpec((1, d, fe), lambda i, j, e: (e, 0, 0)),
                  pl.BlockSpec((1, d, fe), lambda i, j, e: (e, 0, 0)),
                  pl.BlockSpec((1, fe, d), lambda i, j, e: (e, 0, 0)),
                  pl.BlockSpec((1, d), lambda i, j, e: (0, 0))],
        out_specs=tok(d), out_shape=SDS((b, t, d), F32), scratch_shapes=[pltpu.VMEM((tt, d), F32)],
        compiler_params=_cp("parallel", "parallel", "arbitrary"), name="moe_experts")(
            f, wc, h, modv, weg, weu, wed, final_g)


def _rope_tables(n_ctx, s):
    t = jnp.arange(s)
    rowp, colp = (t // GRID_W).astype(F32), (t % GRID_W).astype(F32)
    n_freq = DA_DK // 4
    inv = ROPE_BASE ** (-jnp.arange(n_freq, dtype=F32) / n_freq)
    ang = jnp.concatenate([rowp[:, None] * inv, colp[:, None] * inv], axis=-1)
    cos, sin = jnp.cos(ang), jnp.sin(ang)
    zero = jnp.zeros_like(sin)
    cos32 = jnp.concatenate([cos, cos], axis=-1)
    sa32 = jnp.concatenate([-sin, zero], axis=-1)
    sb32 = jnp.concatenate([zero, sin], axis=-1)
    def full(tab, ctx_val):
        tab = jnp.tile(tab, (1, 8))
        return jnp.concatenate([jnp.full((n_ctx, 256), ctx_val, F32), tab], axis=0)
    return full(cos32, 1.0), full(sa32, 0.0), full(sb32, 0.0)


def _perm_w_in(w_in):
    offs, o = {}, 0
    for name, width in (("dn_qkv", 768), ("dn_z", 256), ("dn_a", 8), ("dn_b", 8), ("s5_u", 256), ("da_q", 256),
                        ("da_k", 256), ("da_v", 256), ("gla_q", 128), ("gla_k", 128), ("gla_v", 256),
                        ("gla_r", 256), ("gla_gate", 32)):
        offs[name] = (o, width)
        o += width
    col = lambda n: w_in[:, offs[n][0]:offs[n][0] + offs[n][1]]
    pad = jnp.zeros((w_in.shape[0], 128 - 48), w_in.dtype)
    return jnp.concatenate([col(n) for n in ("dn_qkv", "dn_z", "s5_u", "da_q", "da_k", "da_v", "gla_q", "gla_k",
                                              "gla_v", "gla_r", "dn_a", "dn_b", "gla_gate")] + [pad], axis=1)


def kernel(x, c, ctx, c_ctx, w_mod, b_mod, norm1_g, norm2_g, w_in, dn_conv, dn_a_log, dn_dt_bias, dn_norm_g,
           s5_a_re, s5_a_im, s5_log_dt, s5_b_re, s5_b_im, s5_c_re, s5_c_im, s5_d, s5_glu_w, s5_glu_b,
           da_lambda, da_norm_g, gla_w_gate, gla_b_gate, gla_norm_g, w_branch, w_gate, b_gate, w_out,
           w_router, b_router, w_e_gate, w_e_up, w_e_down, final_g):
    b, s, d = x.shape
    n_ctx = ctx.shape[1]
    t = n_ctx + s
    depth = w_mod.shape[0]
    assert b + 1 <= 8 and b == 4

    cvec = jnp.zeros((8, d), F32).at[:b].set(c).at[b].set(c_ctx)
    mod_all = _mod_call(cvec, w_mod, b_mod).reshape(depth, 8, 6, d)
    cos_t, sa_t, sb_t = _rope_tables(n_ctx, s)
    h = jnp.concatenate([ctx, x], axis=1)

    lane256 = jnp.arange(256) // 64
    eg = jnp.stack([(jnp.arange(128)[:, None] == (SM_DN_A + dd * 4 + lane256)[None, :]).astype(F32) for dd in range(2)])
    eb = jnp.stack([(jnp.arange(128)[:, None] == (SM_DN_B + dd * 4 + lane256)[None, :]).astype(F32) for dd in range(2)])

    for layer in range(depth):
        lam_init = 0.8 - 0.6 * math.exp(-0.3 * layer)
        ml = mod_all[layer]
        modv = jnp.zeros((b, 16, d), F32).at[:, 0:6].set(ml[:b]).at[:, 8:14].set(jnp.broadcast_to(ml[b], (b, 6, d)))

        a, dn_qkv, dn_z, s5_u, da_qkv, gla_qkv, gla_r, small = _proj_call(
            h, modv, norm1_g[layer][None, :], _perm_w_in(w_in[layer]).astype(BF16), n_ctx)

        dq, dk_, dv, gb = _dn_prep_call(dn_qkv, small, dn_conv[layer], dn_a_log[layer], dn_dt_bias[layer], n_ctx)
        dn_o = _dn_scan_call(dq, dk_, dv, gb, eg, eb, n_ctx)

        m_b, kbd, m_c, al = _s5_operators(
            s5_a_re[layer], s5_a_im[layer], s5_log_dt[layer], s5_b_re[layer], s5_b_im[layer],
            s5_c_re[layer], s5_c_im[layer])
        s5_y = _s5_call(s5_u, m_b, kbd, m_c, al, n_ctx)

        lf = da_lambda[layer]
        lam = jnp.exp(jnp.sum(lf[0] * lf[1])) - jnp.exp(jnp.sum(lf[2] * lf[3])) + lam_init
        lam_row = jnp.full((1, 256), lam, F32)
        da_gain = jnp.tile(da_norm_g[layer], DA_HEADS)[None, :]
        aq, ak, av = _da_prep_call(da_qkv, cos_t, sa_t, sb_t)
        da_lat = _attn_call(aq, ak, av, lam_row, da_gain, 1.0 - lam_init, n_ctx, s, t, 256, 768)
        if layer < depth - 1:
            da_ctx = _attn_call(aq, ak, av, lam_row, da_gain, 1.0 - lam_init, 0, n_ctx, n_ctx, 256, 256)
        else:
            da_ctx = jnp.zeros((b, n_ctx, 256), F32)
        da_y = jnp.concatenate([da_ctx, da_lat], axis=1)

        wg2 = jnp.zeros((2, 128, 128), F32)
        for dd in range(2):
            wg2 = wg2.at[dd, SM_GLA + dd * GLA_RANK:SM_GLA + (dd + 1) * GLA_RANK].set(gla_w_gate[layer, dd])
        gl_o = _gla_call(gla_qkv, small, wg2, gla_b_gate[layer][:, None, :], n_ctx)

        vec = jnp.zeros((8, 256), F32)
        vec = vec.at[0].set(jnp.tile(dn_norm_g[layer], DN_HEADS)).at[1].set(jnp.tile(gla_norm_g[layer], GLA_HEADS))
        vec = vec.at[2].set(s5_d[layer]).at[3].set(s5_glu_b[layer])
        h, f, wc = _merge_call(
            a, h, modv, dn_o, dn_z, s5_y, s5_u, da_y, gl_o, gla_r, vec,
            s5_glu_w[layer].astype(BF16), w_gate[layer].astype(BF16), b_gate[layer][None, :],
            w_branch[layer].astype(BF16), w_out[layer].astype(BF16), norm2_g[layer][None, :],
            w_router.T, b_router[:, None], n_ctx)

        h = _moe_call(f, wc, h, modv, w_e_gate[layer].astype(BF16), w_e_up[layer].astype(BF16),
                      w_e_down[layer].astype(BF16), final_g[None, :], n_ctx, layer == depth - 1)
    return h[:, n_ctx:]
```

```python
import functools
import math

import jax
import jax.numpy as jnp
from jax import lax
from jax.experimental import pallas as pl
from jax.experimental.pallas import tpu as pltpu

F32 = jnp.float32
BF16 = jnp.bfloat16
HI = lax.Precision.HIGHEST
SDS = jax.ShapeDtypeStruct

GRID_W = 64
DN_HEADS, DN_DK, DN_DV, DN_CONV, DN_CHUNK = 4, 64, 64, 5, 64
DN_BLOCK = 256
DN_STREAMS = 2
S5_WIDTH, S5_GROUP, S5_STATE = 256, 16, 64
S5_GROUPS = S5_WIDTH // S5_GROUP
S5_CHUNK = 16
DA_HEADS, DA_DK, DA_DV = 4, 32, 64
ROPE_BASE = 10000.0
GLA_HEADS, GLA_DK, GLA_DV, GLA_RANK, GLA_TAU, GLA_CHUNK = 4, 32, 64, 16, 16.0, 16
N_EXPERTS, N_GROUPS, TOP_K = 16, 4, 2
EPG = N_EXPERTS // N_GROUPS
NORM_EPS = 1e-6
VMEM_LIMIT = 56 * 1024 * 1024

SEG = dict(dn_qkv=(0, 768), dn_z=(768, 256), s5_u=(1024, 256), da_qkv=(1280, 768),
           gla_qkv=(2048, 512), gla_r=(2560, 256), small=(2816, 128))
D_PROJ = 2944
SM_DN_A, SM_DN_B, SM_GLA = 0, 8, 16


def _cp(*sem):
    return pltpu.CompilerParams(dimension_semantics=sem, vmem_limit_bytes=VMEM_LIMIT)


def _nt(a, b, **kw):
    return lax.dot_general(a, b, (((1,), (1,)), ((), ())), preferred_element_type=F32, **kw)


def _tn(a, b, **kw):
    return lax.dot_general(a, b, (((0,), (0,)), ((), ())), preferred_element_type=F32, **kw)


def _dot(a, b, **kw):
    return jnp.dot(a, b, preferred_element_type=F32, **kw)


def _bdot(a, b):
    return jnp.dot(a.astype(BF16), b.astype(BF16), preferred_element_type=F32)


def _split3(x):
    hi = x.astype(BF16)
    r = x - hi.astype(F32)
    mid = r.astype(BF16)
    return hi, mid, (r - mid.astype(F32)).astype(BF16)


def _dot_sel(x, sel, dot=None):
    dot = dot or _dot
    sel = sel.astype(BF16)
    hi, mid, lo = _split3(x)
    return dot(hi, sel) + dot(mid, sel) + dot(lo, sel)


def _sel_dot(sel, x):
    sel = sel.astype(BF16)
    hi, mid, lo = _split3(x)
    return _dot(sel, hi) + _dot(sel, mid) + _dot(sel, lo)


def _iota(shape, dim):
    return lax.broadcasted_iota(jnp.int32, shape, dim)


def _block_ones(n, blk_shift):
    r, c = _iota((n, n), 0), _iota((n, n), 1)
    return ((r >> blk_shift) == (c >> blk_shift)).astype(F32)


def _head_rms(o, gain, eps, ones64):
    ms = _dot_sel(o * o, ones64) * (1.0 / 64.0)
    return o * lax.rsqrt(ms + eps) * gain


def _mod_body(c_ref, w_ref, b_ref, o_ref):
    c = c_ref[...]
    s = c * jax.nn.sigmoid(c)
    o_ref[0] = _dot(s, w_ref[0], precision=HI) + b_ref[0]


def _mod_call(cvec, w_mod, b_mod):
    n_layer, d, d6 = w_mod.shape
    tn = 1536
    return pl.pallas_call(
        _mod_body, grid=(n_layer, d6 // tn),
        in_specs=[pl.BlockSpec((8, d), lambda l, j: (0, 0)),
                  pl.BlockSpec((1, d, tn), lambda l, j: (l, 0, j)),
                  pl.BlockSpec((1, 1, tn), lambda l, j: (l, 0, j))],
        out_specs=pl.BlockSpec((1, 8, tn), lambda l, j: (l, 0, j)),
        out_shape=SDS((n_layer, 8, d6), F32), compiler_params=_cp("parallel", "parallel"),
        name="adaln_mod")(cvec, w_mod, b_mod.reshape(n_layer, 1, d6))


def _modulated_norm(x, gain, mv, is_ctx, k_shift, k_scale):
    y = x * lax.rsqrt(jnp.mean(x * x, axis=-1, keepdims=True) + NORM_EPS) * gain
    shift = jnp.where(is_ctx, mv[8 + k_shift:9 + k_shift], mv[k_shift:k_shift + 1])
    scale = jnp.where(is_ctx, mv[8 + k_scale:9 + k_scale], mv[k_scale:k_scale + 1])
    return y * (1.0 + scale) + shift


def _proj_body(n_ctx, tm, h_ref, mv_ref, g_ref, w_ref, a_ref, *out_refs):
    t = pl.program_id(1)
    row = t * tm + _iota((tm, 1), 0)
    a = _modulated_norm(h_ref[0], g_ref[...], mv_ref[0], row < n_ctx, 0, 1)
    ab = a.astype(BF16)
    a_ref[0] = ab
    p = _dot(ab, w_ref[...])
    for ref, (off, width) in zip(out_refs, SEG.values()):
        ref[0] = p[:, off:off + width]


def _proj_call(h, modv, gain, w_perm, n_ctx):
    b, t, d = h.shape
    tm = 384
    assert t % tm == 0
    outs = [SDS((b, t, d), BF16)] + [SDS((b, t, w), F32) for _, w in SEG.values()]
    out_specs = [pl.BlockSpec((1, tm, d), lambda i, j: (i, j, 0))] + [
        pl.BlockSpec((1, tm, w), lambda i, j: (i, j, 0)) for _, w in SEG.values()]
    return pl.pallas_call(
        functools.partial(_proj_body, n_ctx, tm), grid=(b, t // tm),
        in_specs=[pl.BlockSpec((1, tm, d), lambda i, j: (i, j, 0)),
                  pl.BlockSpec((1, 16, d), lambda i, j: (i, 0, 0)),
                  pl.BlockSpec((1, d), lambda i, j: (0, 0)),
                  pl.BlockSpec((d, D_PROJ), lambda i, j: (0, 0))],
        out_specs=out_specs, out_shape=outs, compiler_params=_cp("parallel", "parallel"),
        name="norm_proj")(h, modv, gain, w_perm)


def _dn_prep_body(nct, ntile, tm, x_ref, pv_ref, nx_ref, sm_ref, w_ref, alog_ref, dtb_ref,
                  q_ref, k_ref, v_ref, gb_ref):
    t = pl.program_id(1)
    first = jnp.logical_or(t == 0, t == nct)
    last = jnp.logical_or(t == nct - 1, t == ntile - 1)
    x = x_ref[0]
    pv = jnp.where(first, 0.0, pv_ref[0])
    nx = jnp.where(last, 0.0, nx_ref[0])
    w = w_ref[...]
    row = _iota((tm, 1), 0)
    xm1 = jnp.where(row == 0, pv[7:8], pltpu.roll(x, 1, 0))
    xm2 = jnp.where(row == 0, pv[6:7], jnp.where(row == 1, pv[7:8], pltpu.roll(x, 2, 0)))
    xp1 = jnp.where(row == tm - 1, nx[0:1], pltpu.roll(x, tm - 1, 0))
    xp2 = jnp.where(row == tm - 2, nx[0:1], jnp.where(row == tm - 1, nx[1:2], pltpu.roll(x, tm - 2, 0)))
    acc = w[0:1] * xm2 + w[1:2] * xm1 + w[2:3] * x + w[3:4] * xp1 + w[4:5] * xp2
    y = acc * jax.nn.sigmoid(acc)
    ones64 = _block_ones(256, 6)
    q, k = y[:, 0:256], y[:, 256:512]
    qs = _dot_sel(q * q, ones64)
    ks = _dot_sel(k * k, ones64)
    q_ref[0] = q * lax.rsqrt(qs + 1e-6) * (DN_DK ** -0.5)
    k_ref[0] = k * lax.rsqrt(ks + 1e-6)
    v_ref[0] = y[:, 512:768]
    sm = sm_ref[0]
    lane = _iota((1, 128), 1)
    g = -jnp.exp(alog_ref[...]) * jax.nn.softplus(sm + dtb_ref[...])
    be = jax.nn.sigmoid(sm)
    gb_ref[0] = jnp.where(lane < 8, g, jnp.where(lane < 16, be, 0.0))


def _dn_prep_call(qkv, small, conv_w, a_log, dt_bias, n_ctx):
    b, t, c = qkv.shape
    tm = 256
    assert n_ctx % tm == 0 and t % tm == 0
    nct, ntile, r8 = n_ctx // tm, t // tm, tm // 8
    w8 = jnp.zeros((8, c), F32).at[:DN_CONV].set(conv_w)
    alog = jnp.zeros((1, 128), F32).at[0, :8].set(a_log.reshape(-1))
    dtb = jnp.zeros((1, 128), F32).at[0, :8].set(dt_bias.reshape(-1))
    full = lambda shape: pl.BlockSpec(shape, lambda i, j: (0,) * len(shape))
    o256 = pl.BlockSpec((1, tm, 256), lambda i, j: (i, j, 0))
    return pl.pallas_call(
        functools.partial(_dn_prep_body, nct, ntile, tm), grid=(b, ntile),
        in_specs=[pl.BlockSpec((1, tm, c), lambda i, j: (i, j, 0)),
                  pl.BlockSpec((1, 8, c), lambda i, j: (i, jnp.maximum(j * r8 - 1, 0), 0)),
                  pl.BlockSpec((1, 8, c), lambda i, j: (i, jnp.minimum((j + 1) * r8, t // 8 - 1), 0)),
                  pl.BlockSpec((1, tm, 128), lambda i, j: (i, j, 0)),
                  full((8, c)), full((1, 128)), full((1, 128))],
        out_specs=[o256, o256, o256, pl.BlockSpec((1, tm, 128), lambda i, j: (i, j, 0))],
        out_shape=[SDS((b, t, 256), F32)] * 3 + [SDS((b, t, 128), F32)],
        compiler_params=_cp("parallel", "parallel"), name="dn_prep")(qkv, qkv, qkv, small, w8, alog, dtb)


def _scan_order(rev, step, n_first, n_total):
    back = jnp.where(step < n_first, n_first - 1 - step, n_total - 1 - (step - n_first))
    return jnp.where(rev, back, step)


def _dn_scan_body(rev, nst, nch, q_ref, k_ref, v_ref, gb_ref, eg_ref, eb_ref, o_ref, s_ref):
    c = DN_CHUNK
    tm = nst * nch * c
    chunks = range(nst * nch)
    rows = lambda x, ci: x[ci * c:(ci + 1) * c]

    @pl.when(pl.program_id(1) == 0)
    def _():
        s_ref[...] = jnp.zeros_like(s_ref)

    q, k, v, gb = [r[...].reshape(tm, r.shape[2]) for r in (q_ref, k_ref, v_ref, gb_ref)]
    gexp = _dot_sel(gb, eg_ref[0])
    bexp = _dot_sel(gb, eb_ref[0])
    r_i, c_i = _iota((c, c), 0), _iota((c, c), 1)
    tri = (c_i >= r_i) if rev else (c_i <= r_i)
    gc = jnp.concatenate([_sel_dot(tri, rows(gexp, ci)) for ci in chunks], axis=0)
    egc = jnp.exp(gc)
    kb = k * bexp
    rhs_u, rhs_w = v * bexp, kb * egc
    rr, cc = _iota((256, 256), 0), _iota((256, 256), 1)
    same_head = (rr >> 6) == (cc >> 6)
    i_idx = _iota((c, 256), 0)
    j_idx = _iota((c, 256), 1) & 63
    lane_h = _iota((c, 256), 1) >> 6
    d_ij = (j_idx - i_idx) if rev else (i_idx - j_idx)
    incl, strict = d_ij >= 0, d_ij > 0
    eye = (i_idx == j_idx).astype(F32)

    def stack4(x):
        return jnp.concatenate([jnp.where(lane_h == h, x, 0.0) for h in range(DN_HEADS)], axis=0).astype(BF16)

    def to_bd(xcat):
        return jnp.where(same_head, jnp.concatenate([xcat] * DN_HEADS, axis=0), 0.0).astype(BF16)

    both = lambda x, y: jnp.concatenate([x, y], axis=0).astype(BF16)
    gcc = [rows(gc, ci) for ci in chunks]
    gl = [g[0:1, :] if rev else g[c - 1:c, :] for g in gcc]
    gcr = [jnp.sum(jnp.where(d_ij <= 0, rows(gexp, ci), 0.0), axis=0, keepdims=True) for ci in chunks]
    decay = [jnp.where(incl, jnp.exp(jnp.where(incl, gcc[ci] - gcr[ci], 0.0)), 0.0) for ci in chunks]
    k4 = [stack4(rows(k, ci)) for ci in chunks]
    kq = [_nt(both(rows(kb, ci), rows(q, ci)), k4[ci]) for ci in chunks]
    lower = [jnp.where(strict, kq[ci][:c] * decay[ci], 0.0) for ci in chunks]
    a_intra = [jnp.where(incl, kq[ci][c:] * decay[ci], 0.0).astype(BF16) for ci in chunks]
    p = [eye - lower[ci] for ci in chunks]
    m = [_dot(lower[ci].astype(BF16), to_bd(lower[ci])) for ci in chunks]
    for it in range(5):
        mb = [to_bd(m[ci]) for ci in chunks]
        if it < 4:
            pm = [_dot(both(p[ci], m[ci]), mb[ci]) for ci in chunks]
            p = [p[ci] + pm[ci][:c] for ci in chunks]
            m = [pm[ci][c:] for ci in chunks]
        else:
            p = [p[ci] + _dot(p[ci].astype(BF16), mb[ci]) for ci in chunks]
    uw = [_dot(p[ci].astype(BF16), jnp.concatenate([stack4(rows(rhs_u, ci)), stack4(rows(rhs_w, ci))], axis=1))
          for ci in chunks]
    wq = [both(uw[ci][:, 256:], rows(q, ci) * rows(egc, ci)) for ci in chunks]
    k_dec = [(rows(k, ci) * jnp.exp(gl[ci] - gcc[ci])).astype(BF16) for ci in chunks]
    s = [s_ref[st] for st in range(nst)]
    outs = [None] * (nst * nch)
    for step in range(nch):
        for st in range(nst):
            ci = st * nch + (nch - 1 - step if rev else step)
            r = _dot(wq[ci], s[st].astype(BF16))
            v_new = uw[ci][:, :256] - r[:c]
            outs[ci] = r[c:] + _dot(a_intra[ci], stack4(v_new))
            s[st] = s[st] * jnp.exp(gl[ci]) + jnp.where(same_head, _tn(k_dec[ci], v_new.astype(BF16)), 0.0)
    for st in range(nst):
        s_ref[st] = s[st]
    o_ref[...] = jnp.concatenate(outs, axis=0).reshape(nst, nch * c, 256)


def _dn_scan_call(q, k, v, gb, eg, eb, n_ctx, rev):
    b, t, _ = q.shape
    tm = DN_BLOCK
    assert t % tm == 0 and n_ctx % tm == 0 and tm % DN_CHUNK == 0
    blk_of = lambda j: _scan_order(rev, j, n_ctx // tm, t // tm)
    nst = DN_STREAMS if b % DN_STREAMS == 0 else 1
    blk = lambda w: pl.BlockSpec((nst, tm, w), lambda i, j: (i, blk_of(j), 0))
    return pl.pallas_call(
        functools.partial(_dn_scan_body, rev, nst, tm // DN_CHUNK), grid=(b // nst, t // tm),
        in_specs=[blk(256), blk(256), blk(256), blk(128),
                  pl.BlockSpec((1, 128, 256), lambda i, j: (0, 0, 0)),
                  pl.BlockSpec((1, 128, 256), lambda i, j: (0, 0, 0))],
        out_specs=blk(256), out_shape=SDS((b, t, 256), F32),
        scratch_shapes=[pltpu.VMEM((nst, 256, 256), F32)],
        compiler_params=_cp("parallel", "arbitrary"), name="dn_scan")(q, k, v, gb, eg, eb)


def _s5_in_body(*refs):
    u_refs, mb_ref, o_ref = refs[:S5_CHUNK], refs[S5_CHUNK], refs[S5_CHUNK + 1]
    acc = None
    for j in range(S5_CHUNK):
        part = _dot(u_refs[j][...].astype(BF16), mb_ref[0, j])
        acc = part if acc is None else acc + part
    o_ref[0] = acc


def _s5_scan_body(n_sub, n_sub_ctx, bc_ref, al_ref, x_ref):
    rev = pl.program_id(0) == 1
    lanes = bc_ref.shape[1]
    ar, ai = al_ref[0:1, :], al_ref[1:2, :]
    is_re = (_iota((1, lanes), 1) & 127) < 64

    def sub(s, st):
        rows = pl.ds(pl.multiple_of(_scan_order(rev, s, n_sub_ctx, n_sub) * 8, 8), 8)
        blk = bc_ref[rows, :]
        seen = []
        for kk in range(8):
            seen.append(st)
            inp = jnp.where(rev, blk[7 - kk:8 - kk], blk[kk:kk + 1])
            swapped = jnp.where(is_re, pltpu.roll(st, lanes - 64, 1), pltpu.roll(st, 64, 1))
            st = st * ar + swapped * ai + inp
        x_ref[rows, :] = jnp.concatenate([jnp.where(rev, seen[7 - r], seen[r]) for r in range(8)], axis=0)
        return st

    lax.fori_loop(0, n_sub, sub, jnp.zeros((1, lanes), F32))


def _s5_out_body(*refs):
    L = S5_CHUNK
    u_refs, x_ref, kbd_ref, mc_ref, y_ref, acc_ref = refs[:L], refs[L], refs[L + 1], refs[L + 2], refs[L + 3], refs[L + 4]
    rev = pl.program_id(0) == 1
    jo = pl.program_id(2)
    acc_ref[...] = _dot(x_ref[0].astype(BF16), mc_ref[0])
    for j in range(L):
        @pl.when(jnp.where(rev, j >= jo, j <= jo))
        def _(j=j):
            acc_ref[...] += _dot(u_refs[j][...].astype(BF16), kbd_ref[0, jnp.abs(jo - j)])
    y_ref[...] = acc_ref[...]


def _s5_call(u, m_b, kbd, m_c, al, n_ctx):
    b, t, w = u.shape
    L = S5_CHUNK
    n = t // L
    r = b * n
    tr = n
    lanes = m_b.shape[-1]
    u3 = u.reshape(r, L * w)
    ublk = [pl.BlockSpec((tr, w), (lambda d, i, k, j=j: (i, j))) for j in range(L)]
    nt = 512
    bc = pl.pallas_call(
        _s5_in_body, grid=(2, r // tr, lanes // nt),
        in_specs=ublk + [pl.BlockSpec((1, L, w, nt), lambda d, i, k: (d, 0, 0, k))],
        out_specs=pl.BlockSpec((1, tr, nt), lambda d, i, k: (d, i, k)),
        out_shape=SDS((2, r, lanes), F32), compiler_params=_cp("parallel", "parallel", "parallel"),
        name="s5_in")(*([u3] * L), m_b)
    assert n % 8 == 0 and (n_ctx // L) % 8 == 0
    xprev = pl.pallas_call(
        functools.partial(_s5_scan_body, n // 8, n_ctx // L // 8), grid=(2, b),
        in_specs=[pl.BlockSpec((None, None, n, lanes), lambda d, i: (d, i, 0, 0)),
                  pl.BlockSpec((None, 8, lanes), lambda d, i: (d, 0, 0))],
        out_specs=pl.BlockSpec((None, None, n, lanes), lambda d, i: (d, i, 0, 0)),
        out_shape=SDS((2, b, n, lanes), F32), compiler_params=_cp("parallel", "parallel"),
        name="s5_scan")(bc.reshape(2, b, n, lanes), al)
    y = pl.pallas_call(
        _s5_out_body, grid=(2, r // tr, L),
        in_specs=ublk + [pl.BlockSpec((1, tr, lanes), lambda d, i, k: (d, i, 0)),
                         pl.BlockSpec((1, L, w, w), lambda d, i, k: (d, 0, 0, 0)),
                         pl.BlockSpec((1, lanes, w), lambda d, i, k: (d, 0, k))],
        out_specs=pl.BlockSpec((None, tr, w), lambda d, i, k: (d, i, k)),
        out_shape=SDS((2, r, L * w), F32), scratch_shapes=[pltpu.VMEM((tr, w), F32)],
        compiler_params=_cp("parallel", "parallel", "arbitrary"), name="s5_out")(
            *([u3] * L), xprev.reshape(2, r, lanes), kbd, m_c)
    return y.reshape(2, b, t, w)


def _s5_operators(a_re, a_im, log_dt, b_re, b_im, c_re, c_im):
    L = S5_CHUNK
    G, P, C = S5_GROUPS, S5_STATE, S5_GROUP
    dt = jnp.exp(log_dt)[:, :, None]
    ls = jnp.arange(L + 1, dtype=F32)
    mag = jnp.exp(a_re[..., None] * dt[..., None] * ls)
    ang = a_im[..., None] * dt[..., None] * ls
    pw_re, pw_im = mag * jnp.cos(ang), mag * jnp.sin(ang)
    ab_re, ab_im = pw_re[..., 1], pw_im[..., 1]
    den = a_re * a_re + a_im * a_im
    f_re = ((ab_re - 1.0) * a_re + ab_im * a_im) / den
    f_im = (ab_im * a_re - (ab_re - 1.0) * a_im) / den
    bb_re = f_re[..., None] * b_re[None] - f_im[..., None] * b_im[None]
    bb_im = f_re[..., None] * b_im[None] + f_im[..., None] * b_re[None]
    eye = jnp.eye(G, dtype=F32)
    ca_re = c_re[..., None] * pw_re[:, :, None] - c_im[..., None] * pw_im[:, :, None]
    ca_im = c_re[..., None] * pw_im[:, :, None] + c_im[..., None] * pw_re[:, :, None]
    kern = (jnp.einsum("dgopl,dgpc->dglco", ca_re, bb_re, precision=HI)
            - jnp.einsum("dgopl,dgpc->dglco", ca_im, bb_im, precision=HI))
    kbd = jnp.einsum("dglco,gh->dlgcho", kern[:, :, :L], eye).reshape(2, L, G * C, G * C)
    q_re, q_im = pw_re[..., :L], pw_im[..., :L]
    mb_re = q_re[..., None] * bb_re[:, :, :, None] - q_im[..., None] * bb_im[:, :, :, None]
    mb_im = q_re[..., None] * bb_im[:, :, :, None] + q_im[..., None] * bb_re[:, :, :, None]
    mbp = jnp.concatenate([mb_re.transpose(0, 1, 3, 4, 2), mb_im.transpose(0, 1, 3, 4, 2)], axis=-1)
    mb_j = jnp.stack([mbp[0][:, ::-1], mbp[1]])
    m_b = jnp.einsum("dgjcq,gh->djgchq", mb_j, eye).reshape(2, L, G * C, G * 2 * P)
    w_re, w_im = ca_re[..., 1:], ca_im[..., 1:]
    mcp = jnp.concatenate([w_re.transpose(0, 1, 3, 4, 2), -w_im.transpose(0, 1, 3, 4, 2)], axis=2)
    mc_j = jnp.stack([mcp[0], mcp[1][:, :, ::-1]])
    m_c = jnp.einsum("dgqjc,gh->dgqjhc", mc_j, eye).reshape(2, G * 2 * P, L * G * C)
    al_re = jnp.concatenate([pw_re[..., L], pw_re[..., L]], axis=-1).reshape(2, 1, G * 2 * P)
    al_im = jnp.concatenate([-pw_im[..., L], pw_im[..., L]], axis=-1).reshape(2, 1, G * 2 * P)
    al = jnp.concatenate([al_re, al_im, jnp.zeros((2, 6, G * 2 * P), F32)], axis=1)
    return m_b.astype(BF16), kbd.astype(BF16), m_c.astype(BF16), al


def _da_prep_body(x_ref, cos_ref, sa_ref, sb_ref, q_ref, k_ref, vt_ref):
    x = x_ref[0]
    cos, sa, sb = cos_ref[...], sa_ref[...], sb_ref[...]

    def rope(z):
        return z * cos + pltpu.roll(z, 240, 1) * sa + pltpu.roll(z, 16, 1) * sb

    q_ref[0] = (rope(x[:, 0:256]) * (DA_DK ** -0.5 * LOG2E)).astype(BF16)
    k_ref[0] = rope(x[:, 256:512]).astype(BF16)
    vt = x[:, 512:768].T.astype(BF16)
    ones = jnp.ones((DA_VROWS - DA_DV, vt.shape[1]), BF16)
    vt_ref[0, 0] = jnp.concatenate(
        [blk for h in range(DA_HEADS) for blk in (vt[h * DA_DV:(h + 1) * DA_DV], ones)], axis=0)


def _da_kblock(t):
    return next(c for c in (1408, 768, 512, 256) if t % c == 0)


DA_VROWS = DA_DV + 16
LOG2E = 1.4426950408889634


def _da_prep_call(qkv, cos, sa, sb):
    b, t, _ = qkv.shape
    tm = _da_kblock(t)
    assert t % tm == 0
    tab = pl.BlockSpec((tm, 256), lambda i, j: (j, 0))
    o = pl.BlockSpec((1, tm, 256), lambda i, j: (i, j, 0))
    return pl.pallas_call(
        _da_prep_body, grid=(b, t // tm),
        in_specs=[pl.BlockSpec((1, tm, 768), lambda i, j: (i, j, 0)), tab, tab, tab],
        out_specs=[o, o, pl.BlockSpec((1, 1, DA_HEADS * DA_VROWS, tm), lambda i, j: (i, j, 0, 0))],
        out_shape=[SDS((b, t, 256), BF16)] * 2 + [SDS((b, t // tm, DA_HEADS * DA_VROWS, tm), BF16)],
        compiler_params=_cp("parallel", "parallel"), name="da_prep")(qkv, cos, sa, sb)


def _attn_body(nkb, tk, tq, post, q_ref, k_ref, vt_ref, lam_ref, g_ref, o_ref, q8_ref, acc_ref, s0_ref, p7_ref,
               a7_ref):
    q = q_ref[0]
    lane = _iota((tq, 256), 1)
    for hm in range(8):
        q8_ref[hm] = jnp.where((lane >> 5) == hm, q, jnp.zeros_like(q))
    acc_ref[...] = jnp.zeros_like(acc_ref)

    def keys(i):
        return k_ref[0, pl.ds(pl.multiple_of(i * tk, 8), tk), :]

    s0_ref[...] = _nt(keys(0), q8_ref[0])

    nsplit = 4 if tk % 32 == 0 else 1
    p7_ref[...] = jnp.zeros_like(p7_ref)
    a7_ref[...] = jnp.ones_like(a7_ref)

    def values(i, hm):
        h = hm // 2
        return vt_ref[0, i][h * DA_VROWS:(h + 1) * DA_VROWS, :tk]

    def accumulate(hm, vals, p, alpha):
        acc_ref[hm] = acc_ref[hm] * alpha + _dot(vals, p)

    def kblock(i, carry):
        m = carry
        kb = keys(i)
        kb_next = keys(jnp.minimum(i + 1, nkb - 1))
        m_rows = []
        s = s0_ref[...]
        pending = (7, values(jnp.maximum(i - 1, 0), 7), p7_ref[...], a7_ref[0:1, :])
        for hm in range(8):
            s_next = _nt(kb, q8_ref[hm + 1]) if hm < 7 else _nt(kb_next, q8_ref[0])
            accumulate(*pending)
            m_old = m[hm:hm + 1, :]
            part = s[0:tk // nsplit]
            for r in range(1, nsplit):
                part = jnp.maximum(part, s[r * tk // nsplit:(r + 1) * tk // nsplit])
            m_new = jnp.maximum(m_old, jnp.max(part, axis=0, keepdims=True))
            alpha = jnp.exp2(m_old - m_new)
            p = jnp.exp2(s - m_new).astype(BF16)
            m_rows.append(m_new)
            pending = (hm, values(i, hm), p, alpha)
            s = s_next
        s0_ref[...] = s
        p7_ref[...] = pending[2]
        a7_ref[0:1, :] = pending[3]
        return jnp.concatenate(m_rows, axis=0)

    lax.fori_loop(0, nkb, kblock, jnp.full((8, tq), -1e30, F32))
    accumulate(7, values(nkb - 1, 7), p7_ref[...], a7_ref[0:1, :])
    lam = lam_ref[0:1, 0:1]
    heads = []
    for h in range(DA_HEADS):
        a1, a2 = acc_ref[2 * h], acc_ref[2 * h + 1]
        o = a1[0:DA_DV] / a1[DA_DV:DA_DV + 1] - lam * (a2[0:DA_DV] / a2[DA_DV:DA_DV + 1])
        ms = jnp.mean(o * o, axis=0, keepdims=True)
        heads.append(o * lax.rsqrt(ms + 1e-5) * g_ref[...] * post)
    o_ref[0] = jnp.concatenate(heads, axis=0).T


def _attn_call(q, k, vt, lam_row, gain_col, post, q_off, n_q, n_keys, tq, tk):
    b = q.shape[0]
    assert n_keys % tk == 0 and n_q % tq == 0 and q_off % tq == 0
    nkb = n_keys // tk
    assert tk == vt.shape[3] or nkb == 1
    qo = q_off // tq
    return pl.pallas_call(
        functools.partial(_attn_body, nkb, tk, tq, post), grid=(b, n_q // tq),
        in_specs=[pl.BlockSpec((1, tq, 256), lambda i, j: (i, j + qo, 0)),
                  pl.BlockSpec((1, n_keys, 256), lambda i, j: (i, 0, 0)),
                  pl.BlockSpec((1, nkb, vt.shape[2], vt.shape[3]), lambda i, j: (i, 0, 0, 0)),
                  pl.BlockSpec((1, 128), lambda i, j: (0, 0)),
                  pl.BlockSpec((DA_DV, 1), lambda i, j: (0, 0))],
        out_specs=pl.BlockSpec((1, tq, 256), lambda i, j: (i, j, 0)),
        out_shape=SDS((b, n_q, 256), F32),
        scratch_shapes=[pltpu.VMEM((8, tq, 256), BF16), pltpu.VMEM((8, DA_VROWS, tq), F32),
                        pltpu.VMEM((tk, tq), F32), pltpu.VMEM((tk, tq), BF16), pltpu.VMEM((8, tq), F32)],
        compiler_params=_cp("parallel", "parallel"), name="diff_attn")(q, k, vt, lam_row, gain_col)


def _gla_body(rev, n, qkv_ref, sm_ref, wg_ref, bg_ref, o_ref, st_ref):
    c = GLA_CHUNK
    tm = n * c

    @pl.when(pl.program_id(1) == 0)
    def _():
        st_ref[...] = jnp.zeros_like(st_ref)

    x = qkv_ref[0]
    q, k, v = x[:, 0:128] * (GLA_DK ** -0.5), x[:, 128:256], x[:, 256:512]
    z = _dot(sm_ref[0], wg_ref[0], precision=HI) + bg_ref[0]
    la = jax.nn.log_sigmoid(z) * (1.0 / GLA_TAU)
    r_i, c_i = _iota((tm, tm), 0), _iota((tm, tm), 1)
    before = (c_i >= r_i) if rev else (c_i <= r_i)
    tri = jnp.logical_and((r_i >> 4) == (c_i >> 4), before)
    b = _sel_dot(tri, la)
    chunk_sel = (_iota((tm, 128), 0) >> 4) == _iota((tm, 128), 1)
    a_cols = jnp.exp(_dot_sel(la, chunk_sel, dot=_tn))
    b3, q3, k3 = b.reshape(n, c, 128), q.reshape(n, c, 128), k.reshape(n, c, 128)
    v3 = v.reshape(n, c, 256)
    bl3 = b3[:, 0:1, :] if rev else b3[:, c - 1:c, :]
    row3 = _iota((n, c, 128), 1)
    head_ones = ((_iota((128, 256), 0) >> 5) == (_iota((128, 256), 1) >> 6)).astype(BF16)
    o3 = jnp.zeros((n, c, 256), F32)
    for j in range(c):
        ok = (row3 <= j) if rev else (row3 >= j)
        e = jnp.where(ok, jnp.exp(jnp.where(ok, b3 - b3[:, j:j + 1, :], 0.0)), 0.0)
        part = (e * q3 * k3[:, j:j + 1, :]).reshape(tm, 128).astype(BF16)
        o3 = o3 + _dot(part, head_ones).reshape(n, c, 256) * v3[:, j:j + 1, :]
    q_dec = (q3 * jnp.exp(b3)).astype(BF16)
    k_dec = (k3 * jnp.exp(bl3 - b3)).astype(BF16)
    vb = v3.astype(BF16)
    st_mask = (_iota((128, 256), 0) >> 5) == (_iota((128, 256), 1) >> 6)
    order = range(n - 1, -1, -1) if rev else range(n)
    upd = {ci: jnp.where(st_mask, _tn(k_dec[ci], vb[ci]), 0.0) for ci in order}
    st = st_ref[...]
    inter = [None] * n
    for ci in order:
        inter[ci] = _dot(q_dec[ci], st.astype(BF16))
        st = st * a_cols[:, ci:ci + 1] + upd[ci]
    st_ref[...] = st
    o_ref[0] = jnp.concatenate(inter, axis=0) + o3.reshape(tm, 256)


def _gla_call(qkv, small, wg, bg, n_ctx, rev):
    b, t, _ = qkv.shape
    tm = 256
    assert t % tm == 0 and n_ctx % tm == 0 and tm // GLA_CHUNK <= 128 and GLA_CHUNK == 16
    blk = lambda i, j: _scan_order(rev, j, n_ctx // tm, t // tm)
    return pl.pallas_call(
        functools.partial(_gla_body, rev, tm // GLA_CHUNK), grid=(b, t // tm),
        in_specs=[pl.BlockSpec((1, tm, 512), lambda i, j: (i, blk(i, j), 0)),
                  pl.BlockSpec((1, tm, 128), lambda i, j: (i, blk(i, j), 0)),
                  pl.BlockSpec((1, 128, 128), lambda i, j: (0, 0, 0)),
                  pl.BlockSpec((1, 1, 128), lambda i, j: (0, 0, 0))],
        out_specs=pl.BlockSpec((1, tm, 256), lambda i, j: (i, blk(i, j), 0)),
        out_shape=SDS((b, t, 256), F32), scratch_shapes=[pltpu.VMEM((128, 256), F32)],
        compiler_params=_cp("parallel", "arbitrary"), name="gla_scan")(qkv, small, wg, bg)


def _merge_body(n_ctx, tm, a_ref, h_ref, mv_ref, dnf_ref, dnr_ref, dnz_ref, s5y_ref, s5yr_ref, s5u_ref,
                da_ref, glf_ref, glr_ref, glz_ref, vec_ref, gluw_ref, wgate_ref, bgate_ref, wbr_ref,
                wout_ref, n2g_ref, wr_ref, br_ref, h_out, f_out, wc_out):
    t = pl.program_id(1)
    d = h_ref.shape[2]
    row = t * tm + _iota((tm, 1), 0)
    is_ctx = row < n_ctx
    mv = mv_ref[0]
    vec = vec_ref[...]
    ones64 = _block_ones(256, 6)
    z = dnz_ref[0]
    ya = _head_rms(dnf_ref[0] + dnr_ref[0], vec[0:1], NORM_EPS, ones64) * (z * jax.nn.sigmoid(z))
    u = s5u_ref[0]
    zz = jax.nn.gelu(s5y_ref[0] + s5yr_ref[0] + vec[2:3] * u)
    yb = zz * jax.nn.sigmoid(_bdot(zz, gluw_ref[...]) + vec[3:4])
    r = glz_ref[0]
    yd = _head_rms(glf_ref[0] + glr_ref[0], vec[1:2], NORM_EPS, ones64) * (r * jax.nn.sigmoid(r))
    a = a_ref[0]
    acc = jnp.zeros((tm, d), F32)
    for i, y in enumerate((ya, yb, da_ref[0], yd)):
        gate = jax.nn.sigmoid(_dot(a, wgate_ref[:, i * d:(i + 1) * d]) + bgate_ref[:, i * d:(i + 1) * d])
        acc = acc + gate * _dot(y.astype(BF16), wbr_ref[i])
    out = _dot(acc.astype(BF16), wout_ref[...])
    gate2 = jnp.where(is_ctx, mv[10:11], mv[2:3])
    h_new = h_ref[0] + gate2 * out
    h_out[0] = h_new
    f = _modulated_norm(h_new, n2g_ref[...], mv, is_ctx, 3, 4)
    f_out[0] = f.astype(BF16)
    sc = jax.nn.sigmoid(_nt(wr_ref[...], f, precision=HI))
    gr = sc + br_ref[...]
    rows_ = [gr[e:e + 1, :] for e in range(N_EXPERTS)]
    best_v, best_g = None, None
    for g in range(N_GROUPS):
        x = rows_[g * EPG:(g + 1) * EPG]
        top2 = None
        for i in range(EPG):
            for j in range(i + 1, EPG):
                s_ij = x[i] + x[j]
                top2 = s_ij if top2 is None else jnp.maximum(top2, s_ij)
        if best_v is None:
            best_v, best_g = top2, jnp.zeros_like(top2, dtype=jnp.int32)
        else:
            take = top2 > best_v
            best_v = jnp.where(take, top2, best_v)
            best_g = jnp.where(take, g, best_g)
    cand = []
    for kk in range(EPG):
        ck = rows_[kk]
        for g in range(1, N_GROUPS):
            ck = jnp.where(best_g == g, rows_[g * EPG + kk], ck)
        cand.append(ck)
    v1, i1 = cand[0], jnp.zeros_like(best_g)
    for kk in range(1, EPG):
        take = cand[kk] > v1
        v1 = jnp.where(take, cand[kk], v1)
        i1 = jnp.where(take, kk, i1)
    v2, i2 = jnp.full_like(v1, -jnp.inf), jnp.zeros_like(best_g)
    for kk in range(EPG):
        take = jnp.logical_and(i1 != kk, cand[kk] > v2)
        v2 = jnp.where(take, cand[kk], v2)
        i2 = jnp.where(take, kk, i2)
    e1, e2 = best_g * EPG + i1, best_g * EPG + i2
    e_iota = _iota((N_EXPERTS, tm), 0)
    s1 = jnp.sum(jnp.where(e_iota == e1, sc, 0.0), axis=0, keepdims=True)
    s2 = jnp.sum(jnp.where(e_iota == e2, sc, 0.0), axis=0, keepdims=True)
    tot = s1 + s2
    wt = jnp.where(e_iota == e1, s1 / tot, jnp.where(e_iota == e2, s2 / tot, 0.0))
    eye = (_iota((N_EXPERTS, N_EXPERTS), 0) == _iota((N_EXPERTS, N_EXPERTS), 1)).astype(F32)
    wc_out[0] = _tn(wt, eye, precision=HI)


def _merge_call(a, h, modv, dn_f, dn_b, dn_z, s5_y, s5_u, da_y, gl_f, gl_b, gl_r, vec, glu_w,
                w_gate, b_gate, w_branch, w_out, n2g, w_rt, b_rt, n_ctx):
    b, t, d = h.shape
    tm = 256
    assert t % tm == 0
    tok = lambda w: pl.BlockSpec((1, tm, w), lambda i, j: (i, j, 0))
    fwd = pl.BlockSpec((None, 1, tm, 256), lambda i, j: (0, i, j, 0))
    bwd = pl.BlockSpec((None, 1, tm, 256), lambda i, j: (1, i, j, 0))
    full = lambda *shape: pl.BlockSpec(shape, lambda i, j: (0,) * len(shape))
    return pl.pallas_call(
        functools.partial(_merge_body, n_ctx, tm), grid=(b, t // tm),
        in_specs=[tok(d), tok(d), pl.BlockSpec((1, 16, d), lambda i, j: (i, 0, 0)),
                  tok(256), tok(256), tok(256), fwd, bwd, tok(256), tok(256), tok(256), tok(256),
                  tok(256), full(8, 256), full(256, 256), full(d, 4 * d), full(1, 4 * d),
                  full(4, 256, d), full(d, d), full(1, d), full(N_EXPERTS, d), full(N_EXPERTS, 1)],
        out_specs=[tok(d), tok(d), tok(N_EXPERTS)],
        out_shape=[SDS((b, t, d), F32), SDS((b, t, d), BF16), SDS((b, t, N_EXPERTS), F32)],
        compiler_params=_cp("parallel", "parallel"), name="merge_router")(
            a, h, modv, dn_f, dn_b, dn_z, s5_y, s5_y, s5_u, da_y, gl_f, gl_b, gl_r, vec, glu_w,
            w_gate, b_gate, w_branch, w_out, n2g, w_rt, b_rt)


def _moe_body(n_ctx, tt, final, f_ref, wc_ref, h_ref, mv_ref, wg_ref, wu_ref, wd_ref, fg_ref, o_ref, acc_ref):
    e = pl.program_id(2)

    @pl.when(e == 0)
    def _():
        acc_ref[...] = jnp.zeros_like(acc_ref)

    x = f_ref[0]
    wc = wc_ref[0]
    wcol = jnp.sum(jnp.where(_iota(wc.shape, 1) == e, wc, 0.0), axis=1, keepdims=True)
    g = _dot(x, wg_ref[0])
    hid = g * jax.nn.sigmoid(g) * _dot(x, wu_ref[0])
    acc_ref[...] += _dot(jnp.where(wcol != 0.0, hid * wcol, 0.0).astype(BF16), wd_ref[0])

    @pl.when(e == N_EXPERTS - 1)
    def _():
        row = pl.program_id(1) * tt + _iota((tt, 1), 0)
        mv = mv_ref[0]
        gate = jnp.where(row < n_ctx, mv[13:14], mv[5:6])
        hn = h_ref[0] + gate * acc_ref[...]
        if final:
            hn = hn * lax.rsqrt(jnp.mean(hn * hn, axis=-1, keepdims=True) + NORM_EPS) * fg_ref[...]
        o_ref[0] = hn


def _moe_call(f, wc, h, modv, weg, weu, wed, final_g, n_ctx, final):
    b, t, d = h.shape
    fe = weg.shape[2]
    tt = next(c for c in (1408, 1024, 768, 512, 256, 128, 64, 32, 16, 8) if t % c == 0)
    tok = lambda w: pl.BlockSpec((1, tt, w), lambda i, j, e: (i, j, 0))
    return pl.pallas_call(
        functools.partial(_moe_body, n_ctx, tt, final), grid=(b, t // tt, N_EXPERTS),
        in_specs=[tok(d), tok(N_EXPERTS), tok(d), pl.BlockSpec((1, 16, d), lambda i, j, e: (i, 0, 0)),
                  pl.BlockSpec((1, d, fe), lambda i, j, e: (e, 0, 0)),
                  pl.BlockSpec((1, d, fe), lambda i, j, e: (e, 0, 0)),
                  pl.BlockSpec((1, fe, d), lambda i, j, e: (e, 0, 0)),
                  pl.BlockSpec((1, d), lambda i, j, e: (0, 0))],
        out_specs=tok(d), out_shape=SDS((b, t, d), F32), scratch_shapes=[pltpu.VMEM((tt, d), F32)],
        compiler_params=_cp("parallel", "parallel", "arbitrary"), name="moe_experts")(
            f, wc, h, modv, weg, weu, wed, final_g)


def _rope_tables(n_ctx, s):
    t = jnp.arange(s)
    rowp, colp = (t // GRID_W).astype(F32), (t % GRID_W).astype(F32)
    n_freq = DA_DK // 4
    inv = ROPE_BASE ** (-jnp.arange(n_freq, dtype=F32) / n_freq)
    ang = jnp.concatenate([rowp[:, None] * inv, colp[:, None] * inv], axis=-1)
    cos, sin = jnp.cos(ang), jnp.sin(ang)
    zero = jnp.zeros_like(sin)
    cos32 = jnp.concatenate([cos, cos], axis=-1)
    sa32 = jnp.concatenate([-sin, zero], axis=-1)
    sb32 = jnp.concatenate([zero, sin], axis=-1)
    def full(tab, ctx_val):
        tab = jnp.tile(tab, (1, 8))
        return jnp.concatenate([jnp.full((n_ctx, 256), ctx_val, F32), tab], axis=0)
    return full(cos32, 1.0), full(sa32, 0.0), full(sb32, 0.0)


def _perm_w_in(w_in):
    offs, o = {}, 0
    for name, width in (("dn_qkv", 768), ("dn_z", 256), ("dn_a", 8), ("dn_b", 8), ("s5_u", 256), ("da_q", 256),
                        ("da_k", 256), ("da_v", 256), ("gla_q", 128), ("gla_k", 128), ("gla_v", 256),
                        ("gla_r", 256), ("gla_gate", 32)):
        offs[name] = (o, width)
        o += width
    col = lambda n: w_in[:, offs[n][0]:offs[n][0] + offs[n][1]]
    pad = jnp.zeros((w_in.shape[0], 128 - 48), w_in.dtype)
    return jnp.concatenate([col(n) for n in ("dn_qkv", "dn_z", "s5_u", "da_q", "da_k", "da_v", "gla_q", "gla_k",
                                              "gla_v", "gla_r", "dn_a", "dn_b", "gla_gate")] + [pad], axis=1)


def kernel(x, c, ctx, c_ctx, w_mod, b_mod, norm1_g, norm2_g, w_in, dn_conv, dn_a_log, dn_dt_bias, dn_norm_g,
           s5_a_re, s5_a_im, s5_log_dt, s5_b_re, s5_b_im, s5_c_re, s5_c_im, s5_d, s5_glu_w, s5_glu_b,
           da_lambda, da_norm_g, gla_w_gate, gla_b_gate, gla_norm_g, w_branch, w_gate, b_gate, w_out,
           w_router, b_router, w_e_gate, w_e_up, w_e_down, final_g):
    b, s, d = x.shape
    n_ctx = ctx.shape[1]
    t = n_ctx + s
    depth = w_mod.shape[0]
    assert b + 1 <= 8 and b == 4

    cvec = jnp.zeros((8, d), F32).at[:b].set(c).at[b].set(c_ctx)
    mod_all = _mod_call(cvec, w_mod, b_mod).reshape(depth, 8, 6, d)
    cos_t, sa_t, sb_t = _rope_tables(n_ctx, s)
    h = jnp.concatenate([ctx, x], axis=1)

    lane256 = jnp.arange(256) // 64
    eg = jnp.stack([(jnp.arange(128)[:, None] == (SM_DN_A + dd * 4 + lane256)[None, :]).astype(F32) for dd in range(2)])
    eb = jnp.stack([(jnp.arange(128)[:, None] == (SM_DN_B + dd * 4 + lane256)[None, :]).astype(F32) for dd in range(2)])

    for layer in range(depth):
        lam_init = 0.8 - 0.6 * math.exp(-0.3 * layer)
        ml = mod_all[layer]
        modv = jnp.zeros((b, 16, d), F32).at[:, 0:6].set(ml[:b]).at[:, 8:14].set(jnp.broadcast_to(ml[b], (b, 6, d)))

        a, dn_qkv, dn_z, s5_u, da_qkv, gla_qkv, gla_r, small = _proj_call(
            h, modv, norm1_g[layer][None, :], _perm_w_in(w_in[layer]).astype(BF16), n_ctx)

        dq, dk_, dv, gb = _dn_prep_call(dn_qkv, small, dn_conv[layer], dn_a_log[layer], dn_dt_bias[layer], n_ctx)
        dn_f, dn_b = [_dn_scan_call(dq, dk_, dv, gb, eg[dd:dd + 1], eb[dd:dd + 1], n_ctx, dd == 1)
                      for dd in range(2)]

        m_b, kbd, m_c, al = _s5_operators(
            s5_a_re[layer], s5_a_im[layer], s5_log_dt[layer], s5_b_re[layer], s5_b_im[layer],
            s5_c_re[layer], s5_c_im[layer])
        s5_y = _s5_call(s5_u, m_b, kbd, m_c, al, n_ctx)

        lf = da_lambda[layer]
        lam = jnp.exp(jnp.sum(lf[0] * lf[1])) - jnp.exp(jnp.sum(lf[2] * lf[3])) + lam_init
        lam_row = jnp.full((1, 128), lam, F32)
        da_gain = da_norm_g[layer][:, None]
        aq, ak, avt = _da_prep_call(da_qkv, cos_t, sa_t, sb_t)
        da_lat = _attn_call(aq, ak, avt, lam_row, da_gain, 1.0 - lam_init, n_ctx, s, t, 256, _da_kblock(t))
        if layer < depth - 1:
            da_ctx = _attn_call(aq, ak, avt, lam_row, da_gain, 1.0 - lam_init, 0, n_ctx, n_ctx, 256, n_ctx)
        else:
            da_ctx = jnp.zeros((b, n_ctx, 256), F32)
        da_y = jnp.concatenate([da_ctx, da_lat], axis=1)

        wg2 = jnp.zeros((2, 128, 128), F32)
        for dd in range(2):
            wg2 = wg2.at[dd, SM_GLA + dd * GLA_RANK:SM_GLA + (dd + 1) * GLA_RANK].set(gla_w_gate[layer, dd])
        gl_f, gl_b = [_gla_call(gla_qkv, small, wg2[dd:dd + 1], gla_b_gate[layer][dd:dd + 1, None, :], n_ctx, dd == 1)
                      for dd in range(2)]

        vec = jnp.zeros((8, 256), F32)
        vec = vec.at[0].set(jnp.tile(dn_norm_g[layer], DN_HEADS)).at[1].set(jnp.tile(gla_norm_g[layer], GLA_HEADS))
        vec = vec.at[2].set(s5_d[layer]).at[3].set(s5_glu_b[layer])
        h, f, wc = _merge_call(
            a, h, modv, dn_f, dn_b, dn_z, s5_y, s5_u, da_y, gl_f, gl_b, gla_r, vec,
            s5_glu_w[layer].astype(BF16), w_gate[layer].astype(BF16), b_gate[layer][None, :],
            w_branch[layer].astype(BF16), w_out[layer].astype(BF16), norm2_g[layer][None, :],
            w_router.T, b_router[:, None], n_ctx)

        h = _moe_call(f, wc, h, modv, w_e_gate[layer].astype(BF16), w_e_up[layer].astype(BF16),
                      w_e_down[layer].astype(BF16), final_g[None, :], n_ctx, layer == depth - 1)
    return h[:, n_ctx:]
```

```python
import functools
import math

import jax
import jax.numpy as jnp
from jax import lax
from jax.experimental import pallas as pl
from jax.experimental.pallas import tpu as pltpu

F32 = jnp.float32
BF16 = jnp.bfloat16
HI = lax.Precision.HIGHEST
SDS = jax.ShapeDtypeStruct

GRID_W = 64
DN_HEADS, DN_DK, DN_DV, DN_CONV, DN_CHUNK = 4, 64, 64, 5, 64
DN_BLOCK = 256
DN_STREAMS = 2
S5_WIDTH, S5_GROUP, S5_STATE = 256, 16, 64
S5_GROUPS = S5_WIDTH // S5_GROUP
S5_CHUNK = 16
DA_HEADS, DA_DK, DA_DV = 4, 32, 64
ROPE_BASE = 10000.0
GLA_HEADS, GLA_DK, GLA_DV, GLA_RANK, GLA_TAU, GLA_CHUNK = 4, 32, 64, 16, 16.0, 16
N_EXPERTS, N_GROUPS, TOP_K = 16, 4, 2
EPG = N_EXPERTS // N_GROUPS
NORM_EPS = 1e-6
VMEM_LIMIT = 56 * 1024 * 1024

SEG = dict(dn_qkv=(0, 768), dn_z=(768, 256), s5_u=(1024, 256), da_qkv=(1280, 768),
           gla_qkv=(2048, 512), gla_r=(2560, 256), small=(2816, 128))
D_PROJ = 2944
SM_DN_A, SM_DN_B, SM_GLA = 0, 8, 16


def _cp(*sem):
    return pltpu.CompilerParams(dimension_semantics=sem, vmem_limit_bytes=VMEM_LIMIT)


def _nt(a, b, **kw):
    return lax.dot_general(a, b, (((1,), (1,)), ((), ())), preferred_element_type=F32, **kw)


def _tn(a, b, **kw):
    return lax.dot_general(a, b, (((0,), (0,)), ((), ())), preferred_element_type=F32, **kw)


def _dot(a, b, **kw):
    return jnp.dot(a, b, preferred_element_type=F32, **kw)


def _bdot(a, b):
    return jnp.dot(a.astype(BF16), b.astype(BF16), preferred_element_type=F32)


def _split3(x):
    hi = x.astype(BF16)
    r = x - hi.astype(F32)
    mid = r.astype(BF16)
    return hi, mid, (r - mid.astype(F32)).astype(BF16)


def _dot_sel(x, sel, dot=None):
    dot = dot or _dot
    sel = sel.astype(BF16)
    hi, mid, lo = _split3(x)
    return dot(hi, sel) + dot(mid, sel) + dot(lo, sel)


def _sel_dot(sel, x):
    sel = sel.astype(BF16)
    hi, mid, lo = _split3(x)
    return _dot(sel, hi) + _dot(sel, mid) + _dot(sel, lo)


def _iota(shape, dim):
    return lax.broadcasted_iota(jnp.int32, shape, dim)


def _block_ones(n, blk_shift):
    r, c = _iota((n, n), 0), _iota((n, n), 1)
    return ((r >> blk_shift) == (c >> blk_shift)).astype(F32)


def _head_rms(o, gain, eps, ones64):
    ms = _dot_sel(o * o, ones64) * (1.0 / 64.0)
    return o * lax.rsqrt(ms + eps) * gain


def _mod_body(c_ref, w_ref, b_ref, o_ref):
    c = c_ref[...]
    s = c * jax.nn.sigmoid(c)
    o_ref[0] = _dot(s, w_ref[0], precision=HI) + b_ref[0]


def _mod_call(cvec, w_mod, b_mod):
    n_layer, d, d6 = w_mod.shape
    tn = 1536
    return pl.pallas_call(
        _mod_body, grid=(n_layer, d6 // tn),
        in_specs=[pl.BlockSpec((8, d), lambda l, j: (0, 0)),
                  pl.BlockSpec((1, d, tn), lambda l, j: (l, 0, j)),
                  pl.BlockSpec((1, 1, tn), lambda l, j: (l, 0, j))],
        out_specs=pl.BlockSpec((1, 8, tn), lambda l, j: (l, 0, j)),
        out_shape=SDS((n_layer, 8, d6), F32), compiler_params=_cp("parallel", "parallel"),
        name="adaln_mod")(cvec, w_mod, b_mod.reshape(n_layer, 1, d6))


def _modulated_norm(x, gain, mv, is_ctx, k_shift, k_scale):
    y = x * lax.rsqrt(jnp.mean(x * x, axis=-1, keepdims=True) + NORM_EPS) * gain
    shift = jnp.where(is_ctx, mv[8 + k_shift:9 + k_shift], mv[k_shift:k_shift + 1])
    scale = jnp.where(is_ctx, mv[8 + k_scale:9 + k_scale], mv[k_scale:k_scale + 1])
    return y * (1.0 + scale) + shift


def _proj_body(n_ctx, tm, h_ref, mv_ref, g_ref, w_ref, a_ref, *out_refs):
    t = pl.program_id(1)
    row = t * tm + _iota((tm, 1), 0)
    a = _modulated_norm(h_ref[0], g_ref[...], mv_ref[0], row < n_ctx, 0, 1)
    ab = a.astype(BF16)
    a_ref[0] = ab
    p = _dot(ab, w_ref[...])
    for ref, (off, width) in zip(out_refs, SEG.values()):
        ref[0] = p[:, off:off + width].astype(ref.dtype)


def _proj_call(h, modv, gain, w_perm, n_ctx):
    b, t, d = h.shape
    tm = 384
    assert t % tm == 0
    outs = [SDS((b, t, d), BF16)] + [SDS((b, t, w), BF16 if name == "s5_u" else F32)
                                     for name, (_, w) in SEG.items()]
    out_specs = [pl.BlockSpec((1, tm, d), lambda i, j: (i, j, 0))] + [
        pl.BlockSpec((1, tm, w), lambda i, j: (i, j, 0)) for _, w in SEG.values()]
    return pl.pallas_call(
        functools.partial(_proj_body, n_ctx, tm), grid=(b, t // tm),
        in_specs=[pl.BlockSpec((1, tm, d), lambda i, j: (i, j, 0)),
                  pl.BlockSpec((1, 16, d), lambda i, j: (i, 0, 0)),
                  pl.BlockSpec((1, d), lambda i, j: (0, 0)),
                  pl.BlockSpec((d, D_PROJ), lambda i, j: (0, 0))],
        out_specs=out_specs, out_shape=outs, compiler_params=_cp("parallel", "parallel"),
        name="norm_proj")(h, modv, gain, w_perm)


def _dn_prep_body(nct, ntile, tm, x_ref, pv_ref, nx_ref, sm_ref, w_ref, alog_ref, dtb_ref,
                  q_ref, k_ref, v_ref, gb_ref):
    t = pl.program_id(1)
    first = jnp.logical_or(t == 0, t == nct)
    last = jnp.logical_or(t == nct - 1, t == ntile - 1)
    x = x_ref[0]
    pv = jnp.where(first, 0.0, pv_ref[0])
    nx = jnp.where(last, 0.0, nx_ref[0])
    w = w_ref[...]
    row = _iota((tm, 1), 0)
    xm1 = jnp.where(row == 0, pv[7:8], pltpu.roll(x, 1, 0))
    xm2 = jnp.where(row == 0, pv[6:7], jnp.where(row == 1, pv[7:8], pltpu.roll(x, 2, 0)))
    xp1 = jnp.where(row == tm - 1, nx[0:1], pltpu.roll(x, tm - 1, 0))
    xp2 = jnp.where(row == tm - 2, nx[0:1], jnp.where(row == tm - 1, nx[1:2], pltpu.roll(x, tm - 2, 0)))
    acc = w[0:1] * xm2 + w[1:2] * xm1 + w[2:3] * x + w[3:4] * xp1 + w[4:5] * xp2
    y = acc * jax.nn.sigmoid(acc)
    ones64 = _block_ones(256, 6)
    q, k = y[:, 0:256], y[:, 256:512]
    qs = _dot_sel(q * q, ones64)
    ks = _dot_sel(k * k, ones64)
    q_ref[0] = q * lax.rsqrt(qs + 1e-6) * (DN_DK ** -0.5)
    k_ref[0] = k * lax.rsqrt(ks + 1e-6)
    v_ref[0] = y[:, 512:768]
    sm = sm_ref[0]
    lane = _iota((1, 128), 1)
    g = -jnp.exp(alog_ref[...]) * jax.nn.softplus(sm + dtb_ref[...])
    be = jax.nn.sigmoid(sm)
    gb_ref[0] = jnp.where(lane < 8, g, jnp.where(lane < 16, be, 0.0))


def _dn_prep_call(qkv, small, conv_w, a_log, dt_bias, n_ctx):
    b, t, c = qkv.shape
    tm = 256
    assert n_ctx % tm == 0 and t % tm == 0
    nct, ntile, r8 = n_ctx // tm, t // tm, tm // 8
    w8 = jnp.zeros((8, c), F32).at[:DN_CONV].set(conv_w)
    alog = jnp.zeros((1, 128), F32).at[0, :8].set(a_log.reshape(-1))
    dtb = jnp.zeros((1, 128), F32).at[0, :8].set(dt_bias.reshape(-1))
    full = lambda shape: pl.BlockSpec(shape, lambda i, j: (0,) * len(shape))
    o256 = pl.BlockSpec((1, tm, 256), lambda i, j: (i, j, 0))
    return pl.pallas_call(
        functools.partial(_dn_prep_body, nct, ntile, tm), grid=(b, ntile),
        in_specs=[pl.BlockSpec((1, tm, c), lambda i, j: (i, j, 0)),
                  pl.BlockSpec((1, 8, c), lambda i, j: (i, jnp.maximum(j * r8 - 1, 0), 0)),
                  pl.BlockSpec((1, 8, c), lambda i, j: (i, jnp.minimum((j + 1) * r8, t // 8 - 1), 0)),
                  pl.BlockSpec((1, tm, 128), lambda i, j: (i, j, 0)),
                  full((8, c)), full((1, 128)), full((1, 128))],
        out_specs=[o256, o256, o256, pl.BlockSpec((1, tm, 128), lambda i, j: (i, j, 0))],
        out_shape=[SDS((b, t, 256), F32)] * 3 + [SDS((b, t, 128), F32)],
        compiler_params=_cp("parallel", "parallel"), name="dn_prep")(qkv, qkv, qkv, small, w8, alog, dtb)


def _scan_order(rev, step, n_first, n_total):
    back = jnp.where(step < n_first, n_first - 1 - step, n_total - 1 - (step - n_first))
    return jnp.where(rev, back, step)


def _dn_scan_body(rev, nst, nch, q_ref, k_ref, v_ref, gb_ref, eg_ref, eb_ref, o_ref, s_ref):
    c = DN_CHUNK
    tm = nst * nch * c
    chunks = range(nst * nch)
    rows = lambda x, ci: x[ci * c:(ci + 1) * c]

    @pl.when(pl.program_id(1) == 0)
    def _():
        s_ref[...] = jnp.zeros_like(s_ref)

    q, k, v, gb = [r[...].reshape(tm, r.shape[2]) for r in (q_ref, k_ref, v_ref, gb_ref)]
    gexp = _dot_sel(gb, eg_ref[0])
    bexp = _dot_sel(gb, eb_ref[0])
    r_i, c_i = _iota((c, c), 0), _iota((c, c), 1)
    tri = (c_i >= r_i) if rev else (c_i <= r_i)
    gc = jnp.concatenate([_sel_dot(tri, rows(gexp, ci)) for ci in chunks], axis=0)
    egc = jnp.exp(gc)
    kb = k * bexp
    rhs_u, rhs_w = v * bexp, kb * egc
    rr, cc = _iota((256, 256), 0), _iota((256, 256), 1)
    same_head = (rr >> 6) == (cc >> 6)
    i_idx = _iota((c, 256), 0)
    j_idx = _iota((c, 256), 1) & 63
    lane_h = _iota((c, 256), 1) >> 6
    d_ij = (j_idx - i_idx) if rev else (i_idx - j_idx)
    incl, strict = d_ij >= 0, d_ij > 0
    eye = (i_idx == j_idx).astype(F32)

    def stack4(x):
        return jnp.concatenate([jnp.where(lane_h == h, x, 0.0) for h in range(DN_HEADS)], axis=0).astype(BF16)

    def to_bd(xcat):
        return jnp.where(same_head, jnp.concatenate([xcat] * DN_HEADS, axis=0), 0.0).astype(BF16)

    both = lambda x, y: jnp.concatenate([x, y], axis=0).astype(BF16)
    gcc = [rows(gc, ci) for ci in chunks]
    gl = [g[0:1, :] if rev else g[c - 1:c, :] for g in gcc]
    gcr = [jnp.sum(jnp.where(d_ij <= 0, rows(gexp, ci), 0.0), axis=0, keepdims=True) for ci in chunks]
    decay = [jnp.where(incl, jnp.exp(jnp.where(incl, gcc[ci] - gcr[ci], 0.0)), 0.0) for ci in chunks]
    k4 = [stack4(rows(k, ci)) for ci in chunks]
    kq = [_nt(both(rows(kb, ci), rows(q, ci)), k4[ci]) for ci in chunks]
    lower = [jnp.where(strict, kq[ci][:c] * decay[ci], 0.0) for ci in chunks]
    a_intra = [jnp.where(incl, kq[ci][c:] * decay[ci], 0.0).astype(BF16) for ci in chunks]
    p = [eye - lower[ci] for ci in chunks]
    m = [_dot(lower[ci].astype(BF16), to_bd(lower[ci])) for ci in chunks]
    for it in range(5):
        mb = [to_bd(m[ci]) for ci in chunks]
        if it < 4:
            pm = [_dot(both(p[ci], m[ci]), mb[ci]) for ci in chunks]
            p = [p[ci] + pm[ci][:c] for ci in chunks]
            m = [pm[ci][c:] for ci in chunks]
        else:
            p = [p[ci] + _dot(p[ci].astype(BF16), mb[ci]) for ci in chunks]
    uw = [_dot(p[ci].astype(BF16), jnp.concatenate([stack4(rows(rhs_u, ci)), stack4(rows(rhs_w, ci))], axis=1))
          for ci in chunks]
    wq = [both(uw[ci][:, 256:], rows(q, ci) * rows(egc, ci)) for ci in chunks]
    k_dec = [(rows(k, ci) * jnp.exp(gl[ci] - gcc[ci])).astype(BF16) for ci in chunks]
    s = [s_ref[st] for st in range(nst)]
    outs = [None] * (nst * nch)
    for step in range(nch):
        for st in range(nst):
            ci = st * nch + (nch - 1 - step if rev else step)
            r = _dot(wq[ci], s[st].astype(BF16))
            v_new = uw[ci][:, :256] - r[:c]
            outs[ci] = r[c:] + _dot(a_intra[ci], stack4(v_new))
            s[st] = s[st] * jnp.exp(gl[ci]) + jnp.where(same_head, _tn(k_dec[ci], v_new.astype(BF16)), 0.0)
    for st in range(nst):
        s_ref[st] = s[st]
    o_ref[...] = jnp.concatenate(outs, axis=0).reshape(nst, nch * c, 256)


def _dn_scan_call(q, k, v, gb, eg, eb, n_ctx, rev):
    b, t, _ = q.shape
    tm = DN_BLOCK
    assert t % tm == 0 and n_ctx % tm == 0 and tm % DN_CHUNK == 0
    blk_of = lambda j: _scan_order(rev, j, n_ctx // tm, t // tm)
    nst = DN_STREAMS if b % DN_STREAMS == 0 else 1
    blk = lambda w: pl.BlockSpec((nst, tm, w), lambda i, j: (i, blk_of(j), 0))
    return pl.pallas_call(
        functools.partial(_dn_scan_body, rev, nst, tm // DN_CHUNK), grid=(b // nst, t // tm),
        in_specs=[blk(256), blk(256), blk(256), blk(128),
                  pl.BlockSpec((1, 128, 256), lambda i, j: (0, 0, 0)),
                  pl.BlockSpec((1, 128, 256), lambda i, j: (0, 0, 0))],
        out_specs=blk(256), out_shape=SDS((b, t, 256), F32),
        scratch_shapes=[pltpu.VMEM((nst, 256, 256), F32)],
        compiler_params=_cp("parallel", "arbitrary"), name="dn_scan")(q, k, v, gb, eg, eb)


def _s5_in_body(u_ref, mb_ref, o_ref):
    w = mb_ref.shape[2]
    acc = None
    for j in range(S5_CHUNK):
        part = _dot(u_ref[:, j * w:(j + 1) * w].astype(BF16), mb_ref[0, j])
        acc = part if acc is None else acc + part
    o_ref[0] = acc


def _s5_scan_body(n_sub, n_sub_ctx, bc_ref, al_ref, x_ref):
    rev = pl.program_id(0) == 1
    half = bc_ref.shape[1] // 2
    ar, ai = al_ref[0:1, :], al_ref[1:2, :]

    def sub(s, st):
        xr, xi = st
        rows = pl.ds(pl.multiple_of(_scan_order(rev, s, n_sub_ctx, n_sub) * 8, 8), 8)
        blk = bc_ref[rows, :]
        seen = []
        for kk in range(8):
            seen.append(jnp.concatenate([xr, xi], axis=1))
            inp = jnp.where(rev, blk[7 - kk:8 - kk], blk[kk:kk + 1])
            xr, xi = xr * ar - xi * ai + inp[:, :half], xi * ar + xr * ai + inp[:, half:]
        x_ref[rows, :] = jnp.concatenate([jnp.where(rev, seen[7 - r], seen[r]) for r in range(8)], axis=0)
        return xr, xi

    zero = jnp.zeros((1, half), F32)
    lax.fori_loop(0, n_sub, sub, (zero, zero))


def _s5_out_body(u_ref, x_ref, m_ref, y_ref, lhs_ref):
    lanes = x_ref.shape[2]

    @pl.when(pl.program_id(2) == 0)
    def _():
        lhs_ref[:, :lanes] = x_ref[0].astype(BF16)
        lhs_ref[:, lanes:] = u_ref[...].astype(BF16)

    y_ref[...] = _dot(lhs_ref[...], m_ref[0]).astype(y_ref.dtype)


def _s5_call(u, m_b, m_out, al, n_ctx):
    b, t, w = u.shape
    L = S5_CHUNK
    n = t // L
    r = b * n
    tr = n
    lanes = m_b.shape[-1]
    u3 = u.reshape(r, L * w)
    ublk = pl.BlockSpec((tr, L * w), lambda d, i, k: (i, 0))
    nt = 512
    bc = pl.pallas_call(
        _s5_in_body, grid=(2, r // tr, lanes // nt),
        in_specs=[ublk, pl.BlockSpec((1, L, w, nt), lambda d, i, k: (d, 0, 0, k))],
        out_specs=pl.BlockSpec((1, tr, nt), lambda d, i, k: (d, i, k)),
        out_shape=SDS((2, r, lanes), F32), compiler_params=_cp("parallel", "parallel", "parallel"),
        name="s5_in")(u3, m_b)
    assert n % 8 == 0 and (n_ctx // L) % 8 == 0
    xprev = pl.pallas_call(
        functools.partial(_s5_scan_body, n // 8, n_ctx // L // 8), grid=(2, b),
        in_specs=[pl.BlockSpec((None, None, n, lanes), lambda d, i: (d, i, 0, 0)),
                  pl.BlockSpec((None, 8, lanes // 2), lambda d, i: (d, 0, 0))],
        out_specs=pl.BlockSpec((None, None, n, lanes), lambda d, i: (d, i, 0, 0)),
        out_shape=SDS((2, b, n, lanes), F32), compiler_params=_cp("parallel", "parallel"),
        name="s5_scan")(bc.reshape(2, b, n, lanes), al)
    y = pl.pallas_call(
        _s5_out_body, grid=(2, r // tr, L),
        in_specs=[ublk, pl.BlockSpec((1, tr, lanes), lambda d, i, k: (d, i, 0)),
                  pl.BlockSpec((1, lanes + L * w, w), lambda d, i, k: (d, 0, k))],
        out_specs=pl.BlockSpec((None, tr, w), lambda d, i, k: (d, i, k)),
        out_shape=SDS((2, r, L * w), BF16), scratch_shapes=[pltpu.VMEM((tr, lanes + L * w), BF16)],
        compiler_params=_cp("parallel", "parallel", "arbitrary"), name="s5_out")(
            u3, xprev.reshape(2, r, lanes), m_out)
    return y.reshape(2, b, t, w)


def _s5_operators(a_re, a_im, log_dt, b_re, b_im, c_re, c_im):
    L = S5_CHUNK
    G, P, C = S5_GROUPS, S5_STATE, S5_GROUP
    dt = jnp.exp(log_dt)[:, :, None]
    ls = jnp.arange(L + 1, dtype=F32)
    mag = jnp.exp(a_re[..., None] * dt[..., None] * ls)
    ang = a_im[..., None] * dt[..., None] * ls
    pw_re, pw_im = mag * jnp.cos(ang), mag * jnp.sin(ang)
    ab_re, ab_im = pw_re[..., 1], pw_im[..., 1]
    den = a_re * a_re + a_im * a_im
    f_re = ((ab_re - 1.0) * a_re + ab_im * a_im) / den
    f_im = (ab_im * a_re - (ab_re - 1.0) * a_im) / den
    bb_re = f_re[..., None] * b_re[None] - f_im[..., None] * b_im[None]
    bb_im = f_re[..., None] * b_im[None] + f_im[..., None] * b_re[None]
    eye = jnp.eye(G, dtype=F32)
    ca_re = c_re[..., None] * pw_re[:, :, None] - c_im[..., None] * pw_im[:, :, None]
    ca_im = c_re[..., None] * pw_im[:, :, None] + c_im[..., None] * pw_re[:, :, None]
    kern = (jnp.einsum("dgopl,dgpc->dglco", ca_re, bb_re, precision=HI)
            - jnp.einsum("dgopl,dgpc->dglco", ca_im, bb_im, precision=HI))
    kbd = jnp.einsum("dglco,gh->dlgcho", kern[:, :, :L], eye).reshape(2, L, G * C, G * C)
    q_re, q_im = pw_re[..., :L], pw_im[..., :L]
    mb_re = q_re[..., None] * bb_re[:, :, :, None] - q_im[..., None] * bb_im[:, :, :, None]
    mb_im = q_re[..., None] * bb_im[:, :, :, None] + q_im[..., None] * bb_re[:, :, :, None]

    def by_step(m):
        m = jnp.stack([m[0][:, :, ::-1], m[1]])
        return jnp.einsum("dgpjc,gh->djgchp", m, eye).reshape(2, L, G * C, G * P)

    m_b = jnp.concatenate([by_step(mb_re), by_step(mb_im)], axis=-1)
    def readout(w):
        w = jnp.stack([w[0], w[1][..., ::-1]])
        return jnp.einsum("dgcpj,gh->dgpjhc", w, eye).reshape(2, G * P, L * G * C)

    m_c = jnp.concatenate([readout(ca_re[..., 1:]), -readout(ca_im[..., 1:])], axis=1)
    j_in, j_out = jnp.arange(L)[:, None], jnp.arange(L)[None, :]
    toep = []
    for d, allowed in enumerate((j_in <= j_out, j_in >= j_out)):
        blocks = jnp.where(allowed[:, :, None, None], kbd[d][jnp.abs(j_out - j_in)], 0.0)
        toep.append(blocks.transpose(0, 2, 1, 3).reshape(L * G * C, L * G * C))
    m_out = jnp.concatenate([m_c, jnp.stack(toep)], axis=1)
    al = jnp.stack([pw_re[..., L].reshape(2, G * P), pw_im[..., L].reshape(2, G * P)], axis=1)
    al = jnp.concatenate([al, jnp.zeros((2, 6, G * P), F32)], axis=1)
    return m_b.astype(BF16), m_out.astype(BF16), al


def _da_prep_body(x_ref, cos_ref, sa_ref, sb_ref, q_ref, k_ref, vt_ref):
    x = x_ref[0]
    cos, sa, sb = cos_ref[...], sa_ref[...], sb_ref[...]

    def rope(z):
        return z * cos + pltpu.roll(z, 240, 1) * sa + pltpu.roll(z, 16, 1) * sb

    q_ref[0] = (rope(x[:, 0:256]) * (DA_DK ** -0.5 * LOG2E)).astype(BF16)
    k_ref[0] = rope(x[:, 256:512]).astype(BF16)
    vt = x[:, 512:768].T.astype(BF16)
    ones = jnp.ones((DA_VROWS - DA_DV, vt.shape[1]), BF16)
    vt_ref[0, 0] = jnp.concatenate(
        [blk for h in range(DA_HEADS) for blk in (vt[h * DA_DV:(h + 1) * DA_DV], ones)], axis=0)


def _da_kblock(t):
    return next(c for c in (1408, 768, 512, 256) if t % c == 0)


DA_VROWS = DA_DV + 16
LOG2E = 1.4426950408889634


def _da_prep_call(qkv, cos, sa, sb):
    b, t, _ = qkv.shape
    tm = _da_kblock(t)
    assert t % tm == 0
    tab = pl.BlockSpec((tm, 256), lambda i, j: (j, 0))
    o = pl.BlockSpec((1, tm, 256), lambda i, j: (i, j, 0))
    return pl.pallas_call(
        _da_prep_body, grid=(b, t // tm),
        in_specs=[pl.BlockSpec((1, tm, 768), lambda i, j: (i, j, 0)), tab, tab, tab],
        out_specs=[o, o, pl.BlockSpec((1, 1, DA_HEADS * DA_VROWS, tm), lambda i, j: (i, j, 0, 0))],
        out_shape=[SDS((b, t, 256), BF16)] * 2 + [SDS((b, t // tm, DA_HEADS * DA_VROWS, tm), BF16)],
        compiler_params=_cp("parallel", "parallel"), name="da_prep")(qkv, cos, sa, sb)


def _attn_body(nkb, tk, tq, post, q_ref, k_ref, vt_ref, lam_ref, g_ref, o_ref, q8_ref, acc_ref, s0_ref, p7_ref,
               a7_ref):
    q = q_ref[0]
    lane = _iota((tq, 256), 1)
    for hm in range(8):
        q8_ref[hm] = jnp.where((lane >> 5) == hm, q, jnp.zeros_like(q))
    acc_ref[...] = jnp.zeros_like(acc_ref)

    def keys(i):
        return k_ref[0, pl.ds(pl.multiple_of(i * tk, 8), tk), :]

    s0_ref[...] = _nt(keys(0), q8_ref[0])

    nsplit = 4 if tk % 32 == 0 else 1
    p7_ref[...] = jnp.zeros_like(p7_ref)
    a7_ref[...] = jnp.ones_like(a7_ref)

    def values(i, hm):
        h = hm // 2
        return vt_ref[0, i][h * DA_VROWS:(h + 1) * DA_VROWS, :tk]

    def accumulate(hm, vals, p, alpha):
        acc_ref[hm] = acc_ref[hm] * alpha + _dot(vals, p)

    def kblock(i, carry):
        m = carry
        kb = keys(i)
        kb_next = keys(jnp.minimum(i + 1, nkb - 1))
        m_rows = []
        s = s0_ref[...]
        pending = (7, values(jnp.maximum(i - 1, 0), 7), p7_ref[...], a7_ref[0:1, :])
        for hm in range(8):
            s_next = _nt(kb, q8_ref[hm + 1]) if hm < 7 else _nt(kb_next, q8_ref[0])
            accumulate(*pending)
            m_old = m[hm:hm + 1, :]
            part = s[0:tk // nsplit]
            for r in range(1, nsplit):
                part = jnp.maximum(part, s[r * tk // nsplit:(r + 1) * tk // nsplit])
            m_new = jnp.maximum(m_old, jnp.max(part, axis=0, keepdims=True))
            alpha = jnp.exp2(m_old - m_new)
            p = jnp.exp2(s - m_new).astype(BF16)
            m_rows.append(m_new)
            pending = (hm, values(i, hm), p, alpha)
            s = s_next
        s0_ref[...] = s
        p7_ref[...] = pending[2]
        a7_ref[0:1, :] = pending[3]
        return jnp.concatenate(m_rows, axis=0)

    lax.fori_loop(0, nkb, kblock, jnp.full((8, tq), -1e30, F32))
    accumulate(7, values(nkb - 1, 7), p7_ref[...], a7_ref[0:1, :])
    lam = lam_ref[0:1, 0:1]
    heads = []
    for h in range(DA_HEADS):
        a1, a2 = acc_ref[2 * h], acc_ref[2 * h + 1]
        o = a1[0:DA_DV] / a1[DA_DV:DA_DV + 1] - lam * (a2[0:DA_DV] / a2[DA_DV:DA_DV + 1])
        ms = jnp.mean(o * o, axis=0, keepdims=True)
        heads.append(o * lax.rsqrt(ms + 1e-5) * g_ref[...] * post)
    o_ref[0] = jnp.concatenate(heads, axis=0).T


def _attn_call(q, k, vt, lam_row, gain_col, post, q_off, n_q, n_keys, tq, tk):
    b = q.shape[0]
    assert n_keys % tk == 0 and n_q % tq == 0 and q_off % tq == 0
    nkb = n_keys // tk
    assert tk == vt.shape[3] or nkb == 1
    qo = q_off // tq
    return pl.pallas_call(
        functools.partial(_attn_body, nkb, tk, tq, post), grid=(b, n_q // tq),
        in_specs=[pl.BlockSpec((1, tq, 256), lambda i, j: (i, j + qo, 0)),
                  pl.BlockSpec((1, n_keys, 256), lambda i, j: (i, 0, 0)),
                  pl.BlockSpec((1, nkb, vt.shape[2], vt.shape[3]), lambda i, j: (i, 0, 0, 0)),
                  pl.BlockSpec((1, 128), lambda i, j: (0, 0)),
                  pl.BlockSpec((DA_DV, 1), lambda i, j: (0, 0))],
        out_specs=pl.BlockSpec((1, tq, 256), lambda i, j: (i, j, 0)),
        out_shape=SDS((b, n_q, 256), F32),
        scratch_shapes=[pltpu.VMEM((8, tq, 256), BF16), pltpu.VMEM((8, DA_VROWS, tq), F32),
                        pltpu.VMEM((tk, tq), F32), pltpu.VMEM((tk, tq), BF16), pltpu.VMEM((8, tq), F32)],
        compiler_params=_cp("parallel", "parallel"), name="diff_attn")(q, k, vt, lam_row, gain_col)


def _gla_body(rev, n, qkv_ref, sm_ref, wg_ref, bg_ref, o_ref, st_ref):
    c = GLA_CHUNK
    tm = n * c

    @pl.when(pl.program_id(1) == 0)
    def _():
        st_ref[...] = jnp.zeros_like(st_ref)

    x = qkv_ref[0]
    q, k, v = x[:, 0:128] * (GLA_DK ** -0.5), x[:, 128:256], x[:, 256:512]
    z = _dot(sm_ref[0], wg_ref[0], precision=HI) + bg_ref[0]
    la = jax.nn.log_sigmoid(z) * (1.0 / GLA_TAU)
    r_i, c_i = _iota((tm, tm), 0), _iota((tm, tm), 1)
    before = (c_i >= r_i) if rev else (c_i <= r_i)
    tri = jnp.logical_and((r_i >> 4) == (c_i >> 4), before)
    b = _sel_dot(tri, la)
    chunk_sel = (_iota((tm, 128), 0) >> 4) == _iota((tm, 128), 1)
    a_cols = jnp.exp(_dot_sel(la, chunk_sel, dot=_tn))
    b3, q3, k3 = b.reshape(n, c, 128), q.reshape(n, c, 128), k.reshape(n, c, 128)
    v3 = v.reshape(n, c, 256)
    bl3 = b3[:, 0:1, :] if rev else b3[:, c - 1:c, :]
    row3 = _iota((n, c, 128), 1)
    head_ones = ((_iota((128, 256), 0) >> 5) == (_iota((128, 256), 1) >> 6)).astype(BF16)
    o3 = jnp.zeros((n, c, 256), F32)
    for j in range(c):
        ok = (row3 <= j) if rev else (row3 >= j)
        e = jnp.where(ok, jnp.exp(jnp.where(ok, b3 - b3[:, j:j + 1, :], 0.0)), 0.0)
        part = (e * q3 * k3[:, j:j + 1, :]).reshape(tm, 128).astype(BF16)
        o3 = o3 + _dot(part, head_ones).reshape(n, c, 256) * v3[:, j:j + 1, :]
    q_dec = (q3 * jnp.exp(b3)).astype(BF16)
    k_dec = (k3 * jnp.exp(bl3 - b3)).astype(BF16)
    vb = v3.astype(BF16)
    st_mask = (_iota((128, 256), 0) >> 5) == (_iota((128, 256), 1) >> 6)
    order = range(n - 1, -1, -1) if rev else range(n)
    upd = {ci: jnp.where(st_mask, _tn(k_dec[ci], vb[ci]), 0.0) for ci in order}
    st = st_ref[...]
    inter = [None] * n
    for ci in order:
        inter[ci] = _dot(q_dec[ci], st.astype(BF16))
        st = st * a_cols[:, ci:ci + 1] + upd[ci]
    st_ref[...] = st
    o_ref[0] = jnp.concatenate(inter, axis=0) + o3.reshape(tm, 256)


def _gla_call(qkv, small, wg, bg, n_ctx, rev):
    b, t, _ = qkv.shape
    tm = 256
    assert t % tm == 0 and n_ctx % tm == 0 and tm // GLA_CHUNK <= 128 and GLA_CHUNK == 16
    blk = lambda i, j: _scan_order(rev, j, n_ctx // tm, t // tm)
    return pl.pallas_call(
        functools.partial(_gla_body, rev, tm // GLA_CHUNK), grid=(b, t // tm),
        in_specs=[pl.BlockSpec((1, tm, 512), lambda i, j: (i, blk(i, j), 0)),
                  pl.BlockSpec((1, tm, 128), lambda i, j: (i, blk(i, j), 0)),
                  pl.BlockSpec((1, 128, 128), lambda i, j: (0, 0, 0)),
                  pl.BlockSpec((1, 1, 128), lambda i, j: (0, 0, 0))],
        out_specs=pl.BlockSpec((1, tm, 256), lambda i, j: (i, blk(i, j), 0)),
        out_shape=SDS((b, t, 256), F32), scratch_shapes=[pltpu.VMEM((128, 256), F32)],
        compiler_params=_cp("parallel", "arbitrary"), name="gla_scan")(qkv, small, wg, bg)


def _merge_body(n_ctx, tm, a_ref, h_ref, mv_ref, dnf_ref, dnr_ref, dnz_ref, s5y_ref, s5yr_ref, s5u_ref,
                da_ref, dac_ref, glf_ref, glr_ref, glz_ref, vec_ref, gluw_ref, wgate_ref, bgate_ref, wbr_ref,
                wout_ref, n2g_ref, wr_ref, br_ref, h_out, f_out, wc_out):
    t = pl.program_id(1)
    d = h_ref.shape[2]
    row = t * tm + _iota((tm, 1), 0)
    is_ctx = row < n_ctx
    mv = mv_ref[0]
    vec = vec_ref[...]
    ones64 = _block_ones(256, 6)
    z = dnz_ref[0]
    ya = _head_rms(dnf_ref[0] + dnr_ref[0], vec[0:1], NORM_EPS, ones64) * (z * jax.nn.sigmoid(z))
    u = s5u_ref[0].astype(F32)
    zz = jax.nn.gelu(s5y_ref[0].astype(F32) + s5yr_ref[0].astype(F32) + vec[2:3] * u)
    yb = zz * jax.nn.sigmoid(_bdot(zz, gluw_ref[...]) + vec[3:4])
    r = glz_ref[0]
    yd = _head_rms(glf_ref[0] + glr_ref[0], vec[1:2], NORM_EPS, ones64) * (r * jax.nn.sigmoid(r))
    a = a_ref[0]
    acc = jnp.zeros((tm, d), F32)
    yc = jnp.where(t * tm < n_ctx, dac_ref[0], da_ref[0])
    for i, y in enumerate((ya, yb, yc, yd)):
        gate = jax.nn.sigmoid(_dot(a, wgate_ref[:, i * d:(i + 1) * d]) + bgate_ref[:, i * d:(i + 1) * d])
        acc = acc + gate * _dot(y.astype(BF16), wbr_ref[i])
    out = _dot(acc.astype(BF16), wout_ref[...])
    gate2 = jnp.where(is_ctx, mv[10:11], mv[2:3])
    h_new = h_ref[0] + gate2 * out
    h_out[0] = h_new
    f = _modulated_norm(h_new, n2g_ref[...], mv, is_ctx, 3, 4)
    f_out[0] = f.astype(BF16)
    f_hi = f.astype(BF16)
    f_lo = (f - f_hi.astype(F32)).astype(BF16)
    wr = wr_ref[...]
    wr_hi = wr.astype(BF16)
    wr_lo = (wr - wr_hi.astype(F32)).astype(BF16)
    lg = _nt(jnp.concatenate([wr_hi, wr_lo], axis=0), f_hi)
    sc = jax.nn.sigmoid(lg[:N_EXPERTS] + lg[N_EXPERTS:] + _nt(wr_hi, f_lo))
    gr = sc + br_ref[...]
    rows_ = [gr[e:e + 1, :] for e in range(N_EXPERTS)]
    best_v, best_g = None, None
    for g in range(N_GROUPS):
        x = rows_[g * EPG:(g + 1) * EPG]
        top2 = None
        for i in range(EPG):
            for j in range(i + 1, EPG):
                s_ij = x[i] + x[j]
                top2 = s_ij if top2 is None else jnp.maximum(top2, s_ij)
        if best_v is None:
            best_v, best_g = top2, jnp.zeros_like(top2, dtype=jnp.int32)
        else:
            take = top2 > best_v
            best_v = jnp.where(take, top2, best_v)
            best_g = jnp.where(take, g, best_g)
    cand = []
    for kk in range(EPG):
        ck = rows_[kk]
        for g in range(1, N_GROUPS):
            ck = jnp.where(best_g == g, rows_[g * EPG + kk], ck)
        cand.append(ck)
    v1, i1 = cand[0], jnp.zeros_like(best_g)
    for kk in range(1, EPG):
        take = cand[kk] > v1
        v1 = jnp.where(take, cand[kk], v1)
        i1 = jnp.where(take, kk, i1)
    v2, i2 = jnp.full_like(v1, -jnp.inf), jnp.zeros_like(best_g)
    for kk in range(EPG):
        take = jnp.logical_and(i1 != kk, cand[kk] > v2)
        v2 = jnp.where(take, cand[kk], v2)
        i2 = jnp.where(take, kk, i2)
    e1, e2 = best_g * EPG + i1, best_g * EPG + i2
    e_iota = _iota((N_EXPERTS, tm), 0)
    s1 = jnp.sum(jnp.where(e_iota == e1, sc, 0.0), axis=0, keepdims=True)
    s2 = jnp.sum(jnp.where(e_iota == e2, sc, 0.0), axis=0, keepdims=True)
    tot = s1 + s2
    wt = jnp.where(e_iota == e1, s1 / tot, jnp.where(e_iota == e2, s2 / tot, 0.0))
    eye = (_iota((N_EXPERTS, N_EXPERTS), 0) == _iota((N_EXPERTS, N_EXPERTS), 1)).astype(F32)
    wc_out[0] = _dot_sel(wt, eye, dot=_tn)


def _merge_call(a, h, modv, dn_f, dn_b, dn_z, s5_y, s5_u, da_lat, da_ctx, gl_f, gl_b, gl_r, vec, glu_w,
                w_gate, b_gate, w_branch, w_out, n2g, w_rt, b_rt, n_ctx):
    b, t, d = h.shape
    tm = 256
    assert t % tm == 0
    tok = lambda w: pl.BlockSpec((1, tm, w), lambda i, j: (i, j, 0))
    fwd = pl.BlockSpec((None, 1, tm, 256), lambda i, j: (0, i, j, 0))
    bwd = pl.BlockSpec((None, 1, tm, 256), lambda i, j: (1, i, j, 0))
    full = lambda *shape: pl.BlockSpec(shape, lambda i, j: (0,) * len(shape))
    return pl.pallas_call(
        functools.partial(_merge_body, n_ctx, tm), grid=(b, t // tm),
        in_specs=[tok(d), tok(d), pl.BlockSpec((1, 16, d), lambda i, j: (i, 0, 0)),
                  tok(256), tok(256), tok(256), fwd, bwd, tok(256),
                  pl.BlockSpec((1, tm, 256), lambda i, j: (i, jnp.maximum(j - n_ctx // tm, 0), 0)),
                  pl.BlockSpec((1, tm, 256), lambda i, j: (i, 0, 0)), tok(256), tok(256),
                  tok(256), full(8, 256), full(256, 256), full(d, 4 * d), full(1, 4 * d),
                  full(4, 256, d), full(d, d), full(1, d), full(N_EXPERTS, d), full(N_EXPERTS, 1)],
        out_specs=[tok(d), tok(d), tok(N_EXPERTS)],
        out_shape=[SDS((b, t, d), F32), SDS((b, t, d), BF16), SDS((b, t, N_EXPERTS), F32)],
        compiler_params=_cp("parallel", "parallel"), name="merge_router")(
            a, h, modv, dn_f, dn_b, dn_z, s5_y, s5_y, s5_u, da_lat, da_ctx, gl_f, gl_b, gl_r, vec, glu_w,
            w_gate, b_gate, w_branch, w_out, n2g, w_rt, b_rt)


def _moe_body(n_ctx, tt, final, f_ref, wc_ref, h_ref, mv_ref, wg_ref, wu_ref, wd_ref, fg_ref, o_ref, acc_ref):
    e = pl.program_id(2)

    @pl.when(e == 0)
    def _():
        acc_ref[...] = jnp.zeros_like(acc_ref)

    x = f_ref[0]
    wc = wc_ref[0]
    wcol = jnp.sum(jnp.where(_iota(wc.shape, 1) == e, wc, 0.0), axis=1, keepdims=True)
    g = _dot(x, wg_ref[0])
    hid = g * jax.nn.sigmoid(g) * _dot(x, wu_ref[0])
    acc_ref[...] += _dot(jnp.where(wcol != 0.0, hid * wcol, 0.0).astype(BF16), wd_ref[0])

    @pl.when(e == N_EXPERTS - 1)
    def _():
        row = pl.program_id(1) * tt + _iota((tt, 1), 0)
        mv = mv_ref[0]
        gate = jnp.where(row < n_ctx, mv[13:14], mv[5:6])
        hn = h_ref[0] + gate * acc_ref[...]
        if final:
            hn = hn * lax.rsqrt(jnp.mean(hn * hn, axis=-1, keepdims=True) + NORM_EPS) * fg_ref[...]
        o_ref[0] = hn


def _moe_call(f, wc, h, modv, weg, weu, wed, final_g, n_ctx, final):
    b, t, d = h.shape
    fe = weg.shape[2]
    tt = next(c for c in (1408, 1024, 768, 512, 256, 128, 64, 32, 16, 8) if t % c == 0)
    tok = lambda w: pl.BlockSpec((1, tt, w), lambda i, j, e: (i, j, 0))
    return pl.pallas_call(
        functools.partial(_moe_body, n_ctx, tt, final), grid=(b, t // tt, N_EXPERTS),
        in_specs=[tok(d), tok(N_EXPERTS), tok(d), pl.BlockSpec((1, 16, d), lambda i, j, e: (i, 0, 0)),
                  pl.BlockSpec((1, d, fe), lambda i, j, e: (e, 0, 0)),
                  pl.BlockSpec((1, d, fe), lambda i, j, e: (e, 0, 0)),
                  pl.BlockSpec((1, fe, d), lambda i, j, e: (e, 0, 0)),
                  pl.BlockSpec((1, d), lambda i, j, e: (0, 0))],
        out_specs=tok(d), out_shape=SDS((b, t, d), F32), scratch_shapes=[pltpu.VMEM((tt, d), F32)],
        compiler_params=_cp("parallel", "parallel", "arbitrary"), name="moe_experts")(
            f, wc, h, modv, weg, weu, wed, final_g)


def _rope_tables(n_ctx, s):
    t = jnp.arange(s)
    rowp, colp = (t // GRID_W).astype(F32), (t % GRID_W).astype(F32)
    n_freq = DA_DK // 4
    inv = ROPE_BASE ** (-jnp.arange(n_freq, dtype=F32) / n_freq)
    ang = jnp.concatenate([rowp[:, None] * inv, colp[:, None] * inv], axis=-1)
    cos, sin = jnp.cos(ang), jnp.sin(ang)
    zero = jnp.zeros_like(sin)
    cos32 = jnp.concatenate([cos, cos], axis=-1)
    sa32 = jnp.concatenate([-sin, zero], axis=-1)
    sb32 = jnp.concatenate([zero, sin], axis=-1)
    def full(tab, ctx_val):
        tab = jnp.tile(tab, (1, 8))
        return jnp.concatenate([jnp.full((n_ctx, 256), ctx_val, F32), tab], axis=0)
    return full(cos32, 1.0), full(sa32, 0.0), full(sb32, 0.0)


def _perm_w_in(w_in):
    offs, o = {}, 0
    for name, width in (("dn_qkv", 768), ("dn_z", 256), ("dn_a", 8), ("dn_b", 8), ("s5_u", 256), ("da_q", 256),
                        ("da_k", 256), ("da_v", 256), ("gla_q", 128), ("gla_k", 128), ("gla_v", 256),
                        ("gla_r", 256), ("gla_gate", 32)):
        offs[name] = (o, width)
        o += width
    col = lambda n: w_in[:, offs[n][0]:offs[n][0] + offs[n][1]]
    pad = jnp.zeros((w_in.shape[0], 128 - 48), w_in.dtype)
    return jnp.concatenate([col(n) for n in ("dn_qkv", "dn_z", "s5_u", "da_q", "da_k", "da_v", "gla_q", "gla_k",
                                              "gla_v", "gla_r", "dn_a", "dn_b", "gla_gate")] + [pad], axis=1)


def kernel(x, c, ctx, c_ctx, w_mod, b_mod, norm1_g, norm2_g, w_in, dn_conv, dn_a_log, dn_dt_bias, dn_norm_g,
           s5_a_re, s5_a_im, s5_log_dt, s5_b_re, s5_b_im, s5_c_re, s5_c_im, s5_d, s5_glu_w, s5_glu_b,
           da_lambda, da_norm_g, gla_w_gate, gla_b_gate, gla_norm_g, w_branch, w_gate, b_gate, w_out,
           w_router, b_router, w_e_gate, w_e_up, w_e_down, final_g):
    b, s, d = x.shape
    n_ctx = ctx.shape[1]
    t = n_ctx + s
    depth = w_mod.shape[0]
    assert b + 1 <= 8 and b == 4

    cvec = jnp.zeros((8, d), F32).at[:b].set(c).at[b].set(c_ctx)
    mod_all = _mod_call(cvec, w_mod, b_mod).reshape(depth, 8, 6, d)
    cos_t, sa_t, sb_t = _rope_tables(n_ctx, s)
    h = jnp.concatenate([ctx, x], axis=1)

    lane256 = jnp.arange(256) // 64
    eg = jnp.stack([(jnp.arange(128)[:, None] == (SM_DN_A + dd * 4 + lane256)[None, :]).astype(F32) for dd in range(2)])
    eb = jnp.stack([(jnp.arange(128)[:, None] == (SM_DN_B + dd * 4 + lane256)[None, :]).astype(F32) for dd in range(2)])

    for layer in range(depth):
        lam_init = 0.8 - 0.6 * math.exp(-0.3 * layer)
        ml = mod_all[layer]
        modv = jnp.zeros((b, 16, d), F32).at[:, 0:6].set(ml[:b]).at[:, 8:14].set(jnp.broadcast_to(ml[b], (b, 6, d)))

        a, dn_qkv, dn_z, s5_u, da_qkv, gla_qkv, gla_r, small = _proj_call(
            h, modv, norm1_g[layer][None, :], _perm_w_in(w_in[layer]).astype(BF16), n_ctx)

        dq, dk_, dv, gb = _dn_prep_call(dn_qkv, small, dn_conv[layer], dn_a_log[layer], dn_dt_bias[layer], n_ctx)
        dn_f, dn_b = [_dn_scan_call(dq, dk_, dv, gb, eg[dd:dd + 1], eb[dd:dd + 1], n_ctx, dd == 1)
                      for dd in range(2)]

        m_b, m_out, al = _s5_operators(
            s5_a_re[layer], s5_a_im[layer], s5_log_dt[layer], s5_b_re[layer], s5_b_im[layer],
            s5_c_re[layer], s5_c_im[layer])
        s5_y = _s5_call(s5_u, m_b, m_out, al, n_ctx)

        lf = da_lambda[layer]
        lam = jnp.exp(jnp.sum(lf[0] * lf[1])) - jnp.exp(jnp.sum(lf[2] * lf[3])) + lam_init
        lam_row = jnp.full((1, 128), lam, F32)
        da_gain = da_norm_g[layer][:, None]
        aq, ak, avt = _da_prep_call(da_qkv, cos_t, sa_t, sb_t)
        da_lat = _attn_call(aq, ak, avt, lam_row, da_gain, 1.0 - lam_init, n_ctx, s, t, 256, _da_kblock(t))
        if layer < depth - 1:
            da_ctx = _attn_call(aq, ak, avt, lam_row, da_gain, 1.0 - lam_init, 0, n_ctx, n_ctx, 256, n_ctx)
        else:
            da_ctx = da_lat

        wg2 = jnp.zeros((2, 128, 128), F32)
        for dd in range(2):
            wg2 = wg2.at[dd, SM_GLA + dd * GLA_RANK:SM_GLA + (dd + 1) * GLA_RANK].set(gla_w_gate[layer, dd])
        gl_f, gl_b = [_gla_call(gla_qkv, small, wg2[dd:dd + 1], gla_b_gate[layer][dd:dd + 1, None, :], n_ctx, dd == 1)
                      for dd in range(2)]

        vec = jnp.zeros((8, 256), F32)
        vec = vec.at[0].set(jnp.tile(dn_norm_g[layer], DN_HEADS)).at[1].set(jnp.tile(gla_norm_g[layer], GLA_HEADS))
        vec = vec.at[2].set(s5_d[layer]).at[3].set(s5_glu_b[layer])
        h, f, wc = _merge_call(
            a, h, modv, dn_f, dn_b, dn_z, s5_y, s5_u, da_lat, da_ctx, gl_f, gl_b, gla_r, vec,
            s5_glu_w[layer].astype(BF16), w_gate[layer].astype(BF16), b_gate[layer][None, :],
            w_branch[layer].astype(BF16), w_out[layer].astype(BF16), norm2_g[layer][None, :],
            w_router.T, b_router[:, None], n_ctx)

        h = _moe_call(f, wc, h, modv, w_e_gate[layer].astype(BF16), w_e_up[layer].astype(BF16),
                      w_e_down[layer].astype(BF16), final_g[None, :], n_ctx, layer == depth - 1)
    return h[:, n_ctx:]
```

```python
import functools
import math

import jax
import jax.numpy as jnp
from jax import lax
from jax.experimental import pallas as pl
from jax.experimental.pallas import tpu as pltpu

F32 = jnp.float32
BF16 = jnp.bfloat16
HI = lax.Precision.HIGHEST
SDS = jax.ShapeDtypeStruct

GRID_W = 64
DN_HEADS, DN_DK, DN_DV, DN_CONV, DN_CHUNK = 4, 64, 64, 5, 64
DN_BLOCK = 256
DN_STREAMS = 2
S5_WIDTH, S5_GROUP, S5_STATE = 256, 16, 64
S5_GROUPS = S5_WIDTH // S5_GROUP
S5_CHUNK = 16
DA_HEADS, DA_DK, DA_DV = 4, 32, 64
ROPE_BASE = 10000.0
GLA_HEADS, GLA_DK, GLA_DV, GLA_RANK, GLA_TAU, GLA_CHUNK = 4, 32, 64, 16, 16.0, 16
N_EXPERTS, N_GROUPS, TOP_K = 16, 4, 2
MOE_CAP = 256
EPG = N_EXPERTS // N_GROUPS
NORM_EPS = 1e-6
VMEM_LIMIT = 56 * 1024 * 1024

SEG = dict(dn_qkv=(0, 768), dn_z=(768, 256), s5_u=(1024, 256), da_qkv=(1280, 768),
           gla_qkv=(2048, 512), gla_r=(2560, 256), small=(2816, 128))
D_PROJ = 2944
SM_DN_A, SM_DN_B, SM_GLA = 0, 8, 16


def _cp(*sem):
    return pltpu.CompilerParams(dimension_semantics=sem, vmem_limit_bytes=VMEM_LIMIT)


def _nt(a, b, **kw):
    return lax.dot_general(a, b, (((1,), (1,)), ((), ())), preferred_element_type=F32, **kw)


def _tn(a, b, **kw):
    return lax.dot_general(a, b, (((0,), (0,)), ((), ())), preferred_element_type=F32, **kw)


def _dot(a, b, **kw):
    return jnp.dot(a, b, preferred_element_type=F32, **kw)


def _bdot(a, b):
    return jnp.dot(a.astype(BF16), b.astype(BF16), preferred_element_type=F32)


def _split3(x):
    hi = x.astype(BF16)
    r = x - hi.astype(F32)
    mid = r.astype(BF16)
    return hi, mid, (r - mid.astype(F32)).astype(BF16)


def _dot_sel(x, sel, dot=None):
    dot = dot or _dot
    sel = sel.astype(BF16)
    hi, mid, lo = _split3(x)
    return dot(hi, sel) + dot(mid, sel) + dot(lo, sel)


def _sel_dot(sel, x):
    sel = sel.astype(BF16)
    hi, mid, lo = _split3(x)
    return _dot(sel, hi) + _dot(sel, mid) + _dot(sel, lo)


def _iota(shape, dim):
    return lax.broadcasted_iota(jnp.int32, shape, dim)


def _block_ones(n, blk_shift):
    r, c = _iota((n, n), 0), _iota((n, n), 1)
    return ((r >> blk_shift) == (c >> blk_shift)).astype(F32)


def _head_rms(o, gain, eps, ones64):
    ms = _dot_sel(o * o, ones64) * (1.0 / 64.0)
    return o * lax.rsqrt(ms + eps) * gain


def _mod_body(c_ref, w_ref, b_ref, o_ref):
    c = c_ref[...]
    s = c * jax.nn.sigmoid(c)
    o_ref[0] = _dot(s, w_ref[0], precision=HI) + b_ref[0]


def _mod_call(cvec, w_mod, b_mod):
    n_layer, d, d6 = w_mod.shape
    tn = 1536
    return pl.pallas_call(
        _mod_body, grid=(n_layer, d6 // tn),
        in_specs=[pl.BlockSpec((8, d), lambda l, j: (0, 0)),
                  pl.BlockSpec((1, d, tn), lambda l, j: (l, 0, j)),
                  pl.BlockSpec((1, 1, tn), lambda l, j: (l, 0, j))],
        out_specs=pl.BlockSpec((1, 8, tn), lambda l, j: (l, 0, j)),
        out_shape=SDS((n_layer, 8, d6), F32), compiler_params=_cp("parallel", "parallel"),
        name="adaln_mod")(cvec, w_mod, b_mod.reshape(n_layer, 1, d6))


def _modulated_norm(x, gain, mv, is_ctx, k_shift, k_scale):
    y = x * lax.rsqrt(jnp.mean(x * x, axis=-1, keepdims=True) + NORM_EPS) * gain
    shift = jnp.where(is_ctx, mv[8 + k_shift:9 + k_shift], mv[k_shift:k_shift + 1])
    scale = jnp.where(is_ctx, mv[8 + k_scale:9 + k_scale], mv[k_scale:k_scale + 1])
    return y * (1.0 + scale) + shift


def _proj_body(n_ctx, tm, h_ref, mv_ref, g_ref, w_ref, a_ref, *out_refs):
    t = pl.program_id(1)
    row = t * tm + _iota((tm, 1), 0)
    a = _modulated_norm(h_ref[0], g_ref[...], mv_ref[0], row < n_ctx, 0, 1)
    ab = a.astype(BF16)
    a_ref[0] = ab
    p = _dot(ab, w_ref[...])
    for ref, (off, width) in zip(out_refs, SEG.values()):
        ref[0] = p[:, off:off + width].astype(ref.dtype)


def _proj_call(h, modv, gain, w_perm, n_ctx):
    b, t, d = h.shape
    tm = 384
    assert t % tm == 0
    outs = [SDS((b, t, d), BF16)] + [SDS((b, t, w), BF16 if name == "s5_u" else F32)
                                     for name, (_, w) in SEG.items()]
    out_specs = [pl.BlockSpec((1, tm, d), lambda i, j: (i, j, 0))] + [
        pl.BlockSpec((1, tm, w), lambda i, j: (i, j, 0)) for _, w in SEG.values()]
    return pl.pallas_call(
        functools.partial(_proj_body, n_ctx, tm), grid=(b, t // tm),
        in_specs=[pl.BlockSpec((1, tm, d), lambda i, j: (i, j, 0)),
                  pl.BlockSpec((1, 16, d), lambda i, j: (i, 0, 0)),
                  pl.BlockSpec((1, d), lambda i, j: (0, 0)),
                  pl.BlockSpec((d, D_PROJ), lambda i, j: (0, 0))],
        out_specs=out_specs, out_shape=outs, compiler_params=_cp("parallel", "parallel"),
        name="norm_proj")(h, modv, gain, w_perm)


def _dn_prep_body(nct, ntile, tm, x_ref, pv_ref, nx_ref, sm_ref, w_ref, alog_ref, dtb_ref,
                  q_ref, k_ref, v_ref, gb_ref):
    t = pl.program_id(1)
    first = jnp.logical_or(t == 0, t == nct)
    last = jnp.logical_or(t == nct - 1, t == ntile - 1)
    x = x_ref[0]
    pv = jnp.where(first, 0.0, pv_ref[0])
    nx = jnp.where(last, 0.0, nx_ref[0])
    w = w_ref[...]
    row = _iota((tm, 1), 0)
    xm1 = jnp.where(row == 0, pv[7:8], pltpu.roll(x, 1, 0))
    xm2 = jnp.where(row == 0, pv[6:7], jnp.where(row == 1, pv[7:8], pltpu.roll(x, 2, 0)))
    xp1 = jnp.where(row == tm - 1, nx[0:1], pltpu.roll(x, tm - 1, 0))
    xp2 = jnp.where(row == tm - 2, nx[0:1], jnp.where(row == tm - 1, nx[1:2], pltpu.roll(x, tm - 2, 0)))
    acc = w[0:1] * xm2 + w[1:2] * xm1 + w[2:3] * x + w[3:4] * xp1 + w[4:5] * xp2
    y = acc * jax.nn.sigmoid(acc)
    ones64 = _block_ones(256, 6)
    q, k = y[:, 0:256], y[:, 256:512]
    qs = _dot_sel(q * q, ones64)
    ks = _dot_sel(k * k, ones64)
    q_ref[0] = q * lax.rsqrt(qs + 1e-6) * (DN_DK ** -0.5)
    k_ref[0] = k * lax.rsqrt(ks + 1e-6)
    v_ref[0] = y[:, 512:768]
    sm = sm_ref[0]
    lane = _iota((1, 128), 1)
    g = -jnp.exp(alog_ref[...]) * jax.nn.softplus(sm + dtb_ref[...])
    be = jax.nn.sigmoid(sm)
    gb_ref[0] = jnp.where(lane < 8, g, jnp.where(lane < 16, be, 0.0))


def _dn_prep_call(qkv, small, conv_w, a_log, dt_bias, n_ctx):
    b, t, c = qkv.shape
    tm = 256
    assert n_ctx % tm == 0 and t % tm == 0
    nct, ntile, r8 = n_ctx // tm, t // tm, tm // 8
    w8 = jnp.zeros((8, c), F32).at[:DN_CONV].set(conv_w)
    alog = jnp.zeros((1, 128), F32).at[0, :8].set(a_log.reshape(-1))
    dtb = jnp.zeros((1, 128), F32).at[0, :8].set(dt_bias.reshape(-1))
    full = lambda shape: pl.BlockSpec(shape, lambda i, j: (0,) * len(shape))
    o256 = pl.BlockSpec((1, tm, 256), lambda i, j: (i, j, 0))
    return pl.pallas_call(
        functools.partial(_dn_prep_body, nct, ntile, tm), grid=(b, ntile),
        in_specs=[pl.BlockSpec((1, tm, c), lambda i, j: (i, j, 0)),
                  pl.BlockSpec((1, 8, c), lambda i, j: (i, jnp.maximum(j * r8 - 1, 0), 0)),
                  pl.BlockSpec((1, 8, c), lambda i, j: (i, jnp.minimum((j + 1) * r8, t // 8 - 1), 0)),
                  pl.BlockSpec((1, tm, 128), lambda i, j: (i, j, 0)),
                  full((8, c)), full((1, 128)), full((1, 128))],
        out_specs=[o256, o256, o256, pl.BlockSpec((1, tm, 128), lambda i, j: (i, j, 0))],
        out_shape=[SDS((b, t, 256), F32)] * 3 + [SDS((b, t, 128), F32)],
        compiler_params=_cp("parallel", "parallel"), name="dn_prep")(qkv, qkv, qkv, small, w8, alog, dtb)


def _scan_order(rev, step, n_first, n_total):
    back = jnp.where(step < n_first, n_first - 1 - step, n_total - 1 - (step - n_first))
    return jnp.where(rev, back, step)


def _dn_scan_body(rev, nst, nch, q_ref, k_ref, v_ref, gb_ref, eg_ref, eb_ref, o_ref, s_ref):
    c = DN_CHUNK
    tm = nst * nch * c
    chunks = range(nst * nch)
    rows = lambda x, ci: x[ci * c:(ci + 1) * c]

    @pl.when(pl.program_id(1) == 0)
    def _():
        s_ref[...] = jnp.zeros_like(s_ref)

    q, k, v, gb = [r[...].reshape(tm, r.shape[2]) for r in (q_ref, k_ref, v_ref, gb_ref)]
    gexp = _dot_sel(gb, eg_ref[0])
    bexp = _dot_sel(gb, eb_ref[0])
    r_i, c_i = _iota((c, c), 0), _iota((c, c), 1)
    tri = (c_i >= r_i) if rev else (c_i <= r_i)
    gc = jnp.concatenate([_sel_dot(tri, rows(gexp, ci)) for ci in chunks], axis=0)
    egc = jnp.exp(gc)
    kb = k * bexp
    rhs_u, rhs_w = v * bexp, kb * egc
    rr, cc = _iota((256, 256), 0), _iota((256, 256), 1)
    same_head = (rr >> 6) == (cc >> 6)
    i_idx = _iota((c, 256), 0)
    j_idx = _iota((c, 256), 1) & 63
    lane_h = _iota((c, 256), 1) >> 6
    d_ij = (j_idx - i_idx) if rev else (i_idx - j_idx)
    incl, strict = d_ij >= 0, d_ij > 0
    eye = (i_idx == j_idx).astype(F32)

    def stack4(x):
        return jnp.concatenate([jnp.where(lane_h == h, x, 0.0) for h in range(DN_HEADS)], axis=0).astype(BF16)

    def to_bd(xcat):
        return jnp.where(same_head, jnp.concatenate([xcat] * DN_HEADS, axis=0), 0.0).astype(BF16)

    both = lambda x, y: jnp.concatenate([x, y], axis=0).astype(BF16)
    gcc = [rows(gc, ci) for ci in chunks]
    gl = [g[0:1, :] if rev else g[c - 1:c, :] for g in gcc]
    gcr = [jnp.sum(jnp.where(d_ij <= 0, rows(gexp, ci), 0.0), axis=0, keepdims=True) for ci in chunks]
    decay = [jnp.where(incl, jnp.exp(jnp.where(incl, gcc[ci] - gcr[ci], 0.0)), 0.0) for ci in chunks]
    k4 = [stack4(rows(k, ci)) for ci in chunks]
    kq = [_nt(both(rows(kb, ci), rows(q, ci)), k4[ci]) for ci in chunks]
    lower = [jnp.where(strict, kq[ci][:c] * decay[ci], 0.0) for ci in chunks]
    a_intra = [jnp.where(incl, kq[ci][c:] * decay[ci], 0.0).astype(BF16) for ci in chunks]
    p = [eye - lower[ci] for ci in chunks]
    m = [_dot(lower[ci].astype(BF16), to_bd(lower[ci])) for ci in chunks]
    for it in range(5):
        mb = [to_bd(m[ci]) for ci in chunks]
        if it < 4:
            pm = [_dot(both(p[ci], m[ci]), mb[ci]) for ci in chunks]
            p = [p[ci] + pm[ci][:c] for ci in chunks]
            m = [pm[ci][c:] for ci in chunks]
        else:
            p = [p[ci] + _dot(p[ci].astype(BF16), mb[ci]) for ci in chunks]
    uw = [_dot(p[ci].astype(BF16), jnp.concatenate([stack4(rows(rhs_u, ci)), stack4(rows(rhs_w, ci))], axis=1))
          for ci in chunks]
    wq = [both(uw[ci][:, 256:], rows(q, ci) * rows(egc, ci)) for ci in chunks]
    k_dec = [(rows(k, ci) * jnp.exp(gl[ci] - gcc[ci])).astype(BF16) for ci in chunks]
    s = [s_ref[st] for st in range(nst)]
    outs = [None] * (nst * nch)
    for step in range(nch):
        for st in range(nst):
            ci = st * nch + (nch - 1 - step if rev else step)
            r = _dot(wq[ci], s[st].astype(BF16))
            v_new = uw[ci][:, :256] - r[:c]
            outs[ci] = r[c:] + _dot(a_intra[ci], stack4(v_new))
            s[st] = s[st] * jnp.exp(gl[ci]) + jnp.where(same_head, _tn(k_dec[ci], v_new.astype(BF16)), 0.0)
    for st in range(nst):
        s_ref[st] = s[st]
    o_ref[...] = jnp.concatenate(outs, axis=0).reshape(nst, nch * c, 256)


def _dn_scan_call(q, k, v, gb, eg, eb, n_ctx, rev):
    b, t, _ = q.shape
    tm = DN_BLOCK
    assert t % tm == 0 and n_ctx % tm == 0 and tm % DN_CHUNK == 0
    blk_of = lambda j: _scan_order(rev, j, n_ctx // tm, t // tm)
    nst = DN_STREAMS if b % DN_STREAMS == 0 else 1
    blk = lambda w: pl.BlockSpec((nst, tm, w), lambda i, j: (i, blk_of(j), 0))
    return pl.pallas_call(
        functools.partial(_dn_scan_body, rev, nst, tm // DN_CHUNK), grid=(b // nst, t // tm),
        in_specs=[blk(256), blk(256), blk(256), blk(128),
                  pl.BlockSpec((1, 128, 256), lambda i, j: (0, 0, 0)),
                  pl.BlockSpec((1, 128, 256), lambda i, j: (0, 0, 0))],
        out_specs=blk(256), out_shape=SDS((b, t, 256), F32),
        scratch_shapes=[pltpu.VMEM((nst, 256, 256), F32)],
        compiler_params=_cp("parallel", "arbitrary"), name="dn_scan")(q, k, v, gb, eg, eb)


def _s5_in_body(u_ref, mb_ref, o_ref):
    w = mb_ref.shape[2]
    acc = None
    for j in range(S5_CHUNK):
        part = _dot(u_ref[:, j * w:(j + 1) * w].astype(BF16), mb_ref[0, j])
        acc = part if acc is None else acc + part
    o_ref[0] = acc


def _s5_scan_body(n_sub, n_sub_ctx, bc_ref, al_ref, x_ref):
    rev = pl.program_id(0) == 1
    half = bc_ref.shape[1] // 2
    ar, ai = al_ref[0:1, :], al_ref[1:2, :]

    def sub(s, st):
        xr, xi = st
        rows = pl.ds(pl.multiple_of(_scan_order(rev, s, n_sub_ctx, n_sub) * 8, 8), 8)
        blk = bc_ref[rows, :]
        seen = []
        for kk in range(8):
            seen.append(jnp.concatenate([xr, xi], axis=1))
            inp = jnp.where(rev, blk[7 - kk:8 - kk], blk[kk:kk + 1])
            xr, xi = xr * ar - xi * ai + inp[:, :half], xi * ar + xr * ai + inp[:, half:]
        x_ref[rows, :] = jnp.concatenate([jnp.where(rev, seen[7 - r], seen[r]) for r in range(8)], axis=0)
        return xr, xi

    zero = jnp.zeros((1, half), F32)
    lax.fori_loop(0, n_sub, sub, (zero, zero))


def _s5_out_body(u_ref, x_ref, mc_ref, kbd_ref, y_ref, lhs_ref, toep_ref):
    lanes, w = x_ref.shape[2], kbd_ref.shape[2]
    rev = pl.program_id(0) == 1
    jo = pl.program_id(2)

    @pl.when(jo == 0)
    def _():
        lhs_ref[:, :lanes] = x_ref[0].astype(BF16)
        lhs_ref[:, lanes:] = u_ref[...].astype(BF16)

    for j in range(S5_CHUNK):
        feeds = jnp.where(rev, j >= jo, j <= jo)
        blk = kbd_ref[0, jnp.abs(jo - j)]
        toep_ref[j * w:(j + 1) * w, :] = jnp.where(feeds, blk, jnp.zeros_like(blk))
    y = _dot(lhs_ref[:, :lanes], mc_ref[0]) + _dot(lhs_ref[:, lanes:], toep_ref[...])
    y_ref[...] = y.astype(y_ref.dtype)


def _s5_call(u, m_b, m_c, kbd, al, n_ctx):
    b, t, w = u.shape
    L = S5_CHUNK
    n = t // L
    r = b * n
    tr = n
    lanes = m_b.shape[-1]
    u3 = u.reshape(r, L * w)
    ublk = pl.BlockSpec((tr, L * w), lambda d, i, k: (i, 0))
    nt = 512
    bc = pl.pallas_call(
        _s5_in_body, grid=(2, r // tr, lanes // nt),
        in_specs=[ublk, pl.BlockSpec((1, L, w, nt), lambda d, i, k: (d, 0, 0, k))],
        out_specs=pl.BlockSpec((1, tr, nt), lambda d, i, k: (d, i, k)),
        out_shape=SDS((2, r, lanes), F32), compiler_params=_cp("parallel", "parallel", "parallel"),
        name="s5_in")(u3, m_b)
    assert n % 8 == 0 and (n_ctx // L) % 8 == 0
    xprev = pl.pallas_call(
        functools.partial(_s5_scan_body, n // 8, n_ctx // L // 8), grid=(2, b),
        in_specs=[pl.BlockSpec((None, None, n, lanes), lambda d, i: (d, i, 0, 0)),
                  pl.BlockSpec((None, 8, lanes // 2), lambda d, i: (d, 0, 0))],
        out_specs=pl.BlockSpec((None, None, n, lanes), lambda d, i: (d, i, 0, 0)),
        out_shape=SDS((2, b, n, lanes), F32), compiler_params=_cp("parallel", "parallel"),
        name="s5_scan")(bc.reshape(2, b, n, lanes), al)
    y = pl.pallas_call(
        _s5_out_body, grid=(2, r // tr, L),
        in_specs=[ublk, pl.BlockSpec((1, tr, lanes), lambda d, i, k: (d, i, 0)),
                  pl.BlockSpec((1, lanes, w), lambda d, i, k: (d, 0, k)),
                  pl.BlockSpec((1, L, w, w), lambda d, i, k: (d, 0, 0, 0))],
        out_specs=pl.BlockSpec((None, tr, w), lambda d, i, k: (d, i, k)),
        out_shape=SDS((2, r, L * w), BF16),
        scratch_shapes=[pltpu.VMEM((tr, lanes + L * w), BF16), pltpu.VMEM((L * w, w), BF16)],
        compiler_params=_cp("parallel", "parallel", "arbitrary"), name="s5_out")(
            u3, xprev.reshape(2, r, lanes), m_c, kbd)
    return y.reshape(2, b, t, w)


def _s5_operators(a_re, a_im, log_dt, b_re, b_im, c_re, c_im):
    L = S5_CHUNK
    G, P, C = S5_GROUPS, S5_STATE, S5_GROUP
    dt = jnp.exp(log_dt)[:, :, None]
    ls = jnp.arange(L + 1, dtype=F32)
    mag = jnp.exp(a_re[..., None] * dt[..., None] * ls)
    ang = a_im[..., None] * dt[..., None] * ls
    pw_re, pw_im = mag * jnp.cos(ang), mag * jnp.sin(ang)
    ab_re, ab_im = pw_re[..., 1], pw_im[..., 1]
    den = a_re * a_re + a_im * a_im
    f_re = ((ab_re - 1.0) * a_re + ab_im * a_im) / den
    f_im = (ab_im * a_re - (ab_re - 1.0) * a_im) / den
    bb_re = f_re[..., None] * b_re[None] - f_im[..., None] * b_im[None]
    bb_im = f_re[..., None] * b_im[None] + f_im[..., None] * b_re[None]
    ca_re = c_re[..., None] * pw_re[:, :, None] - c_im[..., None] * pw_im[:, :, None]
    ca_im = c_re[..., None] * pw_im[:, :, None] + c_im[..., None] * pw_re[:, :, None]
    kern = (jnp.einsum("dgopl,dgpc->dglco", ca_re, bb_re, precision=HI)
            - jnp.einsum("dgopl,dgpc->dglco", ca_im, bb_im, precision=HI))
    GC, GP = G * C, G * P
    grp = lambda idx, per: (idx // per) % G
    row_expand = jnp.tile(jnp.eye(C, dtype=F32), (G, 1))
    gc_rows = grp(jnp.arange(GC), C)[:, None]
    kcomp = kern[:, :, :L].transpose(0, 2, 3, 1, 4).reshape(2, L, C, GC)
    kbd = jnp.where(gc_rows == grp(jnp.arange(GC), C)[None, :],
                    jnp.einsum("rc,dlcq->dlrq", row_expand, kcomp), 0.0)
    q_re, q_im = pw_re[..., :L], pw_im[..., :L]
    mb_re = q_re[..., None] * bb_re[:, :, :, None] - q_im[..., None] * bb_im[:, :, :, None]
    mb_im = q_re[..., None] * bb_im[:, :, :, None] + q_im[..., None] * bb_re[:, :, :, None]
    mb = jnp.stack([mb_re, mb_im], axis=1)
    mb = jnp.stack([mb[0][..., ::-1, :], mb[1]])
    mbcomp = mb.transpose(0, 4, 5, 1, 2, 3).reshape(2, L, C, 2 * GP)
    m_b = jnp.where(gc_rows == grp(jnp.arange(2 * GP), P)[None, :],
                    jnp.einsum("rc,djcq->djrq", row_expand, mbcomp), 0.0)
    w = jnp.stack([ca_re[..., 1:], -ca_im[..., 1:]], axis=1)
    w = jnp.stack([w[0], w[1][..., ::-1]])
    wcomp = w.transpose(0, 1, 2, 4, 5, 3).reshape(2, 2 * GP, L * C)
    col = jnp.arange(L * GC)
    col_expand = jnp.logical_and((jnp.arange(L * C) // C)[:, None] == (col // GC)[None, :],
                                 (jnp.arange(L * C) % C)[:, None] == (col % C)[None, :]).astype(F32)
    m_c = jnp.where(grp(jnp.arange(2 * GP), P)[:, None] == grp(col, C)[None, :],
                    jnp.einsum("drk,kq->drq", wcomp, col_expand), 0.0)
    al = jnp.stack([pw_re[..., L].reshape(2, GP), pw_im[..., L].reshape(2, GP)], axis=1)
    al = jnp.concatenate([al, jnp.zeros((2, 6, GP), F32)], axis=1)
    return m_b.astype(BF16), m_c.astype(BF16), kbd.astype(BF16), al


def _da_prep_body(x_ref, cos_ref, sa_ref, sb_ref, q_ref, k_ref, vt_ref):
    x = x_ref[0]
    cos, sa, sb = cos_ref[...], sa_ref[...], sb_ref[...]

    def rope(z):
        return z * cos + pltpu.roll(z, 240, 1) * sa + pltpu.roll(z, 16, 1) * sb

    q_ref[0] = (rope(x[:, 0:256]) * (DA_DK ** -0.5 * LOG2E)).astype(BF16)
    k_ref[0] = rope(x[:, 256:512]).astype(BF16)
    vt = x[:, 512:768].T.astype(BF16)
    ones = jnp.ones((DA_VROWS - DA_DV, vt.shape[1]), BF16)
    vt_ref[0, 0] = jnp.concatenate(
        [blk for h in range(DA_HEADS) for blk in (vt[h * DA_DV:(h + 1) * DA_DV], ones)], axis=0)


def _da_kblock(t):
    return next(c for c in (1408, 768, 512, 256) if t % c == 0)


DA_VROWS = DA_DV + 16
LOG2E = 1.4426950408889634


def _da_prep_call(qkv, cos, sa, sb):
    b, t, _ = qkv.shape
    tm = _da_kblock(t)
    assert t % tm == 0
    tab = pl.BlockSpec((tm, 256), lambda i, j: (j, 0))
    o = pl.BlockSpec((1, tm, 256), lambda i, j: (i, j, 0))
    return pl.pallas_call(
        _da_prep_body, grid=(b, t // tm),
        in_specs=[pl.BlockSpec((1, tm, 768), lambda i, j: (i, j, 0)), tab, tab, tab],
        out_specs=[o, o, pl.BlockSpec((1, 1, DA_HEADS * DA_VROWS, tm), lambda i, j: (i, j, 0, 0))],
        out_shape=[SDS((b, t, 256), BF16)] * 2 + [SDS((b, t // tm, DA_HEADS * DA_VROWS, tm), BF16)],
        compiler_params=_cp("parallel", "parallel"), name="da_prep")(qkv, cos, sa, sb)


def _attn_body(nkb, tk, tq, post, q_ref, k_ref, vt_ref, lam_ref, g_ref, o_ref, q8_ref, acc_ref, s0_ref, p7_ref,
               a7_ref):
    q = q_ref[0]
    lane = _iota((tq, 256), 1)
    for hm in range(8):
        q8_ref[hm] = jnp.where((lane >> 5) == hm, q, jnp.zeros_like(q))
    acc_ref[...] = jnp.zeros_like(acc_ref)

    def keys(i):
        return k_ref[0, pl.ds(pl.multiple_of(i * tk, 8), tk), :]

    s0_ref[...] = _nt(keys(0), q8_ref[0])

    nsplit = 4 if tk % 32 == 0 else 1
    p7_ref[...] = jnp.zeros_like(p7_ref)
    a7_ref[...] = jnp.ones_like(a7_ref)

    def values(i, hm):
        h = hm // 2
        return vt_ref[0, i][h * DA_VROWS:(h + 1) * DA_VROWS, :tk]

    def accumulate(hm, vals, p, alpha):
        acc_ref[hm] = acc_ref[hm] * alpha + _dot(vals, p)

    def kblock(i, carry):
        m = carry
        kb = keys(i)
        kb_next = keys(jnp.minimum(i + 1, nkb - 1))
        m_rows = []
        s = s0_ref[...]
        pending = (7, values(jnp.maximum(i - 1, 0), 7), p7_ref[...], a7_ref[0:1, :])
        for hm in range(8):
            s_next = _nt(kb, q8_ref[hm + 1]) if hm < 7 else _nt(kb_next, q8_ref[0])
            accumulate(*pending)
            m_old = m[hm:hm + 1, :]
            part = s[0:tk // nsplit]
            for r in range(1, nsplit):
                part = jnp.maximum(part, s[r * tk // nsplit:(r + 1) * tk // nsplit])
            m_new = jnp.maximum(m_old, jnp.max(part, axis=0, keepdims=True))
            alpha = jnp.exp2(m_old - m_new)
            p = jnp.exp2(s - m_new).astype(BF16)
            m_rows.append(m_new)
            pending = (hm, values(i, hm), p, alpha)
            s = s_next
        s0_ref[...] = s
        p7_ref[...] = pending[2]
        a7_ref[0:1, :] = pending[3]
        return jnp.concatenate(m_rows, axis=0)

    lax.fori_loop(0, nkb, kblock, jnp.full((8, tq), -1e30, F32))
    accumulate(7, values(nkb - 1, 7), p7_ref[...], a7_ref[0:1, :])
    lam = lam_ref[0:1, 0:1]
    heads = []
    for h in range(DA_HEADS):
        a1, a2 = acc_ref[2 * h], acc_ref[2 * h + 1]
        o = a1[0:DA_DV] / a1[DA_DV:DA_DV + 1] - lam * (a2[0:DA_DV] / a2[DA_DV:DA_DV + 1])
        ms = jnp.mean(o * o, axis=0, keepdims=True)
        heads.append(o * lax.rsqrt(ms + 1e-5) * g_ref[...] * post)
    o_ref[0] = jnp.concatenate(heads, axis=0).T


def _attn_call(q, k, vt, lam_row, gain_col, post, q_off, n_q, n_keys, tq, tk):
    b = q.shape[0]
    assert n_keys % tk == 0 and n_q % tq == 0 and q_off % tq == 0
    nkb = n_keys // tk
    assert tk == vt.shape[3] or nkb == 1
    qo = q_off // tq
    return pl.pallas_call(
        functools.partial(_attn_body, nkb, tk, tq, post), grid=(b, n_q // tq),
        in_specs=[pl.BlockSpec((1, tq, 256), lambda i, j: (i, j + qo, 0)),
                  pl.BlockSpec((1, n_keys, 256), lambda i, j: (i, 0, 0)),
                  pl.BlockSpec((1, nkb, vt.shape[2], vt.shape[3]), lambda i, j: (i, 0, 0, 0)),
                  pl.BlockSpec((1, 128), lambda i, j: (0, 0)),
                  pl.BlockSpec((DA_DV, 1), lambda i, j: (0, 0))],
        out_specs=pl.BlockSpec((1, tq, 256), lambda i, j: (i, j, 0)),
        out_shape=SDS((b, n_q, 256), F32),
        scratch_shapes=[pltpu.VMEM((8, tq, 256), BF16), pltpu.VMEM((8, DA_VROWS, tq), F32),
                        pltpu.VMEM((tk, tq), F32), pltpu.VMEM((tk, tq), BF16), pltpu.VMEM((8, tq), F32)],
        compiler_params=_cp("parallel", "parallel"), name="diff_attn")(q, k, vt, lam_row, gain_col)


def _gla_body(rev, n, qkv_ref, sm_ref, wg_ref, bg_ref, o_ref, st_ref):
    c = GLA_CHUNK
    tm = n * c

    @pl.when(pl.program_id(1) == 0)
    def _():
        st_ref[...] = jnp.zeros_like(st_ref)

    x = qkv_ref[0]
    q, k, v = x[:, 0:128] * (GLA_DK ** -0.5), x[:, 128:256], x[:, 256:512]
    z = _dot(sm_ref[0], wg_ref[0], precision=HI) + bg_ref[0]
    la = jax.nn.log_sigmoid(z) * (1.0 / GLA_TAU)
    r_i, c_i = _iota((tm, tm), 0), _iota((tm, tm), 1)
    before = (c_i >= r_i) if rev else (c_i <= r_i)
    tri = jnp.logical_and((r_i >> 4) == (c_i >> 4), before)
    b = _sel_dot(tri, la)
    chunk_sel = (_iota((tm, 128), 0) >> 4) == _iota((tm, 128), 1)
    a_cols = jnp.exp(_dot_sel(la, chunk_sel, dot=_tn))
    b3, q3, k3 = b.reshape(n, c, 128), q.reshape(n, c, 128), k.reshape(n, c, 128)
    v3 = v.reshape(n, c, 256)
    bl3 = b3[:, 0:1, :] if rev else b3[:, c - 1:c, :]
    row3 = _iota((n, c, 128), 1)
    head_ones = ((_iota((128, 256), 0) >> 5) == (_iota((128, 256), 1) >> 6)).astype(BF16)
    o3 = jnp.zeros((n, c, 256), F32)
    for j in range(c):
        ok = (row3 <= j) if rev else (row3 >= j)
        e = jnp.where(ok, jnp.exp(jnp.where(ok, b3 - b3[:, j:j + 1, :], 0.0)), 0.0)
        part = (e * q3 * k3[:, j:j + 1, :]).reshape(tm, 128).astype(BF16)
        o3 = o3 + _dot(part, head_ones).reshape(n, c, 256) * v3[:, j:j + 1, :]
    q_dec = (q3 * jnp.exp(b3)).astype(BF16)
    k_dec = (k3 * jnp.exp(bl3 - b3)).astype(BF16)
    vb = v3.astype(BF16)
    st_mask = (_iota((128, 256), 0) >> 5) == (_iota((128, 256), 1) >> 6)
    order = range(n - 1, -1, -1) if rev else range(n)
    upd = {ci: jnp.where(st_mask, _tn(k_dec[ci], vb[ci]), 0.0) for ci in order}
    st = st_ref[...]
    inter = [None] * n
    for ci in order:
        inter[ci] = _dot(q_dec[ci], st.astype(BF16))
        st = st * a_cols[:, ci:ci + 1] + upd[ci]
    st_ref[...] = st
    o_ref[0] = jnp.concatenate(inter, axis=0) + o3.reshape(tm, 256)


def _gla_call(qkv, small, wg, bg, n_ctx, rev):
    b, t, _ = qkv.shape
    tm = 256
    assert t % tm == 0 and n_ctx % tm == 0 and tm // GLA_CHUNK <= 128 and GLA_CHUNK == 16
    blk = lambda i, j: _scan_order(rev, j, n_ctx // tm, t // tm)
    return pl.pallas_call(
        functools.partial(_gla_body, rev, tm // GLA_CHUNK), grid=(b, t // tm),
        in_specs=[pl.BlockSpec((1, tm, 512), lambda i, j: (i, blk(i, j), 0)),
                  pl.BlockSpec((1, tm, 128), lambda i, j: (i, blk(i, j), 0)),
                  pl.BlockSpec((1, 128, 128), lambda i, j: (0, 0, 0)),
                  pl.BlockSpec((1, 1, 128), lambda i, j: (0, 0, 0))],
        out_specs=pl.BlockSpec((1, tm, 256), lambda i, j: (i, blk(i, j), 0)),
        out_shape=SDS((b, t, 256), F32), scratch_shapes=[pltpu.VMEM((128, 256), F32)],
        compiler_params=_cp("parallel", "arbitrary"), name="gla_scan")(qkv, small, wg, bg)


def _merge_body(n_ctx, tm, a_ref, h_ref, mv_ref, dnf_ref, dnr_ref, dnz_ref, s5y_ref, s5yr_ref, s5u_ref,
                da_ref, dac_ref, glf_ref, glr_ref, glz_ref, vec_ref, gluw_ref, wgate_ref, bgate_ref, wbr_ref,
                wout_ref, n2g_ref, wr_ref, br_ref, h_out, f_out, wt_out):
    t = pl.program_id(1)
    d = h_ref.shape[2]
    row = t * tm + _iota((tm, 1), 0)
    is_ctx = row < n_ctx
    mv = mv_ref[0]
    vec = vec_ref[...]
    ones64 = _block_ones(256, 6)
    z = dnz_ref[0]
    ya = _head_rms(dnf_ref[0] + dnr_ref[0], vec[0:1], NORM_EPS, ones64) * (z * jax.nn.sigmoid(z))
    u = s5u_ref[0].astype(F32)
    zz = jax.nn.gelu(s5y_ref[0].astype(F32) + s5yr_ref[0].astype(F32) + vec[2:3] * u)
    yb = zz * jax.nn.sigmoid(_bdot(zz, gluw_ref[...]) + vec[3:4])
    r = glz_ref[0]
    yd = _head_rms(glf_ref[0] + glr_ref[0], vec[1:2], NORM_EPS, ones64) * (r * jax.nn.sigmoid(r))
    a = a_ref[0]
    acc = jnp.zeros((tm, d), F32)
    yc = jnp.where(t * tm < n_ctx, dac_ref[0], da_ref[0])
    for i, y in enumerate((ya, yb, yc, yd)):
        gate = jax.nn.sigmoid(_dot(a, wgate_ref[:, i * d:(i + 1) * d]) + bgate_ref[:, i * d:(i + 1) * d])
        acc = acc + gate * _dot(y.astype(BF16), wbr_ref[i])
    out = _dot(acc.astype(BF16), wout_ref[...])
    gate2 = jnp.where(is_ctx, mv[10:11], mv[2:3])
    h_new = h_ref[0] + gate2 * out
    h_out[0] = h_new
    f = _modulated_norm(h_new, n2g_ref[...], mv, is_ctx, 3, 4)
    f_out[0] = f.astype(BF16)
    f_hi = f.astype(BF16)
    f_lo = (f - f_hi.astype(F32)).astype(BF16)
    wr = wr_ref[...]
    wr_hi = wr.astype(BF16)
    wr_lo = (wr - wr_hi.astype(F32)).astype(BF16)
    lg = _nt(jnp.concatenate([wr_hi, wr_lo], axis=0), f_hi)
    sc = jax.nn.sigmoid(lg[:N_EXPERTS] + lg[N_EXPERTS:] + _nt(wr_hi, f_lo))
    gr = sc + br_ref[...]
    rows_ = [gr[e:e + 1, :] for e in range(N_EXPERTS)]
    best_v, best_g = None, None
    for g in range(N_GROUPS):
        x = rows_[g * EPG:(g + 1) * EPG]
        top2 = None
        for i in range(EPG):
            for j in range(i + 1, EPG):
                s_ij = x[i] + x[j]
                top2 = s_ij if top2 is None else jnp.maximum(top2, s_ij)
        if best_v is None:
            best_v, best_g = top2, jnp.zeros_like(top2, dtype=jnp.int32)
        else:
            take = top2 > best_v
            best_v = jnp.where(take, top2, best_v)
            best_g = jnp.where(take, g, best_g)
    cand = []
    for kk in range(EPG):
        ck = rows_[kk]
        for g in range(1, N_GROUPS):
            ck = jnp.where(best_g == g, rows_[g * EPG + kk], ck)
        cand.append(ck)
    v1, i1 = cand[0], jnp.zeros_like(best_g)
    for kk in range(1, EPG):
        take = cand[kk] > v1
        v1 = jnp.where(take, cand[kk], v1)
        i1 = jnp.where(take, kk, i1)
    v2, i2 = jnp.full_like(v1, -jnp.inf), jnp.zeros_like(best_g)
    for kk in range(EPG):
        take = jnp.logical_and(i1 != kk, cand[kk] > v2)
        v2 = jnp.where(take, cand[kk], v2)
        i2 = jnp.where(take, kk, i2)
    e1, e2 = best_g * EPG + i1, best_g * EPG + i2
    e_iota = _iota((N_EXPERTS, tm), 0)
    s1 = jnp.sum(jnp.where(e_iota == e1, sc, 0.0), axis=0, keepdims=True)
    s2 = jnp.sum(jnp.where(e_iota == e2, sc, 0.0), axis=0, keepdims=True)
    tot = s1 + s2
    wt_out[0] = jnp.where(e_iota == e1, s1 / tot, jnp.where(e_iota == e2, s2 / tot, 0.0))


def _merge_call(a, h, modv, dn_f, dn_b, dn_z, s5_y, s5_u, da_lat, da_ctx, gl_f, gl_b, gl_r, vec, glu_w,
                w_gate, b_gate, w_branch, w_out, n2g, w_rt, b_rt, n_ctx):
    b, t, d = h.shape
    tm = 256
    assert t % tm == 0
    tok = lambda w: pl.BlockSpec((1, tm, w), lambda i, j: (i, j, 0))
    fwd = pl.BlockSpec((None, 1, tm, 256), lambda i, j: (0, i, j, 0))
    bwd = pl.BlockSpec((None, 1, tm, 256), lambda i, j: (1, i, j, 0))
    full = lambda *shape: pl.BlockSpec(shape, lambda i, j: (0,) * len(shape))
    return pl.pallas_call(
        functools.partial(_merge_body, n_ctx, tm), grid=(b, t // tm),
        in_specs=[tok(d), tok(d), pl.BlockSpec((1, 16, d), lambda i, j: (i, 0, 0)),
                  tok(256), tok(256), tok(256), fwd, bwd, tok(256),
                  pl.BlockSpec((1, tm, 256), lambda i, j: (i, jnp.maximum(j - n_ctx // tm, 0), 0)),
                  pl.BlockSpec((1, tm, 256), lambda i, j: (i, 0, 0)), tok(256), tok(256),
                  tok(256), full(8, 256), full(256, 256), full(d, 4 * d), full(1, 4 * d),
                  full(4, 256, d), full(d, d), full(1, d), full(N_EXPERTS, d), full(N_EXPERTS, 1)],
        out_specs=[tok(d), tok(d), pl.BlockSpec((1, N_EXPERTS, tm), lambda i, j: (i, 0, j))],
        out_shape=[SDS((b, t, d), F32), SDS((b, t, d), BF16), SDS((b, N_EXPERTS, t), F32)],
        compiler_params=_cp("parallel", "parallel"), name="merge_router")(
            a, h, modv, dn_f, dn_b, dn_z, s5_y, s5_y, s5_u, da_lat, da_ctx, gl_f, gl_b, gl_r, vec, glu_w,
            w_gate, b_gate, w_branch, w_out, n2g, w_rt, b_rt)


def _moe_body(n_ctx, tt, final, f_ref, wt_ref, h_ref, mv_ref, wg_ref, wu_ref, wd_ref, fg_ref, o_ref,
              acc_ref, rank_t_ref, rank_c_ref):
    e = pl.program_id(2)
    cap = MOE_CAP

    @pl.when(e == 0)
    def _():
        acc_ref[...] = jnp.zeros_like(acc_ref)
        routed = wt_ref[0] != 0.0
        earlier = (_iota((tt, tt), 0) < _iota((tt, tt), 1)).astype(BF16)
        rank = _dot(routed.astype(BF16), earlier)
        rank = jnp.where(routed, rank, -1.0)
        rank_t_ref[...] = rank
        eye = _iota((N_EXPERTS, N_EXPERTS), 0) == _iota((N_EXPERTS, N_EXPERTS), 1)
        rank_c_ref[...] = _dot_sel(rank, eye, dot=_tn)

    x = f_ref[0]
    rank_row = rank_t_ref[pl.ds(e, 1), :]
    w_row = wt_ref[0, pl.ds(e, 1), :]
    rc = rank_c_ref[...]
    rank_col = jnp.sum(jnp.where(_iota(rc.shape, 1) == e, rc, 0.0), axis=1, keepdims=True)
    count = jnp.sum((rank_row >= 0.0).astype(F32))

    for blk in range(-(-tt // cap)):
        @pl.when(count > blk * cap)
        def _(blk=blk):
            sel = rank_row == (_iota((cap, tt), 0) + blk * cap).astype(F32)
            xg = _dot(sel.astype(BF16), x).astype(BF16)
            w_g = jnp.sum(jnp.where(sel, w_row, 0.0), axis=1, keepdims=True)
            g = _dot(xg, wg_ref[0])
            hid = g * jax.nn.sigmoid(g) * _dot(xg, wu_ref[0]) * w_g
            y = _dot(hid.astype(BF16), wd_ref[0]).astype(BF16)
            back = rank_col == (_iota((tt, cap), 1) + blk * cap).astype(F32)
            acc_ref[...] += _dot(back.astype(BF16), y)

    @pl.when(e == N_EXPERTS - 1)
    def _():
        row = pl.program_id(1) * tt + _iota((tt, 1), 0)
        mv = mv_ref[0]
        gate = jnp.where(row < n_ctx, mv[13:14], mv[5:6])
        hn = h_ref[0] + gate * acc_ref[...]
        if final:
            hn = hn * lax.rsqrt(jnp.mean(hn * hn, axis=-1, keepdims=True) + NORM_EPS) * fg_ref[...]
        o_ref[0] = hn


def _moe_call(f, wc, h, modv, weg, weu, wed, final_g, n_ctx, final):
    b, t, d = h.shape
    fe = weg.shape[2]
    tt = next(c for c in (1408, 1024, 768, 512, 256, 128, 64, 32, 16, 8) if t % c == 0)
    tok = lambda w: pl.BlockSpec((1, tt, w), lambda i, j, e: (i, j, 0))
    return pl.pallas_call(
        functools.partial(_moe_body, n_ctx, tt, final), grid=(b, t // tt, N_EXPERTS),
        in_specs=[tok(d), pl.BlockSpec((1, N_EXPERTS, tt), lambda i, j, e: (i, 0, j)), tok(d),
                  pl.BlockSpec((1, 16, d), lambda i, j, e: (i, 0, 0)),
                  pl.BlockSpec((1, d, fe), lambda i, j, e: (e, 0, 0)),
                  pl.BlockSpec((1, d, fe), lambda i, j, e: (e, 0, 0)),
                  pl.BlockSpec((1, fe, d), lambda i, j, e: (e, 0, 0)),
                  pl.BlockSpec((1, d), lambda i, j, e: (0, 0))],
        out_specs=tok(d), out_shape=SDS((b, t, d), F32),
        scratch_shapes=[pltpu.VMEM((tt, d), F32), pltpu.VMEM((N_EXPERTS, tt), F32), pltpu.VMEM((tt, N_EXPERTS), F32)],
        compiler_params=_cp("parallel", "parallel", "arbitrary"), name="moe_experts")(
            f, wc, h, modv, weg, weu, wed, final_g)


def _rope_tables(n_ctx, s):
    t = jnp.arange(s)
    rowp, colp = (t // GRID_W).astype(F32), (t % GRID_W).astype(F32)
    n_freq = DA_DK // 4
    inv = ROPE_BASE ** (-jnp.arange(n_freq, dtype=F32) / n_freq)
    ang = jnp.concatenate([rowp[:, None] * inv, colp[:, None] * inv], axis=-1)
    cos, sin = jnp.cos(ang), jnp.sin(ang)
    zero = jnp.zeros_like(sin)
    cos32 = jnp.concatenate([cos, cos], axis=-1)
    sa32 = jnp.concatenate([-sin, zero], axis=-1)
    sb32 = jnp.concatenate([zero, sin], axis=-1)
    def full(tab, ctx_val):
        tab = jnp.tile(tab, (1, 8))
        return jnp.concatenate([jnp.full((n_ctx, 256), ctx_val, F32), tab], axis=0)
    return full(cos32, 1.0), full(sa32, 0.0), full(sb32, 0.0)


def _perm_w_in(w_in):
    offs, o = {}, 0
    for name, width in (("dn_qkv", 768), ("dn_z", 256), ("dn_a", 8), ("dn_b", 8), ("s5_u", 256), ("da_q", 256),
                        ("da_k", 256), ("da_v", 256), ("gla_q", 128), ("gla_k", 128), ("gla_v", 256),
                        ("gla_r", 256), ("gla_gate", 32)):
        offs[name] = (o, width)
        o += width
    col = lambda n: w_in[:, offs[n][0]:offs[n][0] + offs[n][1]]
    pad = jnp.zeros((w_in.shape[0], 128 - 48), w_in.dtype)
    return jnp.concatenate([col(n) for n in ("dn_qkv", "dn_z", "s5_u", "da_q", "da_k", "da_v", "gla_q", "gla_k",
                                              "gla_v", "gla_r", "dn_a", "dn_b", "gla_gate")] + [pad], axis=1)


def kernel(x, c, ctx, c_ctx, w_mod, b_mod, norm1_g, norm2_g, w_in, dn_conv, dn_a_log, dn_dt_bias, dn_norm_g,
           s5_a_re, s5_a_im, s5_log_dt, s5_b_re, s5_b_im, s5_c_re, s5_c_im, s5_d, s5_glu_w, s5_glu_b,
           da_lambda, da_norm_g, gla_w_gate, gla_b_gate, gla_norm_g, w_branch, w_gate, b_gate, w_out,
           w_router, b_router, w_e_gate, w_e_up, w_e_down, final_g):
    b, s, d = x.shape
    n_ctx = ctx.shape[1]
    t = n_ctx + s
    depth = w_mod.shape[0]
    assert b + 1 <= 8 and b == 4

    cvec = jnp.zeros((8, d), F32).at[:b].set(c).at[b].set(c_ctx)
    mod_all = _mod_call(cvec, w_mod, b_mod).reshape(depth, 8, 6, d)
    cos_t, sa_t, sb_t = _rope_tables(n_ctx, s)
    h = jnp.concatenate([ctx, x], axis=1)

    lane256 = jnp.arange(256) // 64
    eg = jnp.stack([(jnp.arange(128)[:, None] == (SM_DN_A + dd * 4 + lane256)[None, :]).astype(F32) for dd in range(2)])
    eb = jnp.stack([(jnp.arange(128)[:, None] == (SM_DN_B + dd * 4 + lane256)[None, :]).astype(F32) for dd in range(2)])

    for layer in range(depth):
        lam_init = 0.8 - 0.6 * math.exp(-0.3 * layer)
        ml = mod_all[layer]
        modv = jnp.zeros((b, 16, d), F32).at[:, 0:6].set(ml[:b]).at[:, 8:14].set(jnp.broadcast_to(ml[b], (b, 6, d)))

        a, dn_qkv, dn_z, s5_u, da_qkv, gla_qkv, gla_r, small = _proj_call(
            h, modv, norm1_g[layer][None, :], _perm_w_in(w_in[layer]).astype(BF16), n_ctx)

        dq, dk_, dv, gb = _dn_prep_call(dn_qkv, small, dn_conv[layer], dn_a_log[layer], dn_dt_bias[layer], n_ctx)
        dn_f, dn_b = [_dn_scan_call(dq, dk_, dv, gb, eg[dd:dd + 1], eb[dd:dd + 1], n_ctx, dd == 1)
                      for dd in range(2)]

        m_b, m_c, kbd, al = _s5_operators(
            s5_a_re[layer], s5_a_im[layer], s5_log_dt[layer], s5_b_re[layer], s5_b_im[layer],
            s5_c_re[layer], s5_c_im[layer])
        s5_y = _s5_call(s5_u, m_b, m_c, kbd, al, n_ctx)

        lf = da_lambda[layer]
        lam = jnp.exp(jnp.sum(lf[0] * lf[1])) - jnp.exp(jnp.sum(lf[2] * lf[3])) + lam_init
        lam_row = jnp.full((1, 128), lam, F32)
        da_gain = da_norm_g[layer][:, None]
        aq, ak, avt = _da_prep_call(da_qkv, cos_t, sa_t, sb_t)
        da_lat = _attn_call(aq, ak, avt, lam_row, da_gain, 1.0 - lam_init, n_ctx, s, t, 256, _da_kblock(t))
        if layer < depth - 1:
            da_ctx = _attn_call(aq, ak, avt, lam_row, da_gain, 1.0 - lam_init, 0, n_ctx, n_ctx, 256, n_ctx)
        else:
            da_ctx = da_lat

        wg2 = jnp.zeros((2, 128, 128), F32)
        for dd in range(2):
            wg2 = wg2.at[dd, SM_GLA + dd * GLA_RANK:SM_GLA + (dd + 1) * GLA_RANK].set(gla_w_gate[layer, dd])
        gl_f, gl_b = [_gla_call(gla_qkv, small, wg2[dd:dd + 1], gla_b_gate[layer][dd:dd + 1, None, :], n_ctx, dd == 1)
                      for dd in range(2)]

        vec = jnp.zeros((8, 256), F32)
        vec = vec.at[0].set(jnp.tile(dn_norm_g[layer], DN_HEADS)).at[1].set(jnp.tile(gla_norm_g[layer], GLA_HEADS))
        vec = vec.at[2].set(s5_d[layer]).at[3].set(s5_glu_b[layer])
        h, f, wc = _merge_call(
            a, h, modv, dn_f, dn_b, dn_z, s5_y, s5_u, da_lat, da_ctx, gl_f, gl_b, gla_r, vec,
            s5_glu_w[layer].astype(BF16), w_gate[layer].astype(BF16), b_gate[layer][None, :],
            w_branch[layer].astype(BF16), w_out[layer].astype(BF16), norm2_g[layer][None, :],
            w_router.T, b_router[:, None], n_ctx)

        h = _moe_call(f, wc, h, modv, w_e_gate[layer].astype(BF16), w_e_up[layer].astype(BF16),
                      w_e_down[layer].astype(BF16), final_g[None, :], n_ctx, layer == depth - 1)
    return h[:, n_ctx:]
```

```python
import functools
import math

import jax
import jax.numpy as jnp
from jax import lax
from jax.experimental import pallas as pl
from jax.experimental.pallas import tpu as pltpu

F32 = jnp.float32
BF16 = jnp.bfloat16
HI = lax.Precision.HIGHEST
SDS = jax.ShapeDtypeStruct

GRID_W = 64
DN_HEADS, DN_DK, DN_DV, DN_CONV, DN_CHUNK = 4, 64, 64, 5, 64
DN_BLOCK = 256
DN_STREAMS = 4
S5_WIDTH, S5_GROUP, S5_STATE = 256, 16, 64
S5_GROUPS = S5_WIDTH // S5_GROUP
S5_CHUNK = 16
DA_HEADS, DA_DK, DA_DV = 4, 32, 64
ROPE_BASE = 10000.0
GLA_HEADS, GLA_DK, GLA_DV, GLA_RANK, GLA_TAU, GLA_CHUNK = 4, 32, 64, 16, 16.0, 16
N_EXPERTS, N_GROUPS, TOP_K = 16, 4, 2
MOE_CAP = 208
EPG = N_EXPERTS // N_GROUPS
NORM_EPS = 1e-6
VMEM_LIMIT = 56 * 1024 * 1024

SEG = dict(dn_qkv=(0, 768), dn_z=(768, 256), s5_u=(1024, 256), da_qkv=(1280, 768),
           gla_qkv=(2048, 512), gla_r=(2560, 256), small=(2816, 128))
D_PROJ = 2944
SM_DN_A, SM_DN_B, SM_GLA = 0, 8, 16


def _cp(*sem):
    return pltpu.CompilerParams(dimension_semantics=sem, vmem_limit_bytes=VMEM_LIMIT)


def _nt(a, b, **kw):
    return lax.dot_general(a, b, (((1,), (1,)), ((), ())), preferred_element_type=F32, **kw)


def _tn(a, b, **kw):
    return lax.dot_general(a, b, (((0,), (0,)), ((), ())), preferred_element_type=F32, **kw)


def _dot(a, b, **kw):
    return jnp.dot(a, b, preferred_element_type=F32, **kw)


def _bdot(a, b):
    return jnp.dot(a.astype(BF16), b.astype(BF16), preferred_element_type=F32)


def _split3(x):
    hi = x.astype(BF16)
    r = x - hi.astype(F32)
    mid = r.astype(BF16)
    return hi, mid, (r - mid.astype(F32)).astype(BF16)


def _dot_sel(x, sel, dot=None):
    dot = dot or _dot
    sel = sel.astype(BF16)
    hi, mid, lo = _split3(x)
    return dot(hi, sel) + dot(mid, sel) + dot(lo, sel)


def _sel_dot(sel, x):
    sel = sel.astype(BF16)
    hi, mid, lo = _split3(x)
    return _dot(sel, hi) + _dot(sel, mid) + _dot(sel, lo)


def _iota(shape, dim):
    return lax.broadcasted_iota(jnp.int32, shape, dim)


def _block_ones(n, blk_shift):
    r, c = _iota((n, n), 0), _iota((n, n), 1)
    return ((r >> blk_shift) == (c >> blk_shift)).astype(F32)


def _head_rms(o, gain, eps, ones64):
    ms = _dot_sel(o * o, ones64) * (1.0 / 64.0)
    return o * lax.rsqrt(ms + eps) * gain


def _mod_body(c_ref, w_ref, b_ref, o_ref):
    c = c_ref[...]
    s = c * jax.nn.sigmoid(c)
    o_ref[0] = _dot(s, w_ref[0], precision=HI) + b_ref[0]


def _mod_call(cvec, w_mod, b_mod):
    n_layer, d, d6 = w_mod.shape
    tn = 1536
    return pl.pallas_call(
        _mod_body, grid=(n_layer, d6 // tn),
        in_specs=[pl.BlockSpec((8, d), lambda l, j: (0, 0)),
                  pl.BlockSpec((1, d, tn), lambda l, j: (l, 0, j)),
                  pl.BlockSpec((1, 1, tn), lambda l, j: (l, 0, j))],
        out_specs=pl.BlockSpec((1, 8, tn), lambda l, j: (l, 0, j)),
        out_shape=SDS((n_layer, 8, d6), F32), compiler_params=_cp("parallel", "parallel"),
        name="adaln_mod")(cvec, w_mod, b_mod.reshape(n_layer, 1, d6))


def _modulated_norm(x, gain, mv, is_ctx, k_shift, k_scale):
    y = x * lax.rsqrt(jnp.mean(x * x, axis=-1, keepdims=True) + NORM_EPS) * gain
    shift = jnp.where(is_ctx, mv[8 + k_shift:9 + k_shift], mv[k_shift:k_shift + 1])
    scale = jnp.where(is_ctx, mv[8 + k_scale:9 + k_scale], mv[k_scale:k_scale + 1])
    return y * (1.0 + scale) + shift


def _proj_body(n_ctx, tm, h_ref, mv_ref, g_ref, w_ref, a_ref, *out_refs):
    t = pl.program_id(1)
    row = t * tm + _iota((tm, 1), 0)
    a = _modulated_norm(h_ref[0], g_ref[...], mv_ref[0], row < n_ctx, 0, 1)
    ab = a.astype(BF16)
    a_ref[0] = ab
    p = _dot(ab, w_ref[...])
    for ref, (off, width) in zip(out_refs, SEG.values()):
        ref[0] = p[:, off:off + width].astype(ref.dtype)


def _proj_call(h, modv, gain, w_perm, n_ctx):
    b, t, d = h.shape
    tm = 384
    assert t % tm == 0
    outs = [SDS((b, t, d), BF16)] + [SDS((b, t, w), BF16 if name == "s5_u" else F32)
                                     for name, (_, w) in SEG.items()]
    out_specs = [pl.BlockSpec((1, tm, d), lambda i, j: (i, j, 0))] + [
        pl.BlockSpec((1, tm, w), lambda i, j: (i, j, 0)) for _, w in SEG.values()]
    return pl.pallas_call(
        functools.partial(_proj_body, n_ctx, tm), grid=(b, t // tm),
        in_specs=[pl.BlockSpec((1, tm, d), lambda i, j: (i, j, 0)),
                  pl.BlockSpec((1, 16, d), lambda i, j: (i, 0, 0)),
                  pl.BlockSpec((1, d), lambda i, j: (0, 0)),
                  pl.BlockSpec((d, D_PROJ), lambda i, j: (0, 0))],
        out_specs=out_specs, out_shape=outs, compiler_params=_cp("parallel", "parallel"),
        name="norm_proj")(h, modv, gain, w_perm)


def _dn_prep_body(nct, ntile, tm, x_ref, pv_ref, nx_ref, sm_ref, w_ref, alog_ref, dtb_ref,
                  q_ref, k_ref, v_ref, gb_ref):
    t = pl.program_id(1)
    first = jnp.logical_or(t == 0, t == nct)
    last = jnp.logical_or(t == nct - 1, t == ntile - 1)
    x = x_ref[0]
    pv = jnp.where(first, 0.0, pv_ref[0])
    nx = jnp.where(last, 0.0, nx_ref[0])
    w = w_ref[...]
    row = _iota((tm, 1), 0)
    xm1 = jnp.where(row == 0, pv[7:8], pltpu.roll(x, 1, 0))
    xm2 = jnp.where(row == 0, pv[6:7], jnp.where(row == 1, pv[7:8], pltpu.roll(x, 2, 0)))
    xp1 = jnp.where(row == tm - 1, nx[0:1], pltpu.roll(x, tm - 1, 0))
    xp2 = jnp.where(row == tm - 2, nx[0:1], jnp.where(row == tm - 1, nx[1:2], pltpu.roll(x, tm - 2, 0)))
    acc = w[0:1] * xm2 + w[1:2] * xm1 + w[2:3] * x + w[3:4] * xp1 + w[4:5] * xp2
    y = acc * jax.nn.sigmoid(acc)
    ones64 = _block_ones(256, 6)
    q, k = y[:, 0:256], y[:, 256:512]
    qs = _dot_sel(q * q, ones64)
    ks = _dot_sel(k * k, ones64)
    q_ref[0] = q * lax.rsqrt(qs + 1e-6) * (DN_DK ** -0.5)
    k_ref[0] = k * lax.rsqrt(ks + 1e-6)
    v_ref[0] = y[:, 512:768]
    sm = sm_ref[0]
    lane = _iota((1, 128), 1)
    g = -jnp.exp(alog_ref[...]) * jax.nn.softplus(sm + dtb_ref[...])
    be = jax.nn.sigmoid(sm)
    gb_ref[0] = jnp.where(lane < 8, g, jnp.where(lane < 16, be, 0.0))


def _dn_prep_call(qkv, small, conv_w, a_log, dt_bias, n_ctx):
    b, t, c = qkv.shape
    tm = 256
    assert n_ctx % tm == 0 and t % tm == 0
    nct, ntile, r8 = n_ctx // tm, t // tm, tm // 8
    w8 = jnp.zeros((8, c), F32).at[:DN_CONV].set(conv_w)
    alog = jnp.zeros((1, 128), F32).at[0, :8].set(a_log.reshape(-1))
    dtb = jnp.zeros((1, 128), F32).at[0, :8].set(dt_bias.reshape(-1))
    full = lambda shape: pl.BlockSpec(shape, lambda i, j: (0,) * len(shape))
    o256 = pl.BlockSpec((1, tm, 256), lambda i, j: (i, j, 0))
    return pl.pallas_call(
        functools.partial(_dn_prep_body, nct, ntile, tm), grid=(b, ntile),
        in_specs=[pl.BlockSpec((1, tm, c), lambda i, j: (i, j, 0)),
                  pl.BlockSpec((1, 8, c), lambda i, j: (i, jnp.maximum(j * r8 - 1, 0), 0)),
                  pl.BlockSpec((1, 8, c), lambda i, j: (i, jnp.minimum((j + 1) * r8, t // 8 - 1), 0)),
                  pl.BlockSpec((1, tm, 128), lambda i, j: (i, j, 0)),
                  full((8, c)), full((1, 128)), full((1, 128))],
        out_specs=[o256, o256, o256, pl.BlockSpec((1, tm, 128), lambda i, j: (i, j, 0))],
        out_shape=[SDS((b, t, 256), F32)] * 3 + [SDS((b, t, 128), F32)],
        compiler_params=_cp("parallel", "parallel"), name="dn_prep")(qkv, qkv, qkv, small, w8, alog, dtb)


def _scan_order(rev, step, n_first, n_total):
    back = jnp.where(step < n_first, n_first - 1 - step, n_total - 1 - (step - n_first))
    return jnp.where(rev, back, step)


def _dn_scan_body(rev, nst, nch, q_ref, k_ref, v_ref, gb_ref, eg_ref, eb_ref, o_ref, s_ref):
    c = DN_CHUNK
    tm = nst * nch * c
    chunks = range(nst * nch)
    rows = lambda x, ci: x[ci * c:(ci + 1) * c]

    @pl.when(pl.program_id(1) == 0)
    def _():
        s_ref[...] = jnp.zeros_like(s_ref)

    q, k, v, gb = [r[...].reshape(tm, r.shape[2]) for r in (q_ref, k_ref, v_ref, gb_ref)]
    gexp = _dot_sel(gb, eg_ref[0])
    bexp = _dot_sel(gb, eb_ref[0])
    r_i, c_i = _iota((c, c), 0), _iota((c, c), 1)
    tri = (c_i >= r_i) if rev else (c_i <= r_i)
    gc = jnp.concatenate([_sel_dot(tri, rows(gexp, ci)) for ci in chunks], axis=0)
    egc = jnp.exp(gc)
    kb = k * bexp
    rhs_u, rhs_w = v * bexp, kb * egc
    rr, cc = _iota((256, 256), 0), _iota((256, 256), 1)
    same_head = (rr >> 6) == (cc >> 6)
    i_idx = _iota((c, 256), 0)
    j_idx = _iota((c, 256), 1) & 63
    lane_h = _iota((c, 256), 1) >> 6
    d_ij = (j_idx - i_idx) if rev else (i_idx - j_idx)
    incl, strict = d_ij >= 0, d_ij > 0
    eye = (i_idx == j_idx).astype(F32)

    def stack4(x):
        return jnp.concatenate([jnp.where(lane_h == h, x, 0.0) for h in range(DN_HEADS)], axis=0).astype(BF16)

    def to_bd(xcat):
        return jnp.where(same_head, jnp.concatenate([xcat] * DN_HEADS, axis=0), 0.0).astype(BF16)

    both = lambda x, y: jnp.concatenate([x, y], axis=0).astype(BF16)
    gcc = [rows(gc, ci) for ci in chunks]
    gl = [g[0:1, :] if rev else g[c - 1:c, :] for g in gcc]
    gcr = [jnp.sum(jnp.where(d_ij <= 0, rows(gexp, ci), 0.0), axis=0, keepdims=True) for ci in chunks]
    decay = [jnp.where(incl, jnp.exp(jnp.where(incl, gcc[ci] - gcr[ci], 0.0)), 0.0) for ci in chunks]
    k4 = [stack4(rows(k, ci)) for ci in chunks]
    kq = [_nt(both(rows(kb, ci), rows(q, ci)), k4[ci]) for ci in chunks]
    lower = [jnp.where(strict, kq[ci][:c] * decay[ci], 0.0) for ci in chunks]
    a_intra = [jnp.where(incl, kq[ci][c:] * decay[ci], 0.0).astype(BF16) for ci in chunks]
    p = [eye - lower[ci] for ci in chunks]
    m = [_dot(lower[ci].astype(BF16), to_bd(lower[ci])) for ci in chunks]
    for it in range(5):
        mb = [to_bd(m[ci]) for ci in chunks]
        if it < 4:
            pm = [_dot(both(p[ci], m[ci]), mb[ci]) for ci in chunks]
            p = [p[ci] + pm[ci][:c] for ci in chunks]
            m = [pm[ci][c:] for ci in chunks]
        else:
            p = [p[ci] + _dot(p[ci].astype(BF16), mb[ci]) for ci in chunks]
    uw = [_dot(p[ci].astype(BF16), jnp.concatenate([stack4(rows(rhs_u, ci)), stack4(rows(rhs_w, ci))], axis=1))
          for ci in chunks]
    wq = [both(uw[ci][:, 256:], rows(q, ci) * rows(egc, ci)) for ci in chunks]
    k_dec = [(rows(k, ci) * jnp.exp(gl[ci] - gcc[ci])).astype(BF16) for ci in chunks]
    s = [s_ref[st] for st in range(nst)]
    outs = [None] * (nst * nch)
    for step in range(nch):
        for st in range(nst):
            ci = st * nch + (nch - 1 - step if rev else step)
            r = _dot(wq[ci], s[st].astype(BF16))
            v_new = uw[ci][:, :256] - r[:c]
            outs[ci] = r[c:] + _dot(a_intra[ci], stack4(v_new))
            s[st] = s[st] * jnp.exp(gl[ci]) + jnp.where(same_head, _tn(k_dec[ci], v_new.astype(BF16)), 0.0)
    for st in range(nst):
        s_ref[st] = s[st]
    o_ref[...] = jnp.concatenate(outs, axis=0).reshape(nst, nch * c, 256)


def _dn_scan_call(q, k, v, gb, eg, eb, n_ctx, rev):
    b, t, _ = q.shape
    tm = DN_BLOCK
    assert t % tm == 0 and n_ctx % tm == 0 and tm % DN_CHUNK == 0
    blk_of = lambda j: _scan_order(rev, j, n_ctx // tm, t // tm)
    nst = DN_STREAMS if b % DN_STREAMS == 0 else 1
    blk = lambda w: pl.BlockSpec((nst, tm, w), lambda i, j: (i, blk_of(j), 0))
    return pl.pallas_call(
        functools.partial(_dn_scan_body, rev, nst, tm // DN_CHUNK), grid=(b // nst, t // tm),
        in_specs=[blk(256), blk(256), blk(256), blk(128),
                  pl.BlockSpec((1, 128, 256), lambda i, j: (0, 0, 0)),
                  pl.BlockSpec((1, 128, 256), lambda i, j: (0, 0, 0))],
        out_specs=blk(256), out_shape=SDS((b, t, 256), F32),
        scratch_shapes=[pltpu.VMEM((nst, 256, 256), F32)],
        compiler_params=_cp("parallel", "arbitrary"), name="dn_scan")(q, k, v, gb, eg, eb)


def _s5_in_body(u_ref, mb_ref, o_ref):
    w = mb_ref.shape[2]
    acc = None
    for j in range(S5_CHUNK):
        part = _dot(u_ref[:, j * w:(j + 1) * w].astype(BF16), mb_ref[0, j])
        acc = part if acc is None else acc + part
    o_ref[0] = acc


def _s5_scan_body(n_sub, n_sub_ctx, bc_ref, al_ref, x_ref):
    rev = pl.program_id(0) == 1
    half = bc_ref.shape[1] // 2
    ar, ai = al_ref[0:1, :], al_ref[1:2, :]

    def sub(s, st):
        xr, xi = st
        rows = pl.ds(pl.multiple_of(_scan_order(rev, s, n_sub_ctx, n_sub) * 8, 8), 8)
        blk = bc_ref[rows, :]
        seen = []
        for kk in range(8):
            seen.append(jnp.concatenate([xr, xi], axis=1))
            inp = jnp.where(rev, blk[7 - kk:8 - kk], blk[kk:kk + 1])
            xr, xi = xr * ar - xi * ai + inp[:, :half], xi * ar + xr * ai + inp[:, half:]
        x_ref[rows, :] = jnp.concatenate([jnp.where(rev, seen[7 - r], seen[r]) for r in range(8)], axis=0)
        return xr, xi

    zero = jnp.zeros((1, half), F32)
    lax.fori_loop(0, n_sub, sub, (zero, zero))


def _s5_out_body(u_ref, x_ref, mc_ref, kbd_ref, y_ref, lhs_ref, toep_ref):
    lanes, w = x_ref.shape[2], kbd_ref.shape[2]
    rev = pl.program_id(0) == 1
    jo = pl.program_id(2)

    @pl.when(jo == 0)
    def _():
        lhs_ref[:, :lanes] = x_ref[0].astype(BF16)
        lhs_ref[:, lanes:] = u_ref[...].astype(BF16)

    for j in range(S5_CHUNK):
        feeds = jnp.where(rev, j >= jo, j <= jo)
        blk = kbd_ref[0, jnp.abs(jo - j)]
        toep_ref[j * w:(j + 1) * w, :] = jnp.where(feeds, blk, jnp.zeros_like(blk))
    y = _dot(lhs_ref[:, :lanes], mc_ref[0]) + _dot(lhs_ref[:, lanes:], toep_ref[...])
    y_ref[...] = y.astype(y_ref.dtype)


def _s5_call(u, m_b, m_c, kbd, al, n_ctx):
    b, t, w = u.shape
    L = S5_CHUNK
    n = t // L
    r = b * n
    tr = n
    lanes = m_b.shape[-1]
    u3 = u.reshape(r, L * w)
    ublk = pl.BlockSpec((tr, L * w), lambda d, i, k: (i, 0))
    nt = 512
    bc = pl.pallas_call(
        _s5_in_body, grid=(2, r // tr, lanes // nt),
        in_specs=[ublk, pl.BlockSpec((1, L, w, nt), lambda d, i, k: (d, 0, 0, k))],
        out_specs=pl.BlockSpec((1, tr, nt), lambda d, i, k: (d, i, k)),
        out_shape=SDS((2, r, lanes), F32), compiler_params=_cp("parallel", "parallel", "parallel"),
        name="s5_in")(u3, m_b)
    assert n % 8 == 0 and (n_ctx // L) % 8 == 0
    xprev = pl.pallas_call(
        functools.partial(_s5_scan_body, n // 8, n_ctx // L // 8), grid=(2, b),
        in_specs=[pl.BlockSpec((None, None, n, lanes), lambda d, i: (d, i, 0, 0)),
                  pl.BlockSpec((None, 8, lanes // 2), lambda d, i: (d, 0, 0))],
        out_specs=pl.BlockSpec((None, None, n, lanes), lambda d, i: (d, i, 0, 0)),
        out_shape=SDS((2, b, n, lanes), F32), compiler_params=_cp("parallel", "parallel"),
        name="s5_scan")(bc.reshape(2, b, n, lanes), al)
    y = pl.pallas_call(
        _s5_out_body, grid=(2, r // tr, L),
        in_specs=[ublk, pl.BlockSpec((1, tr, lanes), lambda d, i, k: (d, i, 0)),
                  pl.BlockSpec((1, lanes, w), lambda d, i, k: (d, 0, k)),
                  pl.BlockSpec((1, L, w, w), lambda d, i, k: (d, 0, 0, 0))],
        out_specs=pl.BlockSpec((None, tr, w), lambda d, i, k: (d, i, k)),
        out_shape=SDS((2, r, L * w), BF16),
        scratch_shapes=[pltpu.VMEM((tr, lanes + L * w), BF16), pltpu.VMEM((L * w, w), BF16)],
        compiler_params=_cp("parallel", "parallel", "arbitrary"), name="s5_out")(
            u3, xprev.reshape(2, r, lanes), m_c, kbd)
    return y.reshape(2, b, t, w)


def _s5_operators(a_re, a_im, log_dt, b_re, b_im, c_re, c_im):
    L = S5_CHUNK
    G, P, C = S5_GROUPS, S5_STATE, S5_GROUP
    dt = jnp.exp(log_dt)[:, :, None]
    ls = jnp.arange(L + 1, dtype=F32)
    mag = jnp.exp(a_re[..., None] * dt[..., None] * ls)
    ang = a_im[..., None] * dt[..., None] * ls
    pw_re, pw_im = mag * jnp.cos(ang), mag * jnp.sin(ang)
    ab_re, ab_im = pw_re[..., 1], pw_im[..., 1]
    den = a_re * a_re + a_im * a_im
    f_re = ((ab_re - 1.0) * a_re + ab_im * a_im) / den
    f_im = (ab_im * a_re - (ab_re - 1.0) * a_im) / den
    bb_re = f_re[..., None] * b_re[None] - f_im[..., None] * b_im[None]
    bb_im = f_re[..., None] * b_im[None] + f_im[..., None] * b_re[None]
    ca_re = c_re[..., None] * pw_re[:, :, None] - c_im[..., None] * pw_im[:, :, None]
    ca_im = c_re[..., None] * pw_im[:, :, None] + c_im[..., None] * pw_re[:, :, None]
    kern = (jnp.einsum("dgopl,dgpc->dglco", ca_re, bb_re, precision=HI)
            - jnp.einsum("dgopl,dgpc->dglco", ca_im, bb_im, precision=HI))
    GC, GP = G * C, G * P
    grp = lambda idx, per: (idx // per) % G
    row_expand = jnp.tile(jnp.eye(C, dtype=F32), (G, 1))
    gc_rows = grp(jnp.arange(GC), C)[:, None]
    kcomp = kern[:, :, :L].transpose(0, 2, 3, 1, 4).reshape(2, L, C, GC)
    kbd = jnp.where(gc_rows == grp(jnp.arange(GC), C)[None, :],
                    jnp.einsum("rc,dlcq->dlrq", row_expand, kcomp), 0.0)
    q_re, q_im = pw_re[..., :L], pw_im[..., :L]
    mb_re = q_re[..., None] * bb_re[:, :, :, None] - q_im[..., None] * bb_im[:, :, :, None]
    mb_im = q_re[..., None] * bb_im[:, :, :, None] + q_im[..., None] * bb_re[:, :, :, None]
    mb = jnp.stack([mb_re, mb_im], axis=1)
    mb = jnp.stack([mb[0][..., ::-1, :], mb[1]])
    mbcomp = mb.transpose(0, 4, 5, 1, 2, 3).reshape(2, L, C, 2 * GP)
    m_b = jnp.where(gc_rows == grp(jnp.arange(2 * GP), P)[None, :],
                    jnp.einsum("rc,djcq->djrq", row_expand, mbcomp), 0.0)
    w = jnp.stack([ca_re[..., 1:], -ca_im[..., 1:]], axis=1)
    w = jnp.stack([w[0], w[1][..., ::-1]])
    wcomp = w.transpose(0, 1, 2, 4, 5, 3).reshape(2, 2 * GP, L * C)
    col = jnp.arange(L * GC)
    col_expand = jnp.logical_and((jnp.arange(L * C) // C)[:, None] == (col // GC)[None, :],
                                 (jnp.arange(L * C) % C)[:, None] == (col % C)[None, :]).astype(F32)
    m_c = jnp.where(grp(jnp.arange(2 * GP), P)[:, None] == grp(col, C)[None, :],
                    jnp.einsum("drk,kq->drq", wcomp, col_expand), 0.0)
    al = jnp.stack([pw_re[..., L].reshape(2, GP), pw_im[..., L].reshape(2, GP)], axis=1)
    al = jnp.concatenate([al, jnp.zeros((2, 6, GP), F32)], axis=1)
    return m_b.astype(BF16), m_c.astype(BF16), kbd.astype(BF16), al


def _da_prep_body(x_ref, cos_ref, sa_ref, sb_ref, q_ref, k_ref, vt_ref):
    x = x_ref[0]
    cos, sa, sb = cos_ref[...], sa_ref[...], sb_ref[...]

    def rope(z):
        return z * cos + pltpu.roll(z, 240, 1) * sa + pltpu.roll(z, 16, 1) * sb

    q_ref[0] = (rope(x[:, 0:256]) * (DA_DK ** -0.5 * LOG2E)).astype(BF16)
    k_ref[0] = rope(x[:, 256:512]).astype(BF16)
    vt = x[:, 512:768].T.astype(BF16)
    ones = jnp.ones((DA_VROWS - DA_DV, vt.shape[1]), BF16)
    vt_ref[0, 0] = jnp.concatenate(
        [blk for h in range(DA_HEADS) for blk in (vt[h * DA_DV:(h + 1) * DA_DV], ones)], axis=0)


def _da_kblock(t):
    return next(c for c in (1408, 768, 512, 256) if t % c == 0)


DA_VROWS = DA_DV + 16
LOG2E = 1.4426950408889634


def _da_prep_call(qkv, cos, sa, sb):
    b, t, _ = qkv.shape
    tm = _da_kblock(t)
    assert t % tm == 0
    tab = pl.BlockSpec((tm, 256), lambda i, j: (j, 0))
    o = pl.BlockSpec((1, tm, 256), lambda i, j: (i, j, 0))
    return pl.pallas_call(
        _da_prep_body, grid=(b, t // tm),
        in_specs=[pl.BlockSpec((1, tm, 768), lambda i, j: (i, j, 0)), tab, tab, tab],
        out_specs=[o, o, pl.BlockSpec((1, 1, DA_HEADS * DA_VROWS, tm), lambda i, j: (i, j, 0, 0))],
        out_shape=[SDS((b, t, 256), BF16)] * 2 + [SDS((b, t // tm, DA_HEADS * DA_VROWS, tm), BF16)],
        compiler_params=_cp("parallel", "parallel"), name="da_prep")(qkv, cos, sa, sb)


def _attn_body(nkb, tk, tq, post, q_ref, k_ref, vt_ref, lam_ref, g_ref, o_ref, q8_ref, acc_ref, s0_ref, p7_ref,
               a7_ref):
    q = q_ref[0]
    lane = _iota((tq, 256), 1)
    for hm in range(8):
        q8_ref[hm] = jnp.where((lane >> 5) == hm, q, jnp.zeros_like(q))
    acc_ref[...] = jnp.zeros_like(acc_ref)

    def keys(i):
        return k_ref[0, pl.ds(pl.multiple_of(i * tk, 8), tk), :]

    s0_ref[...] = _nt(keys(0), q8_ref[0])

    nsplit = 4 if tk % 32 == 0 else 1
    p7_ref[...] = jnp.zeros_like(p7_ref)
    a7_ref[...] = jnp.ones_like(a7_ref)

    def values(i, hm):
        h = hm // 2
        return vt_ref[0, i][h * DA_VROWS:(h + 1) * DA_VROWS, :tk]

    def accumulate(hm, vals, p, alpha):
        acc_ref[hm] = acc_ref[hm] * alpha + _dot(vals, p)

    def kblock(i, carry):
        m = carry
        kb = keys(i)
        kb_next = keys(jnp.minimum(i + 1, nkb - 1))
        m_rows = []
        s = s0_ref[...]
        pending = (7, values(jnp.maximum(i - 1, 0), 7), p7_ref[...], a7_ref[0:1, :])
        for hm in range(8):
            s_next = _nt(kb, q8_ref[hm + 1]) if hm < 7 else _nt(kb_next, q8_ref[0])
            accumulate(*pending)
            m_old = m[hm:hm + 1, :]
            part = s[0:tk // nsplit]
            for r in range(1, nsplit):
                part = jnp.maximum(part, s[r * tk // nsplit:(r + 1) * tk // nsplit])
            m_new = jnp.maximum(m_old, jnp.max(part, axis=0, keepdims=True))
            alpha = jnp.exp2(m_old - m_new)
            p = jnp.exp2(s - m_new).astype(BF16)
            m_rows.append(m_new)
            pending = (hm, values(i, hm), p, alpha)
            s = s_next
        s0_ref[...] = s
        p7_ref[...] = pending[2]
        a7_ref[0:1, :] = pending[3]
        return jnp.concatenate(m_rows, axis=0)

    lax.fori_loop(0, nkb, kblock, jnp.full((8, tq), -1e30, F32), unroll=2 if nkb % 2 == 0 else 1)
    accumulate(7, values(nkb - 1, 7), p7_ref[...], a7_ref[0:1, :])
    lam = lam_ref[0:1, 0:1]
    heads = []
    for h in range(DA_HEADS):
        a1, a2 = acc_ref[2 * h], acc_ref[2 * h + 1]
        o = a1[0:DA_DV] / a1[DA_DV:DA_DV + 1] - lam * (a2[0:DA_DV] / a2[DA_DV:DA_DV + 1])
        ms = jnp.mean(o * o, axis=0, keepdims=True)
        heads.append(o * lax.rsqrt(ms + 1e-5) * g_ref[...] * post)
    o_ref[0] = jnp.concatenate(heads, axis=0).T


def _attn_call(q, k, vt, lam_row, gain_col, post, q_off, n_q, n_keys, tq, tk):
    b = q.shape[0]
    assert n_keys % tk == 0 and n_q % tq == 0 and q_off % tq == 0
    nkb = n_keys // tk
    assert tk == vt.shape[3] or nkb == 1
    qo = q_off // tq
    return pl.pallas_call(
        functools.partial(_attn_body, nkb, tk, tq, post), grid=(b, n_q // tq),
        in_specs=[pl.BlockSpec((1, tq, 256), lambda i, j: (i, j + qo, 0)),
                  pl.BlockSpec((1, n_keys, 256), lambda i, j: (i, 0, 0)),
                  pl.BlockSpec((1, nkb, vt.shape[2], vt.shape[3]), lambda i, j: (i, 0, 0, 0)),
                  pl.BlockSpec((1, 128), lambda i, j: (0, 0)),
                  pl.BlockSpec((DA_DV, 1), lambda i, j: (0, 0))],
        out_specs=pl.BlockSpec((1, tq, 256), lambda i, j: (i, j, 0)),
        out_shape=SDS((b, n_q, 256), F32),
        scratch_shapes=[pltpu.VMEM((8, tq, 256), BF16), pltpu.VMEM((8, DA_VROWS, tq), F32),
                        pltpu.VMEM((tk, tq), F32), pltpu.VMEM((tk, tq), BF16), pltpu.VMEM((8, tq), F32)],
        compiler_params=_cp("parallel", "parallel"), name="diff_attn")(q, k, vt, lam_row, gain_col)


def _gla_body(rev, n, qkv_ref, sm_ref, wg_ref, bg_ref, o_ref, st_ref):
    c = GLA_CHUNK
    tm = n * c

    @pl.when(pl.program_id(1) == 0)
    def _():
        st_ref[...] = jnp.zeros_like(st_ref)

    x = qkv_ref[0]
    q, k, v = x[:, 0:128] * (GLA_DK ** -0.5), x[:, 128:256], x[:, 256:512]
    z = _dot(sm_ref[0], wg_ref[0], precision=HI) + bg_ref[0]
    la = jax.nn.log_sigmoid(z) * (1.0 / GLA_TAU)
    r_i, c_i = _iota((tm, tm), 0), _iota((tm, tm), 1)
    before = (c_i >= r_i) if rev else (c_i <= r_i)
    tri = jnp.logical_and((r_i >> 4) == (c_i >> 4), before)
    b = _sel_dot(tri, la)
    chunk_sel = (_iota((tm, 128), 0) >> 4) == _iota((tm, 128), 1)
    a_cols = jnp.exp(_dot_sel(la, chunk_sel, dot=_tn))
    b3, q3, k3 = b.reshape(n, c, 128), q.reshape(n, c, 128), k.reshape(n, c, 128)
    v3 = v.reshape(n, c, 256)
    bl3 = b3[:, 0:1, :] if rev else b3[:, c - 1:c, :]
    row3 = _iota((n, c, 128), 1)
    head_ones = ((_iota((128, 256), 0) >> 5) == (_iota((128, 256), 1) >> 6)).astype(BF16)
    o3 = jnp.zeros((n, c, 256), F32)
    for j in range(c):
        ok = (row3 <= j) if rev else (row3 >= j)
        e = jnp.where(ok, jnp.exp(jnp.where(ok, b3 - b3[:, j:j + 1, :], 0.0)), 0.0)
        part = (e * q3 * k3[:, j:j + 1, :]).reshape(tm, 128).astype(BF16)
        o3 = o3 + _dot(part, head_ones).reshape(n, c, 256) * v3[:, j:j + 1, :]
    q_dec = (q3 * jnp.exp(b3)).astype(BF16)
    k_dec = (k3 * jnp.exp(bl3 - b3)).astype(BF16)
    vb = v3.astype(BF16)
    st_mask = (_iota((128, 256), 0) >> 5) == (_iota((128, 256), 1) >> 6)
    order = range(n - 1, -1, -1) if rev else range(n)
    upd = {ci: jnp.where(st_mask, _tn(k_dec[ci], vb[ci]), 0.0) for ci in order}
    st = st_ref[...]
    inter = [None] * n
    for ci in order:
        inter[ci] = _dot(q_dec[ci], st.astype(BF16))
        st = st * a_cols[:, ci:ci + 1] + upd[ci]
    st_ref[...] = st
    o_ref[0] = jnp.concatenate(inter, axis=0) + o3.reshape(tm, 256)


def _gla_call(qkv, small, wg, bg, n_ctx, rev):
    b, t, _ = qkv.shape
    tm = 256
    assert t % tm == 0 and n_ctx % tm == 0 and tm // GLA_CHUNK <= 128 and GLA_CHUNK == 16
    blk = lambda i, j: _scan_order(rev, j, n_ctx // tm, t // tm)
    return pl.pallas_call(
        functools.partial(_gla_body, rev, tm // GLA_CHUNK), grid=(b, t // tm),
        in_specs=[pl.BlockSpec((1, tm, 512), lambda i, j: (i, blk(i, j), 0)),
                  pl.BlockSpec((1, tm, 128), lambda i, j: (i, blk(i, j), 0)),
                  pl.BlockSpec((1, 128, 128), lambda i, j: (0, 0, 0)),
                  pl.BlockSpec((1, 1, 128), lambda i, j: (0, 0, 0))],
        out_specs=pl.BlockSpec((1, tm, 256), lambda i, j: (i, blk(i, j), 0)),
        out_shape=SDS((b, t, 256), F32), scratch_shapes=[pltpu.VMEM((128, 256), F32)],
        compiler_params=_cp("parallel", "arbitrary"), name="gla_scan")(qkv, small, wg, bg)


def _merge_body(n_ctx, tm, a_ref, h_ref, mv_ref, dnf_ref, dnr_ref, dnz_ref, s5y_ref, s5yr_ref, s5u_ref,
                da_ref, dac_ref, glf_ref, glr_ref, glz_ref, vec_ref, gluw_ref, wgate_ref, bgate_ref, wbr_ref,
                wout_ref, n2g_ref, wr_ref, br_ref, h_out, f_out, wt_out):
    t = pl.program_id(1)
    d = h_ref.shape[2]
    row = t * tm + _iota((tm, 1), 0)
    is_ctx = row < n_ctx
    mv = mv_ref[0]
    vec = vec_ref[...]
    ones64 = _block_ones(256, 6)
    z = dnz_ref[0]
    ya = _head_rms(dnf_ref[0] + dnr_ref[0], vec[0:1], NORM_EPS, ones64) * (z * jax.nn.sigmoid(z))
    u = s5u_ref[0].astype(F32)
    zz = jax.nn.gelu(s5y_ref[0].astype(F32) + s5yr_ref[0].astype(F32) + vec[2:3] * u)
    yb = zz * jax.nn.sigmoid(_bdot(zz, gluw_ref[...]) + vec[3:4])
    r = glz_ref[0]
    yd = _head_rms(glf_ref[0] + glr_ref[0], vec[1:2], NORM_EPS, ones64) * (r * jax.nn.sigmoid(r))
    a = a_ref[0]
    acc = jnp.zeros((tm, d), F32)
    yc = jnp.where(t * tm < n_ctx, dac_ref[0], da_ref[0])
    for i, y in enumerate((ya, yb, yc, yd)):
        gate = jax.nn.sigmoid(_dot(a, wgate_ref[:, i * d:(i + 1) * d]) + bgate_ref[:, i * d:(i + 1) * d])
        acc = acc + gate * _dot(y.astype(BF16), wbr_ref[i])
    out = _dot(acc.astype(BF16), wout_ref[...])
    gate2 = jnp.where(is_ctx, mv[10:11], mv[2:3])
    h_new = h_ref[0] + gate2 * out
    h_out[0] = h_new
    f = _modulated_norm(h_new, n2g_ref[...], mv, is_ctx, 3, 4)
    f_out[0] = f.astype(BF16)
    f_hi = f.astype(BF16)
    f_lo = (f - f_hi.astype(F32)).astype(BF16)
    wr = wr_ref[...]
    wr_hi = wr.astype(BF16)
    wr_lo = (wr - wr_hi.astype(F32)).astype(BF16)
    lg = _nt(jnp.concatenate([wr_hi, wr_lo], axis=0), f_hi)
    sc = jax.nn.sigmoid(lg[:N_EXPERTS] + lg[N_EXPERTS:] + _nt(wr_hi, f_lo))
    gr = sc + br_ref[...]
    rows_ = [gr[e:e + 1, :] for e in range(N_EXPERTS)]
    best_v, best_g = None, None
    for g in range(N_GROUPS):
        x = rows_[g * EPG:(g + 1) * EPG]
        top2 = None
        for i in range(EPG):
            for j in range(i + 1, EPG):
                s_ij = x[i] + x[j]
                top2 = s_ij if top2 is None else jnp.maximum(top2, s_ij)
        if best_v is None:
            best_v, best_g = top2, jnp.zeros_like(top2, dtype=jnp.int32)
        else:
            take = top2 > best_v
            best_v = jnp.where(take, top2, best_v)
            best_g = jnp.where(take, g, best_g)
    cand = []
    for kk in range(EPG):
        ck = rows_[kk]
        for g in range(1, N_GROUPS):
            ck = jnp.where(best_g == g, rows_[g * EPG + kk], ck)
        cand.append(ck)
    v1, i1 = cand[0], jnp.zeros_like(best_g)
    for kk in range(1, EPG):
        take = cand[kk] > v1
        v1 = jnp.where(take, cand[kk], v1)
        i1 = jnp.where(take, kk, i1)
    v2, i2 = jnp.full_like(v1, -jnp.inf), jnp.zeros_like(best_g)
    for kk in range(EPG):
        take = jnp.logical_and(i1 != kk, cand[kk] > v2)
        v2 = jnp.where(take, cand[kk], v2)
        i2 = jnp.where(take, kk, i2)
    e1, e2 = best_g * EPG + i1, best_g * EPG + i2
    e_iota = _iota((N_EXPERTS, tm), 0)
    s1 = jnp.sum(jnp.where(e_iota == e1, sc, 0.0), axis=0, keepdims=True)
    s2 = jnp.sum(jnp.where(e_iota == e2, sc, 0.0), axis=0, keepdims=True)
    tot = s1 + s2
    wt_out[0] = jnp.where(e_iota == e1, s1 / tot, jnp.where(e_iota == e2, s2 / tot, 0.0))


def _merge_call(a, h, modv, dn_f, dn_b, dn_z, s5_y, s5_u, da_lat, da_ctx, gl_f, gl_b, gl_r, vec, glu_w,
                w_gate, b_gate, w_branch, w_out, n2g, w_rt, b_rt, n_ctx):
    b, t, d = h.shape
    tm = 256
    assert t % tm == 0
    tok = lambda w: pl.BlockSpec((1, tm, w), lambda i, j: (i, j, 0))
    fwd = pl.BlockSpec((None, 1, tm, 256), lambda i, j: (0, i, j, 0))
    bwd = pl.BlockSpec((None, 1, tm, 256), lambda i, j: (1, i, j, 0))
    full = lambda *shape: pl.BlockSpec(shape, lambda i, j: (0,) * len(shape))
    return pl.pallas_call(
        functools.partial(_merge_body, n_ctx, tm), grid=(b, t // tm),
        in_specs=[tok(d), tok(d), pl.BlockSpec((1, 16, d), lambda i, j: (i, 0, 0)),
                  tok(256), tok(256), tok(256), fwd, bwd, tok(256),
                  pl.BlockSpec((1, tm, 256), lambda i, j: (i, jnp.maximum(j - n_ctx // tm, 0), 0)),
                  pl.BlockSpec((1, tm, 256), lambda i, j: (i, 0, 0)), tok(256), tok(256),
                  tok(256), full(8, 256), full(256, 256), full(d, 4 * d), full(1, 4 * d),
                  full(4, 256, d), full(d, d), full(1, d), full(N_EXPERTS, d), full(N_EXPERTS, 1)],
        out_specs=[tok(d), tok(d), pl.BlockSpec((1, N_EXPERTS, tm), lambda i, j: (i, 0, j))],
        out_shape=[SDS((b, t, d), F32), SDS((b, t, d), BF16), SDS((b, N_EXPERTS, t), F32)],
        compiler_params=_cp("parallel", "parallel"), name="merge_router")(
            a, h, modv, dn_f, dn_b, dn_z, s5_y, s5_y, s5_u, da_lat, da_ctx, gl_f, gl_b, gl_r, vec, glu_w,
            w_gate, b_gate, w_branch, w_out, n2g, w_rt, b_rt)


def _moe_body(n_ctx, tt, final, f_ref, wt_ref, h_ref, mv_ref, wg_ref, wu_ref, wd_ref, fg_ref, o_ref,
              acc_ref, rank_t_ref, rank_c_ref):
    e = pl.program_id(2)
    cap = MOE_CAP

    @pl.when(e == 0)
    def _():
        acc_ref[...] = jnp.zeros_like(acc_ref)
        routed = wt_ref[0] != 0.0
        earlier = (_iota((tt, tt), 0) < _iota((tt, tt), 1)).astype(BF16)
        rank = _dot(routed.astype(BF16), earlier)
        rank = jnp.where(routed, rank, -1.0)
        rank_t_ref[...] = rank
        eye = _iota((N_EXPERTS, N_EXPERTS), 0) == _iota((N_EXPERTS, N_EXPERTS), 1)
        rank_c_ref[...] = _dot_sel(rank, eye, dot=_tn)

    x = f_ref[0]
    rank_row = rank_t_ref[pl.ds(e, 1), :]
    w_row = wt_ref[0, pl.ds(e, 1), :]
    rc = rank_c_ref[...]
    rank_col = jnp.sum(jnp.where(_iota(rc.shape, 1) == e, rc, 0.0), axis=1, keepdims=True)
    count = jnp.sum((rank_row >= 0.0).astype(F32))

    for blk in range(-(-tt // cap)):
        @pl.when(count > blk * cap)
        def _(blk=blk):
            sel = rank_row == (_iota((cap, tt), 0) + blk * cap).astype(F32)
            xg = _dot(sel.astype(BF16), x).astype(BF16)
            w_g = jnp.sum(jnp.where(sel, w_row, 0.0), axis=1, keepdims=True)
            g = _dot(xg, wg_ref[0])
            hid = g * jax.nn.sigmoid(g) * _dot(xg, wu_ref[0]) * w_g
            y = _dot(hid.astype(BF16), wd_ref[0]).astype(BF16)
            back = rank_col == (_iota((tt, cap), 1) + blk * cap).astype(F32)
            acc_ref[...] += _dot(back.astype(BF16), y)

    @pl.when(e == N_EXPERTS - 1)
    def _():
        row = pl.program_id(1) * tt + _iota((tt, 1), 0)
        mv = mv_ref[0]
        gate = jnp.where(row < n_ctx, mv[13:14], mv[5:6])
        hn = h_ref[0] + gate * acc_ref[...]
        if final:
            hn = hn * lax.rsqrt(jnp.mean(hn * hn, axis=-1, keepdims=True) + NORM_EPS) * fg_ref[...]
        o_ref[0] = hn


def _moe_call(f, wc, h, modv, weg, weu, wed, final_g, n_ctx, final):
    b, t, d = h.shape
    fe = weg.shape[2]
    tt = next(c for c in (1408, 1024, 768, 512, 256, 128, 64, 32, 16, 8) if t % c == 0)
    tok = lambda w: pl.BlockSpec((1, tt, w), lambda i, j, e: (i, j, 0))
    return pl.pallas_call(
        functools.partial(_moe_body, n_ctx, tt, final), grid=(b, t // tt, N_EXPERTS),
        in_specs=[tok(d), pl.BlockSpec((1, N_EXPERTS, tt), lambda i, j, e: (i, 0, j)), tok(d),
                  pl.BlockSpec((1, 16, d), lambda i, j, e: (i, 0, 0)),
                  pl.BlockSpec((1, d, fe), lambda i, j, e: (e, 0, 0)),
                  pl.BlockSpec((1, d, fe), lambda i, j, e: (e, 0, 0)),
                  pl.BlockSpec((1, fe, d), lambda i, j, e: (e, 0, 0)),
                  pl.BlockSpec((1, d), lambda i, j, e: (0, 0))],
        out_specs=tok(d), out_shape=SDS((b, t, d), F32),
        scratch_shapes=[pltpu.VMEM((tt, d), F32), pltpu.VMEM((N_EXPERTS, tt), F32), pltpu.VMEM((tt, N_EXPERTS), F32)],
        compiler_params=_cp("parallel", "parallel", "arbitrary"), name="moe_experts")(
            f, wc, h, modv, weg, weu, wed, final_g)


def _rope_tables(n_ctx, s):
    t = jnp.arange(s)
    rowp, colp = (t // GRID_W).astype(F32), (t % GRID_W).astype(F32)
    n_freq = DA_DK // 4
    inv = ROPE_BASE ** (-jnp.arange(n_freq, dtype=F32) / n_freq)
    ang = jnp.concatenate([rowp[:, None] * inv, colp[:, None] * inv], axis=-1)
    cos, sin = jnp.cos(ang), jnp.sin(ang)
    zero = jnp.zeros_like(sin)
    cos32 = jnp.concatenate([cos, cos], axis=-1)
    sa32 = jnp.concatenate([-sin, zero], axis=-1)
    sb32 = jnp.concatenate([zero, sin], axis=-1)
    def full(tab, ctx_val):
        tab = jnp.tile(tab, (1, 8))
        return jnp.concatenate([jnp.full((n_ctx, 256), ctx_val, F32), tab], axis=0)
    return full(cos32, 1.0), full(sa32, 0.0), full(sb32, 0.0)


def _perm_w_in(w_in):
    offs, o = {}, 0
    for name, width in (("dn_qkv", 768), ("dn_z", 256), ("dn_a", 8), ("dn_b", 8), ("s5_u", 256), ("da_q", 256),
                        ("da_k", 256), ("da_v", 256), ("gla_q", 128), ("gla_k", 128), ("gla_v", 256),
                        ("gla_r", 256), ("gla_gate", 32)):
        offs[name] = (o, width)
        o += width
    col = lambda n: w_in[:, offs[n][0]:offs[n][0] + offs[n][1]]
    pad = jnp.zeros((w_in.shape[0], 128 - 48), w_in.dtype)
    return jnp.concatenate([col(n) for n in ("dn_qkv", "dn_z", "s5_u", "da_q", "da_k", "da_v", "gla_q", "gla_k",
                                              "gla_v", "gla_r", "dn_a", "dn_b", "gla_gate")] + [pad], axis=1)


def kernel(x, c, ctx, c_ctx, w_mod, b_mod, norm1_g, norm2_g, w_in, dn_conv, dn_a_log, dn_dt_bias, dn_norm_g,
           s5_a_re, s5_a_im, s5_log_dt, s5_b_re, s5_b_im, s5_c_re, s5_c_im, s5_d, s5_glu_w, s5_glu_b,
           da_lambda, da_norm_g, gla_w_gate, gla_b_gate, gla_norm_g, w_branch, w_gate, b_gate, w_out,
           w_router, b_router, w_e_gate, w_e_up, w_e_down, final_g):
    b, s, d = x.shape
    n_ctx = ctx.shape[1]
    t = n_ctx + s
    depth = w_mod.shape[0]
    assert b + 1 <= 8 and b == 4

    cvec = jnp.zeros((8, d), F32).at[:b].set(c).at[b].set(c_ctx)
    mod_all = _mod_call(cvec, w_mod, b_mod).reshape(depth, 8, 6, d)
    cos_t, sa_t, sb_t = _rope_tables(n_ctx, s)
    h = jnp.concatenate([ctx, x], axis=1)

    lane256 = jnp.arange(256) // 64
    eg = jnp.stack([(jnp.arange(128)[:, None] == (SM_DN_A + dd * 4 + lane256)[None, :]).astype(F32) for dd in range(2)])
    eb = jnp.stack([(jnp.arange(128)[:, None] == (SM_DN_B + dd * 4 + lane256)[None, :]).astype(F32) for dd in range(2)])

    for layer in range(depth):
        lam_init = 0.8 - 0.6 * math.exp(-0.3 * layer)
        ml = mod_all[layer]
        modv = jnp.zeros((b, 16, d), F32).at[:, 0:6].set(ml[:b]).at[:, 8:14].set(jnp.broadcast_to(ml[b], (b, 6, d)))

        a, dn_qkv, dn_z, s5_u, da_qkv, gla_qkv, gla_r, small = _proj_call(
            h, modv, norm1_g[layer][None, :], _perm_w_in(w_in[layer]).astype(BF16), n_ctx)

        dq, dk_, dv, gb = _dn_prep_call(dn_qkv, small, dn_conv[layer], dn_a_log[layer], dn_dt_bias[layer], n_ctx)
        dn_f, dn_b = [_dn_scan_call(dq, dk_, dv, gb, eg[dd:dd + 1], eb[dd:dd + 1], n_ctx, dd == 1)
                      for dd in range(2)]

        m_b, m_c, kbd, al = _s5_operators(
            s5_a_re[layer], s5_a_im[layer], s5_log_dt[layer], s5_b_re[layer], s5_b_im[layer],
            s5_c_re[layer], s5_c_im[layer])
        s5_y = _s5_call(s5_u, m_b, m_c, kbd, al, n_ctx)

        lf = da_lambda[layer]
        lam = jnp.exp(jnp.sum(lf[0] * lf[1])) - jnp.exp(jnp.sum(lf[2] * lf[3])) + lam_init
        lam_row = jnp.full((1, 128), lam, F32)
        da_gain = da_norm_g[layer][:, None]
        aq, ak, avt = _da_prep_call(da_qkv, cos_t, sa_t, sb_t)
        da_lat = _attn_call(aq, ak, avt, lam_row, da_gain, 1.0 - lam_init, n_ctx, s, t, 256, _da_kblock(t))
        if layer < depth - 1:
            da_ctx = _attn_call(aq, ak, avt, lam_row, da_gain, 1.0 - lam_init, 0, n_ctx, n_ctx, 256, n_ctx)
        else:
            da_ctx = da_lat

        wg2 = jnp.zeros((2, 128, 128), F32)
        for dd in range(2):
            wg2 = wg2.at[dd, SM_GLA + dd * GLA_RANK:SM_GLA + (dd + 1) * GLA_RANK].set(gla_w_gate[layer, dd])
        gl_f, gl_b = [_gla_call(gla_qkv, small, wg2[dd:dd + 1], gla_b_gate[layer][dd:dd + 1, None, :], n_ctx, dd == 1)
                      for dd in range(2)]

        vec = jnp.zeros((8, 256), F32)
        vec = vec.at[0].set(jnp.tile(dn_norm_g[layer], DN_HEADS)).at[1].set(jnp.tile(gla_norm_g[layer], GLA_HEADS))
        vec = vec.at[2].set(s5_d[layer]).at[3].set(s5_glu_b[layer])
        h, f, wc = _merge_call(
            a, h, modv, dn_f, dn_b, dn_z, s5_y, s5_u, da_lat, da_ctx, gl_f, gl_b, gla_r, vec,
            s5_glu_w[layer].astype(BF16), w_gate[layer].astype(BF16), b_gate[layer][None, :],
            w_branch[layer].astype(BF16), w_out[layer].astype(BF16), norm2_g[layer][None, :],
            w_router.T, b_router[:, None], n_ctx)

        h = _moe_call(f, wc, h, modv, w_e_gate[layer].astype(BF16), w_e_up[layer].astype(BF16),
                      w_e_down[layer].astype(BF16), final_g[None, :], n_ctx, layer == depth - 1)
    return h[:, n_ctx:]
```

```python
import functools
import math

import jax
import jax.numpy as jnp
from jax import lax
from jax.experimental import pallas as pl
from jax.experimental.pallas import tpu as pltpu

F32 = jnp.float32
BF16 = jnp.bfloat16
HI = lax.Precision.HIGHEST
SDS = jax.ShapeDtypeStruct

GRID_W = 64
DN_HEADS, DN_DK, DN_DV, DN_CONV, DN_CHUNK = 4, 64, 64, 5, 64
DN_BLOCK = 256
DN_STREAMS = 4
S5_WIDTH, S5_GROUP, S5_STATE = 256, 16, 64
S5_GROUPS = S5_WIDTH // S5_GROUP
S5_CHUNK = 16
DA_HEADS, DA_DK, DA_DV = 4, 32, 64
ROPE_BASE = 10000.0
GLA_HEADS, GLA_DK, GLA_DV, GLA_RANK, GLA_TAU, GLA_CHUNK = 4, 32, 64, 16, 16.0, 16
N_EXPERTS, N_GROUPS, TOP_K = 16, 4, 2
MOE_CAP = 208
EPG = N_EXPERTS // N_GROUPS
NORM_EPS = 1e-6
VMEM_LIMIT = 56 * 1024 * 1024
MERGE_VMEM_LIMIT = 60 * 1024 * 1024

SEG = dict(dn_qkv=(0, 768), dn_z=(768, 256), s5_u=(1024, 256), da_qkv=(1280, 768),
           gla_qkv=(2048, 512), gla_r=(2560, 256), small=(2816, 128))
D_PROJ = 2944
SM_DN_A, SM_DN_B, SM_GLA = 0, 8, 16


def _cp(*sem, vmem=VMEM_LIMIT):
    return pltpu.CompilerParams(dimension_semantics=sem, vmem_limit_bytes=vmem)


def _nt(a, b, **kw):
    return lax.dot_general(a, b, (((1,), (1,)), ((), ())), preferred_element_type=F32, **kw)


def _tn(a, b, **kw):
    return lax.dot_general(a, b, (((0,), (0,)), ((), ())), preferred_element_type=F32, **kw)


def _dot(a, b, **kw):
    return jnp.dot(a, b, preferred_element_type=F32, **kw)


def _bdot(a, b):
    return jnp.dot(a.astype(BF16), b.astype(BF16), preferred_element_type=F32)


def _split3(x):
    hi = x.astype(BF16)
    r = x - hi.astype(F32)
    mid = r.astype(BF16)
    return hi, mid, (r - mid.astype(F32)).astype(BF16)


def _dot_sel(x, sel, dot=None):
    dot = dot or _dot
    sel = sel.astype(BF16)
    hi, mid, lo = _split3(x)
    return dot(hi, sel) + dot(mid, sel) + dot(lo, sel)


def _sel_dot(sel, x):
    sel = sel.astype(BF16)
    hi, mid, lo = _split3(x)
    return _dot(sel, hi) + _dot(sel, mid) + _dot(sel, lo)


def _iota(shape, dim):
    return lax.broadcasted_iota(jnp.int32, shape, dim)


def _block_ones(n, blk_shift):
    r, c = _iota((n, n), 0), _iota((n, n), 1)
    return ((r >> blk_shift) == (c >> blk_shift)).astype(F32)


def _head_rms(o, gain, eps, ones64):
    ms = _dot_sel(o * o, ones64) * (1.0 / 64.0)
    return o * lax.rsqrt(ms + eps) * gain


def _mod_body(c_ref, w_ref, b_ref, o_ref):
    c = c_ref[...]
    s = c * jax.nn.sigmoid(c)
    o_ref[0] = _dot(s, w_ref[0], precision=HI) + b_ref[0]


def _mod_call(cvec, w_mod, b_mod):
    n_layer, d, d6 = w_mod.shape
    tn = 1536
    return pl.pallas_call(
        _mod_body, grid=(n_layer, d6 // tn),
        in_specs=[pl.BlockSpec((8, d), lambda l, j: (0, 0)),
                  pl.BlockSpec((1, d, tn), lambda l, j: (l, 0, j)),
                  pl.BlockSpec((1, 1, tn), lambda l, j: (l, 0, j))],
        out_specs=pl.BlockSpec((1, 8, tn), lambda l, j: (l, 0, j)),
        out_shape=SDS((n_layer, 8, d6), F32), compiler_params=_cp("parallel", "parallel"),
        name="adaln_mod")(cvec, w_mod, b_mod.reshape(n_layer, 1, d6))


def _modulated_norm(x, gain, mv, is_ctx, k_shift, k_scale):
    y = x * lax.rsqrt(jnp.mean(x * x, axis=-1, keepdims=True) + NORM_EPS) * gain
    shift = jnp.where(is_ctx, mv[8 + k_shift:9 + k_shift], mv[k_shift:k_shift + 1])
    scale = jnp.where(is_ctx, mv[8 + k_scale:9 + k_scale], mv[k_scale:k_scale + 1])
    return y * (1.0 + scale) + shift


def _proj_body(n_ctx, tm, h_ref, mv_ref, g_ref, w_ref, a_ref, *out_refs):
    t = pl.program_id(1)
    row = t * tm + _iota((tm, 1), 0)
    a = _modulated_norm(h_ref[0], g_ref[...], mv_ref[0], row < n_ctx, 0, 1)
    ab = a.astype(BF16)
    a_ref[0] = ab
    p = _dot(ab, w_ref[...])
    for ref, (off, width) in zip(out_refs, SEG.values()):
        ref[0] = p[:, off:off + width].astype(ref.dtype)


def _proj_call(h, modv, gain, w_perm, n_ctx):
    b, t, d = h.shape
    tm = 384
    assert t % tm == 0
    outs = [SDS((b, t, d), BF16)] + [SDS((b, t, w), BF16 if name == "s5_u" else F32)
                                     for name, (_, w) in SEG.items()]
    out_specs = [pl.BlockSpec((1, tm, d), lambda i, j: (i, j, 0))] + [
        pl.BlockSpec((1, tm, w), lambda i, j: (i, j, 0)) for _, w in SEG.values()]
    return pl.pallas_call(
        functools.partial(_proj_body, n_ctx, tm), grid=(b, t // tm),
        in_specs=[pl.BlockSpec((1, tm, d), lambda i, j: (i, j, 0)),
                  pl.BlockSpec((1, 16, d), lambda i, j: (i, 0, 0)),
                  pl.BlockSpec((1, d), lambda i, j: (0, 0)),
                  pl.BlockSpec((d, D_PROJ), lambda i, j: (0, 0))],
        out_specs=out_specs, out_shape=outs, compiler_params=_cp("parallel", "parallel"),
        name="norm_proj")(h, modv, gain, w_perm)


def _dn_prep_body(nct, ntile, tm, x_ref, pv_ref, nx_ref, sm_ref, w_ref, alog_ref, dtb_ref,
                  q_ref, k_ref, v_ref, gb_ref):
    t = pl.program_id(1)
    first = jnp.logical_or(t == 0, t == nct)
    last = jnp.logical_or(t == nct - 1, t == ntile - 1)
    x = x_ref[0]
    pv = jnp.where(first, 0.0, pv_ref[0])
    nx = jnp.where(last, 0.0, nx_ref[0])
    w = w_ref[...]
    row = _iota((tm, 1), 0)
    xm1 = jnp.where(row == 0, pv[7:8], pltpu.roll(x, 1, 0))
    xm2 = jnp.where(row == 0, pv[6:7], jnp.where(row == 1, pv[7:8], pltpu.roll(x, 2, 0)))
    xp1 = jnp.where(row == tm - 1, nx[0:1], pltpu.roll(x, tm - 1, 0))
    xp2 = jnp.where(row == tm - 2, nx[0:1], jnp.where(row == tm - 1, nx[1:2], pltpu.roll(x, tm - 2, 0)))
    acc = w[0:1] * xm2 + w[1:2] * xm1 + w[2:3] * x + w[3:4] * xp1 + w[4:5] * xp2
    y = acc * jax.nn.sigmoid(acc)
    ones64 = _block_ones(256, 6)
    q, k = y[:, 0:256], y[:, 256:512]
    qs = _dot_sel(q * q, ones64)
    ks = _dot_sel(k * k, ones64)
    q_ref[0] = q * lax.rsqrt(qs + 1e-6) * (DN_DK ** -0.5)
    k_ref[0] = k * lax.rsqrt(ks + 1e-6)
    v_ref[0] = y[:, 512:768]
    sm = sm_ref[0]
    lane = _iota((1, 128), 1)
    g = -jnp.exp(alog_ref[...]) * jax.nn.softplus(sm + dtb_ref[...])
    be = jax.nn.sigmoid(sm)
    gb_ref[0] = jnp.where(lane < 8, g, jnp.where(lane < 16, be, 0.0))


def _dn_prep_call(qkv, small, conv_w, a_log, dt_bias, n_ctx):
    b, t, c = qkv.shape
    tm = 256
    assert n_ctx % tm == 0 and t % tm == 0
    nct, ntile, r8 = n_ctx // tm, t // tm, tm // 8
    w8 = jnp.zeros((8, c), F32).at[:DN_CONV].set(conv_w)
    alog = jnp.zeros((1, 128), F32).at[0, :8].set(a_log.reshape(-1))
    dtb = jnp.zeros((1, 128), F32).at[0, :8].set(dt_bias.reshape(-1))
    full = lambda shape: pl.BlockSpec(shape, lambda i, j: (0,) * len(shape))
    o256 = pl.BlockSpec((1, tm, 256), lambda i, j: (i, j, 0))
    return pl.pallas_call(
        functools.partial(_dn_prep_body, nct, ntile, tm), grid=(b, ntile),
        in_specs=[pl.BlockSpec((1, tm, c), lambda i, j: (i, j, 0)),
                  pl.BlockSpec((1, 8, c), lambda i, j: (i, jnp.maximum(j * r8 - 1, 0), 0)),
                  pl.BlockSpec((1, 8, c), lambda i, j: (i, jnp.minimum((j + 1) * r8, t // 8 - 1), 0)),
                  pl.BlockSpec((1, tm, 128), lambda i, j: (i, j, 0)),
                  full((8, c)), full((1, 128)), full((1, 128))],
        out_specs=[o256, o256, o256, pl.BlockSpec((1, tm, 128), lambda i, j: (i, j, 0))],
        out_shape=[SDS((b, t, 256), F32)] * 3 + [SDS((b, t, 128), F32)],
        compiler_params=_cp("parallel", "parallel"), name="dn_prep")(qkv, qkv, qkv, small, w8, alog, dtb)


def _scan_order(rev, step, n_first, n_total):
    back = jnp.where(step < n_first, n_first - 1 - step, n_total - 1 - (step - n_first))
    return jnp.where(rev, back, step)


def _dn_scan_body(rev, nst, nch, q_ref, k_ref, v_ref, gb_ref, eg_ref, eb_ref, o_ref, s_ref):
    c = DN_CHUNK
    tm = nst * nch * c
    chunks = range(nst * nch)
    rows = lambda x, ci: x[ci * c:(ci + 1) * c]

    @pl.when(pl.program_id(1) == 0)
    def _():
        s_ref[...] = jnp.zeros_like(s_ref)

    q, k, v, gb = [r[...].reshape(tm, r.shape[2]) for r in (q_ref, k_ref, v_ref, gb_ref)]
    gexp = _dot_sel(gb, eg_ref[0])
    bexp = _dot_sel(gb, eb_ref[0])
    r_i, c_i = _iota((c, c), 0), _iota((c, c), 1)
    tri = (c_i >= r_i) if rev else (c_i <= r_i)
    gc = jnp.concatenate([_sel_dot(tri, rows(gexp, ci)) for ci in chunks], axis=0)
    egc = jnp.exp(gc)
    kb = k * bexp
    rhs_u, rhs_w = v * bexp, kb * egc
    rr, cc = _iota((256, 256), 0), _iota((256, 256), 1)
    same_head = (rr >> 6) == (cc >> 6)
    i_idx = _iota((c, 256), 0)
    j_idx = _iota((c, 256), 1) & 63
    lane_h = _iota((c, 256), 1) >> 6
    d_ij = (j_idx - i_idx) if rev else (i_idx - j_idx)
    incl, strict = d_ij >= 0, d_ij > 0
    eye = (i_idx == j_idx).astype(F32)

    def stack4(x):
        return jnp.concatenate([jnp.where(lane_h == h, x, 0.0) for h in range(DN_HEADS)], axis=0).astype(BF16)

    def to_bd(xcat):
        return jnp.where(same_head, jnp.concatenate([xcat] * DN_HEADS, axis=0), 0.0).astype(BF16)

    both = lambda x, y: jnp.concatenate([x, y], axis=0).astype(BF16)
    gcc = [rows(gc, ci) for ci in chunks]
    gl = [g[0:1, :] if rev else g[c - 1:c, :] for g in gcc]
    gcr = [jnp.sum(jnp.where(d_ij <= 0, rows(gexp, ci), 0.0), axis=0, keepdims=True) for ci in chunks]
    decay = [jnp.where(incl, jnp.exp(jnp.where(incl, gcc[ci] - gcr[ci], 0.0)), 0.0) for ci in chunks]
    k4 = [stack4(rows(k, ci)) for ci in chunks]
    kq = [_nt(both(rows(kb, ci), rows(q, ci)), k4[ci]) for ci in chunks]
    lower = [jnp.where(strict, kq[ci][:c] * decay[ci], 0.0) for ci in chunks]
    a_intra = [jnp.where(incl, kq[ci][c:] * decay[ci], 0.0).astype(BF16) for ci in chunks]
    p = [eye - lower[ci] for ci in chunks]
    m = [_dot(lower[ci].astype(BF16), to_bd(lower[ci])) for ci in chunks]
    for it in range(5):
        mb = [to_bd(m[ci]) for ci in chunks]
        if it < 4:
            pm = [_dot(both(p[ci], m[ci]), mb[ci]) for ci in chunks]
            p = [p[ci] + pm[ci][:c] for ci in chunks]
            m = [pm[ci][c:] for ci in chunks]
        else:
            p = [p[ci] + _dot(p[ci].astype(BF16), mb[ci]) for ci in chunks]
    uw = [_dot(p[ci].astype(BF16), jnp.concatenate([stack4(rows(rhs_u, ci)), stack4(rows(rhs_w, ci))], axis=1))
          for ci in chunks]
    wq = [both(uw[ci][:, 256:], rows(q, ci) * rows(egc, ci)) for ci in chunks]
    k_dec = [(rows(k, ci) * jnp.exp(gl[ci] - gcc[ci])).astype(BF16) for ci in chunks]
    s = [s_ref[st] for st in range(nst)]
    outs = [None] * (nst * nch)
    for step in range(nch):
        for st in range(nst):
            ci = st * nch + (nch - 1 - step if rev else step)
            r = _dot(wq[ci], s[st].astype(BF16))
            v_new = uw[ci][:, :256] - r[:c]
            outs[ci] = r[c:] + _dot(a_intra[ci], stack4(v_new))
            s[st] = s[st] * jnp.exp(gl[ci]) + jnp.where(same_head, _tn(k_dec[ci], v_new.astype(BF16)), 0.0)
    for st in range(nst):
        s_ref[st] = s[st]
    o_ref[...] = jnp.concatenate(outs, axis=0).reshape(nst, nch * c, 256)


def _dn_scan_call(q, k, v, gb, eg, eb, n_ctx, rev):
    b, t, _ = q.shape
    tm = DN_BLOCK
    assert t % tm == 0 and n_ctx % tm == 0 and tm % DN_CHUNK == 0
    blk_of = lambda j: _scan_order(rev, j, n_ctx // tm, t // tm)
    nst = DN_STREAMS if b % DN_STREAMS == 0 else 1
    blk = lambda w: pl.BlockSpec((nst, tm, w), lambda i, j: (i, blk_of(j), 0))
    return pl.pallas_call(
        functools.partial(_dn_scan_body, rev, nst, tm // DN_CHUNK), grid=(b // nst, t // tm),
        in_specs=[blk(256), blk(256), blk(256), blk(128),
                  pl.BlockSpec((1, 128, 256), lambda i, j: (0, 0, 0)),
                  pl.BlockSpec((1, 128, 256), lambda i, j: (0, 0, 0))],
        out_specs=blk(256), out_shape=SDS((b, t, 256), F32),
        scratch_shapes=[pltpu.VMEM((nst, 256, 256), F32)],
        compiler_params=_cp("parallel", "arbitrary"), name="dn_scan")(q, k, v, gb, eg, eb)


def _s5_in_body(u_ref, mb_ref, o_ref):
    w = mb_ref.shape[2]
    acc = None
    for j in range(S5_CHUNK):
        part = _dot(u_ref[:, j * w:(j + 1) * w].astype(BF16), mb_ref[0, j])
        acc = part if acc is None else acc + part
    o_ref[0] = acc


def _s5_scan_body(n_sub, n_sub_ctx, bc_ref, al_ref, x_ref):
    rev = pl.program_id(0) == 1
    half = bc_ref.shape[1] // 2
    ar, ai = al_ref[0:1, :], al_ref[1:2, :]

    def sub(s, st):
        xr, xi = st
        rows = pl.ds(pl.multiple_of(_scan_order(rev, s, n_sub_ctx, n_sub) * 8, 8), 8)
        blk = bc_ref[rows, :]
        seen = []
        for kk in range(8):
            seen.append(jnp.concatenate([xr, xi], axis=1))
            inp = jnp.where(rev, blk[7 - kk:8 - kk], blk[kk:kk + 1])
            xr, xi = xr * ar - xi * ai + inp[:, :half], xi * ar + xr * ai + inp[:, half:]
        x_ref[rows, :] = jnp.concatenate([jnp.where(rev, seen[7 - r], seen[r]) for r in range(8)], axis=0)
        return xr, xi

    zero = jnp.zeros((1, half), F32)
    lax.fori_loop(0, n_sub, sub, (zero, zero))


def _s5_out_body(u_ref, x_ref, mc_ref, kbd_ref, y_ref, lhs_ref, toep_ref):
    lanes, w = x_ref.shape[2], kbd_ref.shape[2]
    rev = pl.program_id(0) == 1
    jo = pl.program_id(2)

    @pl.when(jo == 0)
    def _():
        lhs_ref[:, :lanes] = x_ref[0].astype(BF16)
        lhs_ref[:, lanes:] = u_ref[...].astype(BF16)

    for j in range(S5_CHUNK):
        feeds = jnp.where(rev, j >= jo, j <= jo)
        blk = kbd_ref[0, jnp.abs(jo - j)]
        toep_ref[j * w:(j + 1) * w, :] = jnp.where(feeds, blk, jnp.zeros_like(blk))
    y = _dot(lhs_ref[:, :lanes], mc_ref[0]) + _dot(lhs_ref[:, lanes:], toep_ref[...])
    y_ref[...] = y.astype(y_ref.dtype)


def _s5_call(u, m_b, m_c, kbd, al, n_ctx):
    b, t, w = u.shape
    L = S5_CHUNK
    n = t // L
    r = b * n
    tr = n
    lanes = m_b.shape[-1]
    u3 = u.reshape(r, L * w)
    ublk = pl.BlockSpec((tr, L * w), lambda d, i, k: (i, 0))
    nt = 512
    bc = pl.pallas_call(
        _s5_in_body, grid=(2, r // tr, lanes // nt),
        in_specs=[ublk, pl.BlockSpec((1, L, w, nt), lambda d, i, k: (d, 0, 0, k))],
        out_specs=pl.BlockSpec((1, tr, nt), lambda d, i, k: (d, i, k)),
        out_shape=SDS((2, r, lanes), F32), compiler_params=_cp("parallel", "parallel", "parallel"),
        name="s5_in")(u3, m_b)
    assert n % 8 == 0 and (n_ctx // L) % 8 == 0
    xprev = pl.pallas_call(
        functools.partial(_s5_scan_body, n // 8, n_ctx // L // 8), grid=(2, b),
        in_specs=[pl.BlockSpec((None, None, n, lanes), lambda d, i: (d, i, 0, 0)),
                  pl.BlockSpec((None, 8, lanes // 2), lambda d, i: (d, 0, 0))],
        out_specs=pl.BlockSpec((None, None, n, lanes), lambda d, i: (d, i, 0, 0)),
        out_shape=SDS((2, b, n, lanes), F32), compiler_params=_cp("parallel", "parallel"),
        name="s5_scan")(bc.reshape(2, b, n, lanes), al)
    y = pl.pallas_call(
        _s5_out_body, grid=(2, r // tr, L),
        in_specs=[ublk, pl.BlockSpec((1, tr, lanes), lambda d, i, k: (d, i, 0)),
                  pl.BlockSpec((1, lanes, w), lambda d, i, k: (d, 0, k)),
                  pl.BlockSpec((1, L, w, w), lambda d, i, k: (d, 0, 0, 0))],
        out_specs=pl.BlockSpec((None, tr, w), lambda d, i, k: (d, i, k)),
        out_shape=SDS((2, r, L * w), BF16),
        scratch_shapes=[pltpu.VMEM((tr, lanes + L * w), BF16), pltpu.VMEM((L * w, w), BF16)],
        compiler_params=_cp("parallel", "parallel", "arbitrary"), name="s5_out")(
            u3, xprev.reshape(2, r, lanes), m_c, kbd)
    return y.reshape(2, b, t, w)


def _s5_operators(a_re, a_im, log_dt, b_re, b_im, c_re, c_im):
    L = S5_CHUNK
    G, P, C = S5_GROUPS, S5_STATE, S5_GROUP
    dt = jnp.exp(log_dt)[:, :, None]
    ls = jnp.arange(L + 1, dtype=F32)
    mag = jnp.exp(a_re[..., None] * dt[..., None] * ls)
    ang = a_im[..., None] * dt[..., None] * ls
    pw_re, pw_im = mag * jnp.cos(ang), mag * jnp.sin(ang)
    ab_re, ab_im = pw_re[..., 1], pw_im[..., 1]
    den = a_re * a_re + a_im * a_im
    f_re = ((ab_re - 1.0) * a_re + ab_im * a_im) / den
    f_im = (ab_im * a_re - (ab_re - 1.0) * a_im) / den
    bb_re = f_re[..., None] * b_re[None] - f_im[..., None] * b_im[None]
    bb_im = f_re[..., None] * b_im[None] + f_im[..., None] * b_re[None]
    ca_re = c_re[..., None] * pw_re[:, :, None] - c_im[..., None] * pw_im[:, :, None]
    ca_im = c_re[..., None] * pw_im[:, :, None] + c_im[..., None] * pw_re[:, :, None]
    kern = (jnp.einsum("dgopl,dgpc->dglco", ca_re, bb_re, precision=HI)
            - jnp.einsum("dgopl,dgpc->dglco", ca_im, bb_im, precision=HI))
    GC, GP = G * C, G * P
    grp = lambda idx, per: (idx // per) % G
    row_expand = jnp.tile(jnp.eye(C, dtype=F32), (G, 1))
    gc_rows = grp(jnp.arange(GC), C)[:, None]
    kcomp = kern[:, :, :L].transpose(0, 2, 3, 1, 4).reshape(2, L, C, GC)
    kbd = jnp.where(gc_rows == grp(jnp.arange(GC), C)[None, :],
                    jnp.einsum("rc,dlcq->dlrq", row_expand, kcomp), 0.0)
    q_re, q_im = pw_re[..., :L], pw_im[..., :L]
    mb_re = q_re[..., None] * bb_re[:, :, :, None] - q_im[..., None] * bb_im[:, :, :, None]
    mb_im = q_re[..., None] * bb_im[:, :, :, None] + q_im[..., None] * bb_re[:, :, :, None]
    mb = jnp.stack([mb_re, mb_im], axis=1)
    mb = jnp.stack([mb[0][..., ::-1, :], mb[1]])
    mbcomp = mb.transpose(0, 4, 5, 1, 2, 3).reshape(2, L, C, 2 * GP)
    m_b = jnp.where(gc_rows == grp(jnp.arange(2 * GP), P)[None, :],
                    jnp.einsum("rc,djcq->djrq", row_expand, mbcomp), 0.0)
    w = jnp.stack([ca_re[..., 1:], -ca_im[..., 1:]], axis=1)
    w = jnp.stack([w[0], w[1][..., ::-1]])
    wcomp = w.transpose(0, 1, 2, 4, 5, 3).reshape(2, 2 * GP, L * C)
    col = jnp.arange(L * GC)
    col_expand = jnp.logical_and((jnp.arange(L * C) // C)[:, None] == (col // GC)[None, :],
                                 (jnp.arange(L * C) % C)[:, None] == (col % C)[None, :]).astype(F32)
    m_c = jnp.where(grp(jnp.arange(2 * GP), P)[:, None] == grp(col, C)[None, :],
                    jnp.einsum("drk,kq->drq", wcomp, col_expand), 0.0)
    al = jnp.stack([pw_re[..., L].reshape(2, GP), pw_im[..., L].reshape(2, GP)], axis=1)
    al = jnp.concatenate([al, jnp.zeros((2, 6, GP), F32)], axis=1)
    return m_b.astype(BF16), m_c.astype(BF16), kbd.astype(BF16), al


def _da_prep_body(x_ref, cos_ref, sa_ref, sb_ref, q_ref, k_ref, vt_ref):
    x = x_ref[0]
    cos, sa, sb = cos_ref[...], sa_ref[...], sb_ref[...]

    def rope(z):
        return z * cos + pltpu.roll(z, 240, 1) * sa + pltpu.roll(z, 16, 1) * sb

    q_ref[0] = (rope(x[:, 0:256]) * (DA_DK ** -0.5 * LOG2E)).astype(BF16)
    k_ref[0] = rope(x[:, 256:512]).astype(BF16)
    vt = x[:, 512:768].T.astype(BF16)
    ones = jnp.ones((DA_VROWS - DA_DV, vt.shape[1]), BF16)
    vt_ref[0, 0] = jnp.concatenate(
        [blk for h in range(DA_HEADS) for blk in (vt[h * DA_DV:(h + 1) * DA_DV], ones)], axis=0)


def _da_kblock(t):
    return next(c for c in (1408, 768, 512, 256) if t % c == 0)


DA_VROWS = DA_DV + 16
LOG2E = 1.4426950408889634


def _da_prep_call(qkv, cos, sa, sb):
    b, t, _ = qkv.shape
    tm = _da_kblock(t)
    assert t % tm == 0
    tab = pl.BlockSpec((tm, 256), lambda i, j: (j, 0))
    o = pl.BlockSpec((1, tm, 256), lambda i, j: (i, j, 0))
    return pl.pallas_call(
        _da_prep_body, grid=(b, t // tm),
        in_specs=[pl.BlockSpec((1, tm, 768), lambda i, j: (i, j, 0)), tab, tab, tab],
        out_specs=[o, o, pl.BlockSpec((1, 1, DA_HEADS * DA_VROWS, tm), lambda i, j: (i, j, 0, 0))],
        out_shape=[SDS((b, t, 256), BF16)] * 2 + [SDS((b, t // tm, DA_HEADS * DA_VROWS, tm), BF16)],
        compiler_params=_cp("parallel", "parallel"), name="da_prep")(qkv, cos, sa, sb)


def _attn_body(nkb, tk, tq, post, q_ref, k_ref, vt_ref, lam_ref, g_ref, o_ref, q8_ref, acc_ref, s0_ref, p7_ref,
               a7_ref):
    q = q_ref[0]
    lane = _iota((tq, 256), 1)
    for hm in range(8):
        q8_ref[hm] = jnp.where((lane >> 5) == hm, q, jnp.zeros_like(q))
    acc_ref[...] = jnp.zeros_like(acc_ref)

    def keys(i):
        return k_ref[0, pl.ds(pl.multiple_of(i * tk, 8), tk), :]

    s0_ref[...] = _nt(keys(0), q8_ref[0])

    nsplit = 4 if tk % 32 == 0 else 1
    p7_ref[...] = jnp.zeros_like(p7_ref)
    a7_ref[...] = jnp.ones_like(a7_ref)

    def values(i, hm):
        h = hm // 2
        return vt_ref[0, i][h * DA_VROWS:(h + 1) * DA_VROWS, :tk]

    def accumulate(hm, vals, p, alpha):
        acc_ref[hm] = acc_ref[hm] * alpha + _dot(vals, p)

    def kblock(i, carry):
        m = carry
        kb = keys(i)
        kb_next = keys(jnp.minimum(i + 1, nkb - 1))
        m_rows = []
        s = s0_ref[...]
        pending = (7, values(jnp.maximum(i - 1, 0), 7), p7_ref[...], a7_ref[0:1, :])
        for hm in range(8):
            s_next = _nt(kb, q8_ref[hm + 1]) if hm < 7 else _nt(kb_next, q8_ref[0])
            accumulate(*pending)
            m_old = m[hm:hm + 1, :]
            part = s[0:tk // nsplit]
            for r in range(1, nsplit):
                part = jnp.maximum(part, s[r * tk // nsplit:(r + 1) * tk // nsplit])
            m_new = jnp.maximum(m_old, jnp.max(part, axis=0, keepdims=True))
            alpha = jnp.exp2(m_old - m_new)
            p = jnp.exp2(s - m_new).astype(BF16)
            m_rows.append(m_new)
            pending = (hm, values(i, hm), p, alpha)
            s = s_next
        s0_ref[...] = s
        p7_ref[...] = pending[2]
        a7_ref[0:1, :] = pending[3]
        return jnp.concatenate(m_rows, axis=0)

    lax.fori_loop(0, nkb, kblock, jnp.full((8, tq), -1e30, F32),
                  unroll=next(u for u in (3, 2, 1) if nkb % u == 0))
    accumulate(7, values(nkb - 1, 7), p7_ref[...], a7_ref[0:1, :])
    lam = lam_ref[0:1, 0:1]
    heads = []
    for h in range(DA_HEADS):
        a1, a2 = acc_ref[2 * h], acc_ref[2 * h + 1]
        o = a1[0:DA_DV] / a1[DA_DV:DA_DV + 1] - lam * (a2[0:DA_DV] / a2[DA_DV:DA_DV + 1])
        ms = jnp.mean(o * o, axis=0, keepdims=True)
        heads.append(o * lax.rsqrt(ms + 1e-5) * g_ref[...] * post)
    o_ref[0] = jnp.concatenate(heads, axis=0).T


def _attn_call(q, k, vt, lam_row, gain_col, post, q_off, n_q, n_keys, tq, tk):
    b = q.shape[0]
    assert n_keys % tk == 0 and n_q % tq == 0 and q_off % tq == 0
    nkb = n_keys // tk
    assert tk == vt.shape[3] or nkb == 1
    qo = q_off // tq
    return pl.pallas_call(
        functools.partial(_attn_body, nkb, tk, tq, post), grid=(b, n_q // tq),
        in_specs=[pl.BlockSpec((1, tq, 256), lambda i, j: (i, j + qo, 0)),
                  pl.BlockSpec((1, n_keys, 256), lambda i, j: (i, 0, 0)),
                  pl.BlockSpec((1, nkb, vt.shape[2], vt.shape[3]), lambda i, j: (i, 0, 0, 0)),
                  pl.BlockSpec((1, 128), lambda i, j: (0, 0)),
                  pl.BlockSpec((DA_DV, 1), lambda i, j: (0, 0))],
        out_specs=pl.BlockSpec((1, tq, 256), lambda i, j: (i, j, 0)),
        out_shape=SDS((b, n_q, 256), F32),
        scratch_shapes=[pltpu.VMEM((8, tq, 256), BF16), pltpu.VMEM((8, DA_VROWS, tq), F32),
                        pltpu.VMEM((tk, tq), F32), pltpu.VMEM((tk, tq), BF16), pltpu.VMEM((8, tq), F32)],
        compiler_params=_cp("parallel", "parallel"), name="diff_attn")(q, k, vt, lam_row, gain_col)


def _gla_body(rev, n, qkv_ref, sm_ref, wg_ref, bg_ref, o_ref, st_ref):
    c = GLA_CHUNK
    tm = n * c

    @pl.when(pl.program_id(1) == 0)
    def _():
        st_ref[...] = jnp.zeros_like(st_ref)

    x = qkv_ref[0]
    q, k, v = x[:, 0:128] * (GLA_DK ** -0.5), x[:, 128:256], x[:, 256:512]
    z = _dot(sm_ref[0], wg_ref[0], precision=HI) + bg_ref[0]
    la = jax.nn.log_sigmoid(z) * (1.0 / GLA_TAU)
    r_i, c_i = _iota((tm, tm), 0), _iota((tm, tm), 1)
    before = (c_i >= r_i) if rev else (c_i <= r_i)
    tri = jnp.logical_and((r_i >> 4) == (c_i >> 4), before)
    b = _sel_dot(tri, la)
    chunk_sel = (_iota((tm, 128), 0) >> 4) == _iota((tm, 128), 1)
    a_cols = jnp.exp(_dot_sel(la, chunk_sel, dot=_tn))
    b3, q3, k3 = b.reshape(n, c, 128), q.reshape(n, c, 128), k.reshape(n, c, 128)
    v3 = v.reshape(n, c, 256)
    bl3 = b3[:, 0:1, :] if rev else b3[:, c - 1:c, :]
    row3 = _iota((n, c, 128), 1)
    head_ones = ((_iota((128, 256), 0) >> 5) == (_iota((128, 256), 1) >> 6)).astype(BF16)
    o3 = jnp.zeros((n, c, 256), F32)
    for j in range(c):
        ok = (row3 <= j) if rev else (row3 >= j)
        e = jnp.where(ok, jnp.exp(jnp.where(ok, b3 - b3[:, j:j + 1, :], 0.0)), 0.0)
        part = (e * q3 * k3[:, j:j + 1, :]).reshape(tm, 128).astype(BF16)
        o3 = o3 + _dot(part, head_ones).reshape(n, c, 256) * v3[:, j:j + 1, :]
    q_dec = (q3 * jnp.exp(b3)).astype(BF16)
    k_dec = (k3 * jnp.exp(bl3 - b3)).astype(BF16)
    vb = v3.astype(BF16)
    st_mask = (_iota((128, 256), 0) >> 5) == (_iota((128, 256), 1) >> 6)
    order = range(n - 1, -1, -1) if rev else range(n)
    upd = {ci: jnp.where(st_mask, _tn(k_dec[ci], vb[ci]), 0.0) for ci in order}
    st = st_ref[...]
    inter = [None] * n
    for ci in order:
        inter[ci] = _dot(q_dec[ci], st.astype(BF16))
        st = st * a_cols[:, ci:ci + 1] + upd[ci]
    st_ref[...] = st
    o_ref[0] = jnp.concatenate(inter, axis=0) + o3.reshape(tm, 256)


def _gla_call(qkv, small, wg, bg, n_ctx, rev):
    b, t, _ = qkv.shape
    tm = 256
    assert t % tm == 0 and n_ctx % tm == 0 and tm // GLA_CHUNK <= 128 and GLA_CHUNK == 16
    blk = lambda i, j: _scan_order(rev, j, n_ctx // tm, t // tm)
    return pl.pallas_call(
        functools.partial(_gla_body, rev, tm // GLA_CHUNK), grid=(b, t // tm),
        in_specs=[pl.BlockSpec((1, tm, 512), lambda i, j: (i, blk(i, j), 0)),
                  pl.BlockSpec((1, tm, 128), lambda i, j: (i, blk(i, j), 0)),
                  pl.BlockSpec((1, 128, 128), lambda i, j: (0, 0, 0)),
                  pl.BlockSpec((1, 1, 128), lambda i, j: (0, 0, 0))],
        out_specs=pl.BlockSpec((1, tm, 256), lambda i, j: (i, blk(i, j), 0)),
        out_shape=SDS((b, t, 256), F32), scratch_shapes=[pltpu.VMEM((128, 256), F32)],
        compiler_params=_cp("parallel", "arbitrary"), name="gla_scan")(qkv, small, wg, bg)


def _merge_body(n_ctx, tm, nsub, a_ref, h_ref, mv_ref, dnf_ref, dnr_ref, dnz_ref, s5y_ref, s5yr_ref, s5u_ref, *rest):
    da_refs, rest = rest[:nsub], rest[nsub:]
    (dac_ref, glf_ref, glr_ref, glz_ref, vec_ref, gluw_ref, wgate_ref, bgate_ref, wbr_ref, wout_ref, n2g_ref,
     wr_ref, br_ref, h_out, f_out, wt_out) = rest
    t = pl.program_id(1)
    d = h_ref.shape[2]
    mv = mv_ref[0]
    vec = vec_ref[...]
    ones64 = _block_ones(256, 6)

    def mix(k):
        sl = slice(k * tm, (k + 1) * tm)
        z = dnz_ref[0, sl, :]
        ya = _head_rms(dnf_ref[0, sl, :] + dnr_ref[0, sl, :], vec[0:1], NORM_EPS, ones64) * (z * jax.nn.sigmoid(z))
        u = s5u_ref[0, sl, :].astype(F32)
        zz = jax.nn.gelu(s5y_ref[0, sl, :].astype(F32) + s5yr_ref[0, sl, :].astype(F32) + vec[2:3] * u)
        yb = zz * jax.nn.sigmoid(_bdot(zz, gluw_ref[...]) + vec[3:4])
        r = glz_ref[0, sl, :]
        yd = _head_rms(glf_ref[0, sl, :] + glr_ref[0, sl, :], vec[1:2], NORM_EPS, ones64) * (r * jax.nn.sigmoid(r))
        a = a_ref[0, sl, :]
        acc = jnp.zeros((tm, d), F32)
        yc = jnp.where((t * nsub + k) * tm < n_ctx, dac_ref[0], da_refs[k][0])
        for i, y in enumerate((ya, yb, yc, yd)):
            gate = jax.nn.sigmoid(_dot(a, wgate_ref[:, i * d:(i + 1) * d]) + bgate_ref[:, i * d:(i + 1) * d])
            acc = acc + gate * _dot(y.astype(BF16), wbr_ref[i])
        return _dot(acc.astype(BF16), wout_ref[...])

    def finish(k, out):
        sl = slice(k * tm, (k + 1) * tm)
        is_ctx = ((t * nsub + k) * tm + _iota((tm, 1), 0)) < n_ctx
        gate2 = jnp.where(is_ctx, mv[10:11], mv[2:3])
        h_new = h_ref[0, sl, :] + gate2 * out
        h_out[0, sl, :] = h_new
        f = _modulated_norm(h_new, n2g_ref[...], mv, is_ctx, 3, 4)
        f_out[0, sl, :] = f.astype(BF16)
        f_hi = f.astype(BF16)
        f_lo = (f - f_hi.astype(F32)).astype(BF16)
        wr = wr_ref[...]
        wr_hi = wr.astype(BF16)
        wr_lo = (wr - wr_hi.astype(F32)).astype(BF16)
        lg = _nt(jnp.concatenate([wr_hi, wr_lo], axis=0), f_hi)
        sc = jax.nn.sigmoid(lg[:N_EXPERTS] + lg[N_EXPERTS:] + _nt(wr_hi, f_lo))
        gr = sc + br_ref[...]
        rows_ = [gr[e:e + 1, :] for e in range(N_EXPERTS)]
        best_v, best_g = None, None
        for g in range(N_GROUPS):
            x = rows_[g * EPG:(g + 1) * EPG]
            top2 = None
            for i in range(EPG):
                for j in range(i + 1, EPG):
                    s_ij = x[i] + x[j]
                    top2 = s_ij if top2 is None else jnp.maximum(top2, s_ij)
            if best_v is None:
                best_v, best_g = top2, jnp.zeros_like(top2, dtype=jnp.int32)
            else:
                take = top2 > best_v
                best_v = jnp.where(take, top2, best_v)
                best_g = jnp.where(take, g, best_g)
        cand = []
        for kk in range(EPG):
            ck = rows_[kk]
            for g in range(1, N_GROUPS):
                ck = jnp.where(best_g == g, rows_[g * EPG + kk], ck)
            cand.append(ck)
        v1, i1 = cand[0], jnp.zeros_like(best_g)
        for kk in range(1, EPG):
            take = cand[kk] > v1
            v1 = jnp.where(take, cand[kk], v1)
            i1 = jnp.where(take, kk, i1)
        v2, i2 = jnp.full_like(v1, -jnp.inf), jnp.zeros_like(best_g)
        for kk in range(EPG):
            take = jnp.logical_and(i1 != kk, cand[kk] > v2)
            v2 = jnp.where(take, cand[kk], v2)
            i2 = jnp.where(take, kk, i2)
        e1, e2 = best_g * EPG + i1, best_g * EPG + i2
        e_iota = _iota((N_EXPERTS, tm), 0)
        s1 = jnp.sum(jnp.where(e_iota == e1, sc, 0.0), axis=0, keepdims=True)
        s2 = jnp.sum(jnp.where(e_iota == e2, sc, 0.0), axis=0, keepdims=True)
        tot = s1 + s2
        wt_out[0, :, sl] = jnp.where(e_iota == e1, s1 / tot, jnp.where(e_iota == e2, s2 / tot, 0.0))

    outs = [mix(k) for k in range(nsub)]
    for k in range(nsub):
        finish(k, outs[k])


def _merge_call(a, h, modv, dn_f, dn_b, dn_z, s5_y, s5_u, da_lat, da_ctx, gl_f, gl_b, gl_r, vec, glu_w,
                w_gate, b_gate, w_branch, w_out, n2g, w_rt, b_rt, n_ctx):
    b, t, d = h.shape
    tm = 256
    nsub = next(c for c in (3, 2, 1) if (t // tm) % c == 0)
    ts = nsub * tm
    assert t % tm == 0 and n_ctx % tm == 0
    tok = lambda w: pl.BlockSpec((1, ts, w), lambda i, j: (i, j, 0))
    fwd = pl.BlockSpec((None, 1, ts, 256), lambda i, j: (0, i, j, 0))
    bwd = pl.BlockSpec((None, 1, ts, 256), lambda i, j: (1, i, j, 0))
    full = lambda *shape: pl.BlockSpec(shape, lambda i, j: (0,) * len(shape), pipeline_mode=pl.Buffered(1))
    da_specs = [pl.BlockSpec((1, tm, 256), (lambda i, j, k=k: (i, jnp.maximum(nsub * j + k - n_ctx // tm, 0), 0)))
                for k in range(nsub)]
    return pl.pallas_call(
        functools.partial(_merge_body, n_ctx, tm, nsub), grid=(b, t // ts),
        in_specs=[tok(d), tok(d), pl.BlockSpec((1, 16, d), lambda i, j: (i, 0, 0)),
                  tok(256), tok(256), tok(256), fwd, bwd, tok(256)] + da_specs + [
                  pl.BlockSpec((1, tm, 256), lambda i, j: (i, 0, 0)), tok(256), tok(256),
                  tok(256), full(8, 256), full(256, 256), full(d, 4 * d), full(1, 4 * d),
                  full(4, 256, d), full(d, d), full(1, d), full(N_EXPERTS, d), full(N_EXPERTS, 1)],
        out_specs=[tok(d), tok(d), pl.BlockSpec((1, N_EXPERTS, ts), lambda i, j: (i, 0, j))],
        out_shape=[SDS((b, t, d), F32), SDS((b, t, d), BF16), SDS((b, N_EXPERTS, t), F32)],
        compiler_params=_cp("parallel", "parallel", vmem=MERGE_VMEM_LIMIT), name="merge_router")(
            a, h, modv, dn_f, dn_b, dn_z, s5_y, s5_y, s5_u, *([da_lat] * nsub), da_ctx, gl_f, gl_b, gl_r, vec, glu_w,
            w_gate, b_gate, w_branch, w_out, n2g, w_rt, b_rt)


def _moe_body(n_ctx, tt, final, f_ref, wt_ref, h_ref, mv_ref, wg_ref, wu_ref, wd_ref, fg_ref, o_ref,
              acc_ref, rank_t_ref, rank_c_ref):
    e = pl.program_id(2)
    cap = MOE_CAP

    @pl.when(e == 0)
    def _():
        acc_ref[...] = jnp.zeros_like(acc_ref)
        routed = wt_ref[0] != 0.0
        earlier = (_iota((tt, tt), 0) < _iota((tt, tt), 1)).astype(BF16)
        rank = _dot(routed.astype(BF16), earlier)
        rank = jnp.where(routed, rank, -1.0)
        rank_t_ref[...] = rank
        eye = _iota((N_EXPERTS, N_EXPERTS), 0) == _iota((N_EXPERTS, N_EXPERTS), 1)
        rank_c_ref[...] = _dot_sel(rank, eye, dot=_tn)

    x = f_ref[0]
    rank_row = rank_t_ref[pl.ds(e, 1), :]
    w_row = wt_ref[0, pl.ds(e, 1), :]
    rc = rank_c_ref[...]
    rank_col = jnp.sum(jnp.where(_iota(rc.shape, 1) == e, rc, 0.0), axis=1, keepdims=True)
    count = jnp.sum((rank_row >= 0.0).astype(F32))

    for blk in range(-(-tt // cap)):
        @pl.when(count > blk * cap)
        def _(blk=blk):
            sel = rank_row == (_iota((cap, tt), 0) + blk * cap).astype(F32)
            xg = _dot(sel.astype(BF16), x).astype(BF16)
            w_g = jnp.sum(jnp.where(sel, w_row, 0.0), axis=1, keepdims=True)
            g = _dot(xg, wg_ref[0])
            hid = g * jax.nn.sigmoid(g) * _dot(xg, wu_ref[0]) * w_g
            y = _dot(hid.astype(BF16), wd_ref[0]).astype(BF16)
            back = rank_col == (_iota((tt, cap), 1) + blk * cap).astype(F32)
            acc_ref[...] += _dot(back.astype(BF16), y)

    @pl.when(e == N_EXPERTS - 1)
    def _():
        row = pl.program_id(1) * tt + _iota((tt, 1), 0)
        mv = mv_ref[0]
        gate = jnp.where(row < n_ctx, mv[13:14], mv[5:6])
        hn = h_ref[0] + gate * acc_ref[...]
        if final:
            hn = hn * lax.rsqrt(jnp.mean(hn * hn, axis=-1, keepdims=True) + NORM_EPS) * fg_ref[...]
        o_ref[0] = hn


def _moe_call(f, wc, h, modv, weg, weu, wed, final_g, n_ctx, final):
    b, t, d = h.shape
    fe = weg.shape[2]
    tt = next(c for c in (1408, 1024, 768, 512, 256, 128, 64, 32, 16, 8) if t % c == 0)
    tok = lambda w: pl.BlockSpec((1, tt, w), lambda i, j, e: (i, j, 0))
    return pl.pallas_call(
        functools.partial(_moe_body, n_ctx, tt, final), grid=(b, t // tt, N_EXPERTS),
        in_specs=[tok(d), pl.BlockSpec((1, N_EXPERTS, tt), lambda i, j, e: (i, 0, j)), tok(d),
                  pl.BlockSpec((1, 16, d), lambda i, j, e: (i, 0, 0)),
                  pl.BlockSpec((1, d, fe), lambda i, j, e: (e, 0, 0)),
                  pl.BlockSpec((1, d, fe), lambda i, j, e: (e, 0, 0)),
                  pl.BlockSpec((1, fe, d), lambda i, j, e: (e, 0, 0)),
                  pl.BlockSpec((1, d), lambda i, j, e: (0, 0))],
        out_specs=tok(d), out_shape=SDS((b, t, d), F32),
        scratch_shapes=[pltpu.VMEM((tt, d), F32), pltpu.VMEM((N_EXPERTS, tt), F32), pltpu.VMEM((tt, N_EXPERTS), F32)],
        compiler_params=_cp("parallel", "parallel", "arbitrary"), name="moe_experts")(
            f, wc, h, modv, weg, weu, wed, final_g)


def _rope_tables(n_ctx, s):
    t = jnp.arange(s)
    rowp, colp = (t // GRID_W).astype(F32), (t % GRID_W).astype(F32)
    n_freq = DA_DK // 4
    inv = ROPE_BASE ** (-jnp.arange(n_freq, dtype=F32) / n_freq)
    ang = jnp.concatenate([rowp[:, None] * inv, colp[:, None] * inv], axis=-1)
    cos, sin = jnp.cos(ang), jnp.sin(ang)
    zero = jnp.zeros_like(sin)
    cos32 = jnp.concatenate([cos, cos], axis=-1)
    sa32 = jnp.concatenate([-sin, zero], axis=-1)
    sb32 = jnp.concatenate([zero, sin], axis=-1)
    def full(tab, ctx_val):
        tab = jnp.tile(tab, (1, 8))
        return jnp.concatenate([jnp.full((n_ctx, 256), ctx_val, F32), tab], axis=0)
    return full(cos32, 1.0), full(sa32, 0.0), full(sb32, 0.0)


def _perm_w_in(w_in):
    offs, o = {}, 0
    for name, width in (("dn_qkv", 768), ("dn_z", 256), ("dn_a", 8), ("dn_b", 8), ("s5_u", 256), ("da_q", 256),
                        ("da_k", 256), ("da_v", 256), ("gla_q", 128), ("gla_k", 128), ("gla_v", 256),
                        ("gla_r", 256), ("gla_gate", 32)):
        offs[name] = (o, width)
        o += width
    col = lambda n: w_in[:, offs[n][0]:offs[n][0] + offs[n][1]]
    pad = jnp.zeros((w_in.shape[0], 128 - 48), w_in.dtype)
    return jnp.concatenate([col(n) for n in ("dn_qkv", "dn_z", "s5_u", "da_q", "da_k", "da_v", "gla_q", "gla_k",
                                              "gla_v", "gla_r", "dn_a", "dn_b", "gla_gate")] + [pad], axis=1)


def kernel(x, c, ctx, c_ctx, w_mod, b_mod, norm1_g, norm2_g, w_in, dn_conv, dn_a_log, dn_dt_bias, dn_norm_g,
           s5_a_re, s5_a_im, s5_log_dt, s5_b_re, s5_b_im, s5_c_re, s5_c_im, s5_d, s5_glu_w, s5_glu_b,
           da_lambda, da_norm_g, gla_w_gate, gla_b_gate, gla_norm_g, w_branch, w_gate, b_gate, w_out,
           w_router, b_router, w_e_gate, w_e_up, w_e_down, final_g):
    b, s, d = x.shape
    n_ctx = ctx.shape[1]
    t = n_ctx + s
    depth = w_mod.shape[0]
    assert b + 1 <= 8 and b == 4

    cvec = jnp.zeros((8, d), F32).at[:b].set(c).at[b].set(c_ctx)
    mod_all = _mod_call(cvec, w_mod, b_mod).reshape(depth, 8, 6, d)
    cos_t, sa_t, sb_t = _rope_tables(n_ctx, s)
    h = jnp.concatenate([ctx, x], axis=1)

    lane256 = jnp.arange(256) // 64
    eg = jnp.stack([(jnp.arange(128)[:, None] == (SM_DN_A + dd * 4 + lane256)[None, :]).astype(F32) for dd in range(2)])
    eb = jnp.stack([(jnp.arange(128)[:, None] == (SM_DN_B + dd * 4 + lane256)[None, :]).astype(F32) for dd in range(2)])

    for layer in range(depth):
        lam_init = 0.8 - 0.6 * math.exp(-0.3 * layer)
        ml = mod_all[layer]
        modv = jnp.zeros((b, 16, d), F32).at[:, 0:6].set(ml[:b]).at[:, 8:14].set(jnp.broadcast_to(ml[b], (b, 6, d)))

        a, dn_qkv, dn_z, s5_u, da_qkv, gla_qkv, gla_r, small = _proj_call(
            h, modv, norm1_g[layer][None, :], _perm_w_in(w_in[layer]).astype(BF16), n_ctx)

        dq, dk_, dv, gb = _dn_prep_call(dn_qkv, small, dn_conv[layer], dn_a_log[layer], dn_dt_bias[layer], n_ctx)
        dn_f, dn_b = [_dn_scan_call(dq, dk_, dv, gb, eg[dd:dd + 1], eb[dd:dd + 1], n_ctx, dd == 1)
                      for dd in range(2)]

        m_b, m_c, kbd, al = _s5_operators(
            s5_a_re[layer], s5_a_im[layer], s5_log_dt[layer], s5_b_re[layer], s5_b_im[layer],
            s5_c_re[layer], s5_c_im[layer])
        s5_y = _s5_call(s5_u, m_b, m_c, kbd, al, n_ctx)

        lf = da_lambda[layer]
        lam = jnp.exp(jnp.sum(lf[0] * lf[1])) - jnp.exp(jnp.sum(lf[2] * lf[3])) + lam_init
        lam_row = jnp.full((1, 128), lam, F32)
        da_gain = da_norm_g[layer][:, None]
        aq, ak, avt = _da_prep_call(da_qkv, cos_t, sa_t, sb_t)
        da_lat = _attn_call(aq, ak, avt, lam_row, da_gain, 1.0 - lam_init, n_ctx, s, t, 256, _da_kblock(t))
        if layer < depth - 1:
            da_ctx = _attn_call(aq, ak, avt, lam_row, da_gain, 1.0 - lam_init, 0, n_ctx, n_ctx, 256, n_ctx)
        else:
            da_ctx = da_lat

        wg2 = jnp.zeros((2, 128, 128), F32)
        for dd in range(2):
            wg2 = wg2.at[dd, SM_GLA + dd * GLA_RANK:SM_GLA + (dd + 1) * GLA_RANK].set(gla_w_gate[layer, dd])
        gl_f, gl_b = [_gla_call(gla_qkv, small, wg2[dd:dd + 1], gla_b_gate[layer][dd:dd + 1, None, :], n_ctx, dd == 1)
                      for dd in range(2)]

        vec = jnp.zeros((8, 256), F32)
        vec = vec.at[0].set(jnp.tile(dn_norm_g[layer], DN_HEADS)).at[1].set(jnp.tile(gla_norm_g[layer], GLA_HEADS))
        vec = vec.at[2].set(s5_d[layer]).at[3].set(s5_glu_b[layer])
        h, f, wc = _merge_call(
            a, h, modv, dn_f, dn_b, dn_z, s5_y, s5_u, da_lat, da_ctx, gl_f, gl_b, gla_r, vec,
            s5_glu_w[layer].astype(BF16), w_gate[layer].astype(BF16), b_gate[layer][None, :],
            w_branch[layer].astype(BF16), w_out[layer].astype(BF16), norm2_g[layer][None, :],
            w_router.T, b_router[:, None], n_ctx)

        h = _moe_call(f, wc, h, modv, w_e_gate[layer].astype(BF16), w_e_up[layer].astype(BF16),
                      w_e_down[layer].astype(BF16), final_g[None, :], n_ctx, layer == depth - 1)
    return h[:, n_ctx:]
```

```python
import functools
import math

import jax
import jax.numpy as jnp
from jax import lax
from jax.experimental import pallas as pl
from jax.experimental.pallas import tpu as pltpu

F32 = jnp.float32
BF16 = jnp.bfloat16
HI = lax.Precision.HIGHEST
SDS = jax.ShapeDtypeStruct

GRID_W = 64
DN_HEADS, DN_DK, DN_DV, DN_CONV, DN_CHUNK = 4, 64, 64, 5, 64
DN_BLOCK = 256
DN_STREAMS = 4
S5_WIDTH, S5_GROUP, S5_STATE = 256, 16, 64
S5_GROUPS = S5_WIDTH // S5_GROUP
S5_CHUNK = 16
DA_HEADS, DA_DK, DA_DV = 4, 32, 64
ROPE_BASE = 10000.0
GLA_HEADS, GLA_DK, GLA_DV, GLA_RANK, GLA_TAU, GLA_CHUNK = 4, 32, 64, 16, 16.0, 16
N_EXPERTS, N_GROUPS, TOP_K = 16, 4, 2
MOE_CAP = 208
EPG = N_EXPERTS // N_GROUPS
NORM_EPS = 1e-6
VMEM_LIMIT = 56 * 1024 * 1024
MERGE_VMEM_LIMIT = 60 * 1024 * 1024

SEG = dict(dn_qkv=(0, 768), dn_z=(768, 256), s5_u=(1024, 256), da_qkv=(1280, 768),
           gla_qkv=(2048, 512), gla_r=(2560, 256), small=(2816, 128))
D_PROJ = 2944
SM_DN_A, SM_DN_B, SM_GLA = 0, 8, 16


def _cp(*sem, vmem=VMEM_LIMIT):
    return pltpu.CompilerParams(dimension_semantics=sem, vmem_limit_bytes=vmem)


def _nt(a, b, **kw):
    return lax.dot_general(a, b, (((1,), (1,)), ((), ())), preferred_element_type=F32, **kw)


def _tn(a, b, **kw):
    return lax.dot_general(a, b, (((0,), (0,)), ((), ())), preferred_element_type=F32, **kw)


def _dot(a, b, **kw):
    return jnp.dot(a, b, preferred_element_type=F32, **kw)


def _bdot(a, b):
    return jnp.dot(a.astype(BF16), b.astype(BF16), preferred_element_type=F32)


def _split3(x):
    hi = x.astype(BF16)
    r = x - hi.astype(F32)
    mid = r.astype(BF16)
    return hi, mid, (r - mid.astype(F32)).astype(BF16)


def _dot_sel(x, sel, dot=None):
    dot = dot or _dot
    sel = sel.astype(BF16)
    hi, mid, lo = _split3(x)
    return dot(hi, sel) + dot(mid, sel) + dot(lo, sel)


def _sel_dot(sel, x):
    sel = sel.astype(BF16)
    hi, mid, lo = _split3(x)
    return _dot(sel, hi) + _dot(sel, mid) + _dot(sel, lo)


def _iota(shape, dim):
    return lax.broadcasted_iota(jnp.int32, shape, dim)


def _block_ones(n, blk_shift):
    r, c = _iota((n, n), 0), _iota((n, n), 1)
    return ((r >> blk_shift) == (c >> blk_shift)).astype(F32)


def _head_rms(o, gain, eps, ones64):
    ms = _dot_sel(o * o, ones64) * (1.0 / 64.0)
    return o * lax.rsqrt(ms + eps) * gain


def _mod_body(c_ref, w_ref, b_ref, o_ref):
    c = c_ref[...]
    s = c * jax.nn.sigmoid(c)
    o_ref[0] = _dot(s, w_ref[0], precision=HI) + b_ref[0]


def _mod_call(cvec, w_mod, b_mod):
    n_layer, d, d6 = w_mod.shape
    tn = 1536
    return pl.pallas_call(
        _mod_body, grid=(n_layer, d6 // tn),
        in_specs=[pl.BlockSpec((8, d), lambda l, j: (0, 0)),
                  pl.BlockSpec((1, d, tn), lambda l, j: (l, 0, j)),
                  pl.BlockSpec((1, 1, tn), lambda l, j: (l, 0, j))],
        out_specs=pl.BlockSpec((1, 8, tn), lambda l, j: (l, 0, j)),
        out_shape=SDS((n_layer, 8, d6), F32), compiler_params=_cp("parallel", "parallel"),
        name="adaln_mod")(cvec, w_mod, b_mod.reshape(n_layer, 1, d6))


def _modulated_norm(x, gain, mv, is_ctx, k_shift, k_scale):
    y = x * lax.rsqrt(jnp.mean(x * x, axis=-1, keepdims=True) + NORM_EPS) * gain
    shift = jnp.where(is_ctx, mv[8 + k_shift:9 + k_shift], mv[k_shift:k_shift + 1])
    scale = jnp.where(is_ctx, mv[8 + k_scale:9 + k_scale], mv[k_scale:k_scale + 1])
    return y * (1.0 + scale) + shift


def _proj_body(n_ctx, tm, h_ref, mv_ref, g_ref, w_ref, a_ref, *out_refs):
    t = pl.program_id(1)
    row = t * tm + _iota((tm, 1), 0)
    a = _modulated_norm(h_ref[0], g_ref[...], mv_ref[0], row < n_ctx, 0, 1)
    ab = a.astype(BF16)
    a_ref[0] = ab
    p = _dot(ab, w_ref[...])
    for ref, (off, width) in zip(out_refs, SEG.values()):
        ref[0] = p[:, off:off + width].astype(ref.dtype)


def _proj_call(h, modv, gain, w_perm, n_ctx):
    b, t, d = h.shape
    tm = 384
    assert t % tm == 0
    outs = [SDS((b, t, d), BF16)] + [SDS((b, t, w), BF16 if name == "s5_u" else F32)
                                     for name, (_, w) in SEG.items()]
    out_specs = [pl.BlockSpec((1, tm, d), lambda i, j: (i, j, 0))] + [
        pl.BlockSpec((1, tm, w), lambda i, j: (i, j, 0)) for _, w in SEG.values()]
    return pl.pallas_call(
        functools.partial(_proj_body, n_ctx, tm), grid=(b, t // tm),
        in_specs=[pl.BlockSpec((1, tm, d), lambda i, j: (i, j, 0)),
                  pl.BlockSpec((1, 16, d), lambda i, j: (i, 0, 0)),
                  pl.BlockSpec((1, d), lambda i, j: (0, 0)),
                  pl.BlockSpec((d, D_PROJ), lambda i, j: (0, 0))],
        out_specs=out_specs, out_shape=outs, compiler_params=_cp("parallel", "parallel"),
        name="norm_proj")(h, modv, gain, w_perm)


def _dn_prep_body(nct, ntile, tm, x_ref, pv_ref, nx_ref, sm_ref, w_ref, alog_ref, dtb_ref,
                  q_ref, k_ref, v_ref, gb_ref):
    t = pl.program_id(1)
    first = jnp.logical_or(t == 0, t == nct)
    last = jnp.logical_or(t == nct - 1, t == ntile - 1)
    x = x_ref[0]
    pv = jnp.where(first, 0.0, pv_ref[0])
    nx = jnp.where(last, 0.0, nx_ref[0])
    w = w_ref[...]
    row = _iota((tm, 1), 0)
    xm1 = jnp.where(row == 0, pv[7:8], pltpu.roll(x, 1, 0))
    xm2 = jnp.where(row == 0, pv[6:7], jnp.where(row == 1, pv[7:8], pltpu.roll(x, 2, 0)))
    xp1 = jnp.where(row == tm - 1, nx[0:1], pltpu.roll(x, tm - 1, 0))
    xp2 = jnp.where(row == tm - 2, nx[0:1], jnp.where(row == tm - 1, nx[1:2], pltpu.roll(x, tm - 2, 0)))
    acc = w[0:1] * xm2 + w[1:2] * xm1 + w[2:3] * x + w[3:4] * xp1 + w[4:5] * xp2
    y = acc * jax.nn.sigmoid(acc)
    ones64 = _block_ones(256, 6)
    q, k = y[:, 0:256], y[:, 256:512]
    qs = _dot_sel(q * q, ones64)
    ks = _dot_sel(k * k, ones64)
    q_ref[0] = q * lax.rsqrt(qs + 1e-6) * (DN_DK ** -0.5)
    k_ref[0] = k * lax.rsqrt(ks + 1e-6)
    v_ref[0] = y[:, 512:768]
    sm = sm_ref[0]
    lane = _iota((1, 128), 1)
    g = -jnp.exp(alog_ref[...]) * jax.nn.softplus(sm + dtb_ref[...])
    be = jax.nn.sigmoid(sm)
    gb_ref[0] = jnp.where(lane < 8, g, jnp.where(lane < 16, be, 0.0))


def _dn_prep_call(qkv, small, conv_w, a_log, dt_bias, n_ctx):
    b, t, c = qkv.shape
    tm = 256
    assert n_ctx % tm == 0 and t % tm == 0
    nct, ntile, r8 = n_ctx // tm, t // tm, tm // 8
    w8 = jnp.zeros((8, c), F32).at[:DN_CONV].set(conv_w)
    alog = jnp.zeros((1, 128), F32).at[0, :8].set(a_log.reshape(-1))
    dtb = jnp.zeros((1, 128), F32).at[0, :8].set(dt_bias.reshape(-1))
    full = lambda shape: pl.BlockSpec(shape, lambda i, j: (0,) * len(shape))
    o256 = pl.BlockSpec((1, tm, 256), lambda i, j: (i, j, 0))
    return pl.pallas_call(
        functools.partial(_dn_prep_body, nct, ntile, tm), grid=(b, ntile),
        in_specs=[pl.BlockSpec((1, tm, c), lambda i, j: (i, j, 0)),
                  pl.BlockSpec((1, 8, c), lambda i, j: (i, jnp.maximum(j * r8 - 1, 0), 0)),
                  pl.BlockSpec((1, 8, c), lambda i, j: (i, jnp.minimum((j + 1) * r8, t // 8 - 1), 0)),
                  pl.BlockSpec((1, tm, 128), lambda i, j: (i, j, 0)),
                  full((8, c)), full((1, 128)), full((1, 128))],
        out_specs=[o256, o256, o256, pl.BlockSpec((1, tm, 128), lambda i, j: (i, j, 0))],
        out_shape=[SDS((b, t, 256), F32)] * 3 + [SDS((b, t, 128), F32)],
        compiler_params=_cp("parallel", "parallel"), name="dn_prep")(qkv, qkv, qkv, small, w8, alog, dtb)


def _scan_order(rev, step, n_first, n_total):
    back = jnp.where(step < n_first, n_first - 1 - step, n_total - 1 - (step - n_first))
    return jnp.where(rev, back, step)


def _dn_scan_body(rev, nst, nch, q_ref, k_ref, v_ref, gb_ref, eg_ref, eb_ref, o_ref, s_ref):
    c = DN_CHUNK
    tm = nst * nch * c
    chunks = range(nst * nch)
    rows = lambda x, ci: x[ci * c:(ci + 1) * c]

    @pl.when(pl.program_id(1) == 0)
    def _():
        s_ref[...] = jnp.zeros_like(s_ref)

    q, k, v, gb = [r[...].reshape(tm, r.shape[2]) for r in (q_ref, k_ref, v_ref, gb_ref)]
    gexp = _dot_sel(gb, eg_ref[0])
    bexp = _dot_sel(gb, eb_ref[0])
    r_i, c_i = _iota((c, c), 0), _iota((c, c), 1)
    tri = (c_i >= r_i) if rev else (c_i <= r_i)
    gc = jnp.concatenate([_sel_dot(tri, rows(gexp, ci)) for ci in chunks], axis=0)
    egc = jnp.exp(gc)
    kb = k * bexp
    rhs_u, rhs_w = v * bexp, kb * egc
    rr, cc = _iota((256, 256), 0), _iota((256, 256), 1)
    same_head = (rr >> 6) == (cc >> 6)
    i_idx = _iota((c, 256), 0)
    j_idx = _iota((c, 256), 1) & 63
    lane_h = _iota((c, 256), 1) >> 6
    d_ij = (j_idx - i_idx) if rev else (i_idx - j_idx)
    incl, strict = d_ij >= 0, d_ij > 0
    eye = (i_idx == j_idx).astype(F32)

    def stack4(x):
        return jnp.concatenate([jnp.where(lane_h == h, x, 0.0) for h in range(DN_HEADS)], axis=0).astype(BF16)

    def to_bd(xcat):
        return jnp.where(same_head, jnp.concatenate([xcat] * DN_HEADS, axis=0), 0.0).astype(BF16)

    both = lambda x, y: jnp.concatenate([x, y], axis=0).astype(BF16)
    gcc = [rows(gc, ci) for ci in chunks]
    gl = [g[0:1, :] if rev else g[c - 1:c, :] for g in gcc]
    gcr = [jnp.sum(jnp.where(d_ij <= 0, rows(gexp, ci), 0.0), axis=0, keepdims=True) for ci in chunks]
    decay = [jnp.where(incl, jnp.exp(jnp.where(incl, gcc[ci] - gcr[ci], 0.0)), 0.0) for ci in chunks]
    k4 = [stack4(rows(k, ci)) for ci in chunks]
    kq = [_nt(both(rows(kb, ci), rows(q, ci)), k4[ci]) for ci in chunks]
    lower = [jnp.where(strict, kq[ci][:c] * decay[ci], 0.0) for ci in chunks]
    a_intra = [jnp.where(incl, kq[ci][c:] * decay[ci], 0.0).astype(BF16) for ci in chunks]
    p = [eye - lower[ci] for ci in chunks]
    m = [_dot(lower[ci].astype(BF16), to_bd(lower[ci])) for ci in chunks]
    for it in range(5):
        mb = [to_bd(m[ci]) for ci in chunks]
        if it < 4:
            pm = [_dot(both(p[ci], m[ci]), mb[ci]) for ci in chunks]
            p = [p[ci] + pm[ci][:c] for ci in chunks]
            m = [pm[ci][c:] for ci in chunks]
        else:
            p = [p[ci] + _dot(p[ci].astype(BF16), mb[ci]) for ci in chunks]
    uw = [_dot(p[ci].astype(BF16), jnp.concatenate([stack4(rows(rhs_u, ci)), stack4(rows(rhs_w, ci))], axis=1))
          for ci in chunks]
    wq = [both(uw[ci][:, 256:], rows(q, ci) * rows(egc, ci)) for ci in chunks]
    k_dec = [(rows(k, ci) * jnp.exp(gl[ci] - gcc[ci])).astype(BF16) for ci in chunks]
    s = [s_ref[st] for st in range(nst)]
    outs = [None] * (nst * nch)
    for step in range(nch):
        for st in range(nst):
            ci = st * nch + (nch - 1 - step if rev else step)
            r = _dot(wq[ci], s[st].astype(BF16))
            v_new = uw[ci][:, :256] - r[:c]
            outs[ci] = r[c:] + _dot(a_intra[ci], stack4(v_new))
            s[st] = s[st] * jnp.exp(gl[ci]) + jnp.where(same_head, _tn(k_dec[ci], v_new.astype(BF16)), 0.0)
    for st in range(nst):
        s_ref[st] = s[st]
    o_ref[...] = jnp.concatenate(outs, axis=0).reshape(nst, nch * c, 256)


def _dn_scan_call(q, k, v, gb, eg, eb, n_ctx, rev):
    b, t, _ = q.shape
    tm = DN_BLOCK
    assert t % tm == 0 and n_ctx % tm == 0 and tm % DN_CHUNK == 0
    blk_of = lambda j: _scan_order(rev, j, n_ctx // tm, t // tm)
    nst = DN_STREAMS if b % DN_STREAMS == 0 else 1
    blk = lambda w: pl.BlockSpec((nst, tm, w), lambda i, j: (i, blk_of(j), 0))
    return pl.pallas_call(
        functools.partial(_dn_scan_body, rev, nst, tm // DN_CHUNK), grid=(b // nst, t // tm),
        in_specs=[blk(256), blk(256), blk(256), blk(128),
                  pl.BlockSpec((1, 128, 256), lambda i, j: (0, 0, 0)),
                  pl.BlockSpec((1, 128, 256), lambda i, j: (0, 0, 0))],
        out_specs=blk(256), out_shape=SDS((b, t, 256), F32),
        scratch_shapes=[pltpu.VMEM((nst, 256, 256), F32)],
        compiler_params=_cp("parallel", "arbitrary"), name="dn_scan")(q, k, v, gb, eg, eb)


def _s5_in_body(u_ref, mb_ref, o_ref):
    w = mb_ref.shape[2]
    acc = None
    for j in range(S5_CHUNK):
        part = _dot(u_ref[:, j * w:(j + 1) * w].astype(BF16), mb_ref[0, j])
        acc = part if acc is None else acc + part
    o_ref[0] = acc


def _s5_scan_body(n_sub, n_sub_ctx, bc_ref, al_ref, x_ref):
    rev = pl.program_id(0) == 1
    half = bc_ref.shape[1] // 2
    ar, ai = al_ref[0:1, :], al_ref[1:2, :]

    def sub(s, st):
        xr, xi = st
        rows = pl.ds(pl.multiple_of(_scan_order(rev, s, n_sub_ctx, n_sub) * 8, 8), 8)
        blk = bc_ref[rows, :]
        seen = []
        for kk in range(8):
            seen.append(jnp.concatenate([xr, xi], axis=1))
            inp = jnp.where(rev, blk[7 - kk:8 - kk], blk[kk:kk + 1])
            xr, xi = xr * ar - xi * ai + inp[:, :half], xi * ar + xr * ai + inp[:, half:]
        x_ref[rows, :] = jnp.concatenate([jnp.where(rev, seen[7 - r], seen[r]) for r in range(8)], axis=0)
        return xr, xi

    zero = jnp.zeros((1, half), F32)
    lax.fori_loop(0, n_sub, sub, (zero, zero))


def _s5_out_body(u_ref, x_ref, mc_ref, kbd_ref, y_ref, lhs_ref, toep_ref):
    lanes, w = x_ref.shape[2], kbd_ref.shape[2]
    rev = pl.program_id(0) == 1
    jo = pl.program_id(2)

    @pl.when(jo == 0)
    def _():
        lhs_ref[:, :lanes] = x_ref[0].astype(BF16)
        lhs_ref[:, lanes:] = u_ref[...].astype(BF16)

    for j in range(S5_CHUNK):
        feeds = jnp.where(rev, j >= jo, j <= jo)
        blk = kbd_ref[0, jnp.abs(jo - j)]
        toep_ref[j * w:(j + 1) * w, :] = jnp.where(feeds, blk, jnp.zeros_like(blk))
    y = _dot(lhs_ref[:, :lanes], mc_ref[0]) + _dot(lhs_ref[:, lanes:], toep_ref[...])
    y_ref[...] = y.astype(y_ref.dtype)


def _s5_call(u, m_b, m_c, kbd, al, n_ctx):
    b, t, w = u.shape
    L = S5_CHUNK
    n = t // L
    r = b * n
    tr = n
    lanes = m_b.shape[-1]
    u3 = u.reshape(r, L * w)
    ublk = pl.BlockSpec((tr, L * w), lambda d, i, k: (i, 0))
    nt = 512
    bc = pl.pallas_call(
        _s5_in_body, grid=(2, r // tr, lanes // nt),
        in_specs=[ublk, pl.BlockSpec((1, L, w, nt), lambda d, i, k: (d, 0, 0, k))],
        out_specs=pl.BlockSpec((1, tr, nt), lambda d, i, k: (d, i, k)),
        out_shape=SDS((2, r, lanes), F32), compiler_params=_cp("parallel", "parallel", "parallel"),
        name="s5_in")(u3, m_b)
    assert n % 8 == 0 and (n_ctx // L) % 8 == 0
    xprev = pl.pallas_call(
        functools.partial(_s5_scan_body, n // 8, n_ctx // L // 8), grid=(2, b),
        in_specs=[pl.BlockSpec((None, None, n, lanes), lambda d, i: (d, i, 0, 0)),
                  pl.BlockSpec((None, 8, lanes // 2), lambda d, i: (d, 0, 0))],
        out_specs=pl.BlockSpec((None, None, n, lanes), lambda d, i: (d, i, 0, 0)),
        out_shape=SDS((2, b, n, lanes), F32), compiler_params=_cp("parallel", "parallel"),
        name="s5_scan")(bc.reshape(2, b, n, lanes), al)
    y = pl.pallas_call(
        _s5_out_body, grid=(2, r // tr, L),
        in_specs=[ublk, pl.BlockSpec((1, tr, lanes), lambda d, i, k: (d, i, 0)),
                  pl.BlockSpec((1, lanes, w), lambda d, i, k: (d, 0, k)),
                  pl.BlockSpec((1, L, w, w), lambda d, i, k: (d, 0, 0, 0))],
        out_specs=pl.BlockSpec((None, tr, w), lambda d, i, k: (d, i, k)),
        out_shape=SDS((2, r, L * w), BF16),
        scratch_shapes=[pltpu.VMEM((tr, lanes + L * w), BF16), pltpu.VMEM((L * w, w), BF16)],
        compiler_params=_cp("parallel", "parallel", "arbitrary"), name="s5_out")(
            u3, xprev.reshape(2, r, lanes), m_c, kbd)
    return y.reshape(2, b, t, w)


def _s5_operators(a_re, a_im, log_dt, b_re, b_im, c_re, c_im):
    L = S5_CHUNK
    G, P, C = S5_GROUPS, S5_STATE, S5_GROUP
    dt = jnp.exp(log_dt)[:, :, None]
    ls = jnp.arange(L + 1, dtype=F32)
    mag = jnp.exp(a_re[..., None] * dt[..., None] * ls)
    ang = a_im[..., None] * dt[..., None] * ls
    pw_re, pw_im = mag * jnp.cos(ang), mag * jnp.sin(ang)
    ab_re, ab_im = pw_re[..., 1], pw_im[..., 1]
    den = a_re * a_re + a_im * a_im
    f_re = ((ab_re - 1.0) * a_re + ab_im * a_im) / den
    f_im = (ab_im * a_re - (ab_re - 1.0) * a_im) / den
    bb_re = f_re[..., None] * b_re[None] - f_im[..., None] * b_im[None]
    bb_im = f_re[..., None] * b_im[None] + f_im[..., None] * b_re[None]
    ca_re = c_re[..., None] * pw_re[:, :, None] - c_im[..., None] * pw_im[:, :, None]
    ca_im = c_re[..., None] * pw_im[:, :, None] + c_im[..., None] * pw_re[:, :, None]
    kern = (jnp.einsum("dgopl,dgpc->dglco", ca_re, bb_re, precision=HI)
            - jnp.einsum("dgopl,dgpc->dglco", ca_im, bb_im, precision=HI))
    GC, GP = G * C, G * P
    grp = lambda idx, per: (idx // per) % G
    row_expand = jnp.tile(jnp.eye(C, dtype=F32), (G, 1))
    gc_rows = grp(jnp.arange(GC), C)[:, None]
    kcomp = kern[:, :, :L].transpose(0, 2, 3, 1, 4).reshape(2, L, C, GC)
    kbd = jnp.where(gc_rows == grp(jnp.arange(GC), C)[None, :],
                    jnp.einsum("rc,dlcq->dlrq", row_expand, kcomp), 0.0)
    q_re, q_im = pw_re[..., :L], pw_im[..., :L]
    mb_re = q_re[..., None] * bb_re[:, :, :, None] - q_im[..., None] * bb_im[:, :, :, None]
    mb_im = q_re[..., None] * bb_im[:, :, :, None] + q_im[..., None] * bb_re[:, :, :, None]
    mb = jnp.stack([mb_re, mb_im], axis=1)
    mb = jnp.stack([mb[0][..., ::-1, :], mb[1]])
    mbcomp = mb.transpose(0, 4, 5, 1, 2, 3).reshape(2, L, C, 2 * GP)
    m_b = jnp.where(gc_rows == grp(jnp.arange(2 * GP), P)[None, :],
                    jnp.einsum("rc,djcq->djrq", row_expand, mbcomp), 0.0)
    w = jnp.stack([ca_re[..., 1:], -ca_im[..., 1:]], axis=1)
    w = jnp.stack([w[0], w[1][..., ::-1]])
    wcomp = w.transpose(0, 1, 2, 4, 5, 3).reshape(2, 2 * GP, L * C)
    col = jnp.arange(L * GC)
    col_expand = jnp.logical_and((jnp.arange(L * C) // C)[:, None] == (col // GC)[None, :],
                                 (jnp.arange(L * C) % C)[:, None] == (col % C)[None, :]).astype(F32)
    m_c = jnp.where(grp(jnp.arange(2 * GP), P)[:, None] == grp(col, C)[None, :],
                    jnp.einsum("drk,kq->drq", wcomp, col_expand), 0.0)
    al = jnp.stack([pw_re[..., L].reshape(2, GP), pw_im[..., L].reshape(2, GP)], axis=1)
    al = jnp.concatenate([al, jnp.zeros((2, 6, GP), F32)], axis=1)
    return m_b.astype(BF16), m_c.astype(BF16), kbd.astype(BF16), al


def _da_prep_body(x_ref, cos_ref, sa_ref, sb_ref, q_ref, k_ref, vt_ref):
    x = x_ref[0]
    cos, sa, sb = cos_ref[...], sa_ref[...], sb_ref[...]

    def rope(z):
        return z * cos + pltpu.roll(z, 240, 1) * sa + pltpu.roll(z, 16, 1) * sb

    q_ref[0] = (rope(x[:, 0:256]) * (DA_DK ** -0.5 * LOG2E)).astype(BF16)
    k_ref[0] = rope(x[:, 256:512]).astype(BF16)
    vt = x[:, 512:768].T.astype(BF16)
    ones = jnp.ones((DA_VROWS - DA_DV, vt.shape[1]), BF16)
    vt_ref[0, 0] = jnp.concatenate(
        [blk for h in range(DA_HEADS) for blk in (vt[h * DA_DV:(h + 1) * DA_DV], ones)], axis=0)


def _da_kblock(t):
    return next(c for c in (1408, 768, 512, 256) if t % c == 0)


DA_VROWS = DA_DV + 16
LOG2E = 1.4426950408889634


def _da_prep_call(qkv, cos, sa, sb):
    b, t, _ = qkv.shape
    tm = _da_kblock(t)
    assert t % tm == 0
    tab = pl.BlockSpec((tm, 256), lambda i, j: (j, 0))
    o = pl.BlockSpec((1, tm, 256), lambda i, j: (i, j, 0))
    return pl.pallas_call(
        _da_prep_body, grid=(b, t // tm),
        in_specs=[pl.BlockSpec((1, tm, 768), lambda i, j: (i, j, 0)), tab, tab, tab],
        out_specs=[o, o, pl.BlockSpec((1, 1, DA_HEADS * DA_VROWS, tm), lambda i, j: (i, j, 0, 0))],
        out_shape=[SDS((b, t, 256), BF16)] * 2 + [SDS((b, t // tm, DA_HEADS * DA_VROWS, tm), BF16)],
        compiler_params=_cp("parallel", "parallel"), name="da_prep")(qkv, cos, sa, sb)


def _attn_body(nkb, tk, tq, post, q_ref, k_ref, vt_ref, lam_ref, g_ref, o_ref, q8_ref, acc_ref, s0_ref, p7_ref,
               a7_ref):
    q = q_ref[0]
    lane = _iota((tq, 256), 1)
    for hm in range(8):
        q8_ref[hm] = jnp.where((lane >> 5) == hm, q, jnp.zeros_like(q))
    acc_ref[...] = jnp.zeros_like(acc_ref)

    def keys(i):
        return k_ref[0, pl.ds(pl.multiple_of(i * tk, 8), tk), :]

    s0_ref[0] = _nt(keys(0), q8_ref[0])

    nsplit = 4 if tk % 32 == 0 else 1
    p7_ref[...] = jnp.zeros_like(p7_ref)
    a7_ref[...] = jnp.ones_like(a7_ref)

    def values(i, hm):
        h = hm // 2
        return vt_ref[0, i][h * DA_VROWS:(h + 1) * DA_VROWS, :tk]

    def accumulate(hm, vals, p, alpha):
        acc_ref[hm] = acc_ref[hm] * alpha + _dot(vals, p)

    def kblock(i, carry):
        m = carry
        kb = keys(i)
        kb_next = keys(jnp.minimum(i + 1, nkb - 1))
        m_rows = []
        pending = (7, values(jnp.maximum(i - 1, 0), 7), p7_ref[...], a7_ref[0:1, :])
        for hm in range(8):
            s0_ref[(hm + 1) % 2] = _nt(kb, q8_ref[hm + 1]) if hm < 7 else _nt(kb_next, q8_ref[0])
            accumulate(*pending)
            s = s0_ref[hm % 2]
            m_old = m[hm:hm + 1, :]
            part = s[0:tk // nsplit]
            for r in range(1, nsplit):
                part = jnp.maximum(part, s[r * tk // nsplit:(r + 1) * tk // nsplit])
            m_new = jnp.maximum(m_old, jnp.max(part, axis=0, keepdims=True))
            alpha = jnp.exp2(m_old - m_new)
            p = jnp.exp2(s - m_new).astype(BF16)
            m_rows.append(m_new)
            pending = (hm, values(i, hm), p, alpha)
        p7_ref[...] = pending[2]
        a7_ref[0:1, :] = pending[3]
        return jnp.concatenate(m_rows, axis=0)

    lax.fori_loop(0, nkb, kblock, jnp.full((8, tq), -1e30, F32),
                  unroll=next(u for u in (3, 2, 1) if nkb % u == 0))
    accumulate(7, values(nkb - 1, 7), p7_ref[...], a7_ref[0:1, :])
    lam = lam_ref[0:1, 0:1]
    heads = []
    for h in range(DA_HEADS):
        a1, a2 = acc_ref[2 * h], acc_ref[2 * h + 1]
        o = a1[0:DA_DV] / a1[DA_DV:DA_DV + 1] - lam * (a2[0:DA_DV] / a2[DA_DV:DA_DV + 1])
        ms = jnp.mean(o * o, axis=0, keepdims=True)
        heads.append(o * lax.rsqrt(ms + 1e-5) * g_ref[...] * post)
    o_ref[0] = jnp.concatenate(heads, axis=0).T


def _attn_call(q, k, vt, lam_row, gain_col, post, q_off, n_q, n_keys, tq, tk):
    b = q.shape[0]
    assert n_keys % tk == 0 and n_q % tq == 0 and q_off % tq == 0
    nkb = n_keys // tk
    assert tk == vt.shape[3] or nkb == 1
    qo = q_off // tq
    return pl.pallas_call(
        functools.partial(_attn_body, nkb, tk, tq, post), grid=(b, n_q // tq),
        in_specs=[pl.BlockSpec((1, tq, 256), lambda i, j: (i, j + qo, 0)),
                  pl.BlockSpec((1, n_keys, 256), lambda i, j: (i, 0, 0)),
                  pl.BlockSpec((1, nkb, vt.shape[2], vt.shape[3]), lambda i, j: (i, 0, 0, 0)),
                  pl.BlockSpec((1, 128), lambda i, j: (0, 0)),
                  pl.BlockSpec((DA_DV, 1), lambda i, j: (0, 0))],
        out_specs=pl.BlockSpec((1, tq, 256), lambda i, j: (i, j, 0)),
        out_shape=SDS((b, n_q, 256), F32),
        scratch_shapes=[pltpu.VMEM((8, tq, 256), BF16), pltpu.VMEM((8, DA_VROWS, tq), F32),
                        pltpu.VMEM((2, tk, tq), F32), pltpu.VMEM((tk, tq), BF16), pltpu.VMEM((8, tq), F32)],
        compiler_params=_cp("parallel", "parallel"), name="diff_attn")(q, k, vt, lam_row, gain_col)


def _gla_body(rev, n, qkv_ref, sm_ref, wg_ref, bg_ref, o_ref, st_ref):
    c = GLA_CHUNK
    tm = n * c

    @pl.when(pl.program_id(1) == 0)
    def _():
        st_ref[...] = jnp.zeros_like(st_ref)

    x = qkv_ref[0]
    q, k, v = x[:, 0:128] * (GLA_DK ** -0.5), x[:, 128:256], x[:, 256:512]
    z = _dot(sm_ref[0], wg_ref[0], precision=HI) + bg_ref[0]
    la = jax.nn.log_sigmoid(z) * (1.0 / GLA_TAU)
    r_i, c_i = _iota((tm, tm), 0), _iota((tm, tm), 1)
    before = (c_i >= r_i) if rev else (c_i <= r_i)
    tri = jnp.logical_and((r_i >> 4) == (c_i >> 4), before)
    b = _sel_dot(tri, la)
    chunk_sel = (_iota((tm, 128), 0) >> 4) == _iota((tm, 128), 1)
    a_cols = jnp.exp(_dot_sel(la, chunk_sel, dot=_tn))
    b3, q3, k3 = b.reshape(n, c, 128), q.reshape(n, c, 128), k.reshape(n, c, 128)
    v3 = v.reshape(n, c, 256)
    bl3 = b3[:, 0:1, :] if rev else b3[:, c - 1:c, :]
    row3 = _iota((n, c, 128), 1)
    head_ones = ((_iota((128, 256), 0) >> 5) == (_iota((128, 256), 1) >> 6)).astype(BF16)
    o3 = jnp.zeros((n, c, 256), F32)
    for j in range(c):
        ok = (row3 <= j) if rev else (row3 >= j)
        e = jnp.where(ok, jnp.exp(jnp.where(ok, b3 - b3[:, j:j + 1, :], 0.0)), 0.0)
        part = (e * q3 * k3[:, j:j + 1, :]).reshape(tm, 128).astype(BF16)
        o3 = o3 + _dot(part, head_ones).reshape(n, c, 256) * v3[:, j:j + 1, :]
    q_dec = (q3 * jnp.exp(b3)).astype(BF16)
    k_dec = (k3 * jnp.exp(bl3 - b3)).astype(BF16)
    vb = v3.astype(BF16)
    st_mask = (_iota((128, 256), 0) >> 5) == (_iota((128, 256), 1) >> 6)
    order = range(n - 1, -1, -1) if rev else range(n)
    upd = {ci: jnp.where(st_mask, _tn(k_dec[ci], vb[ci]), 0.0) for ci in order}
    st = st_ref[...]
    inter = [None] * n
    for ci in order:
        inter[ci] = _dot(q_dec[ci], st.astype(BF16))
        st = st * a_cols[:, ci:ci + 1] + upd[ci]
    st_ref[...] = st
    o_ref[0] = jnp.concatenate(inter, axis=0) + o3.reshape(tm, 256)


def _gla_call(qkv, small, wg, bg, n_ctx, rev):
    b, t, _ = qkv.shape
    tm = 256
    assert t % tm == 0 and n_ctx % tm == 0 and tm // GLA_CHUNK <= 128 and GLA_CHUNK == 16
    blk = lambda i, j: _scan_order(rev, j, n_ctx // tm, t // tm)
    return pl.pallas_call(
        functools.partial(_gla_body, rev, tm // GLA_CHUNK), grid=(b, t // tm),
        in_specs=[pl.BlockSpec((1, tm, 512), lambda i, j: (i, blk(i, j), 0)),
                  pl.BlockSpec((1, tm, 128), lambda i, j: (i, blk(i, j), 0)),
                  pl.BlockSpec((1, 128, 128), lambda i, j: (0, 0, 0)),
                  pl.BlockSpec((1, 1, 128), lambda i, j: (0, 0, 0))],
        out_specs=pl.BlockSpec((1, tm, 256), lambda i, j: (i, blk(i, j), 0)),
        out_shape=SDS((b, t, 256), F32), scratch_shapes=[pltpu.VMEM((128, 256), F32)],
        compiler_params=_cp("parallel", "arbitrary"), name="gla_scan")(qkv, small, wg, bg)


def _merge_body(n_ctx, tm, nsub, a_ref, h_ref, mv_ref, dnf_ref, dnr_ref, dnz_ref, s5y_ref, s5yr_ref, s5u_ref, *rest):
    da_refs, rest = rest[:nsub], rest[nsub:]
    (dac_ref, glf_ref, glr_ref, glz_ref, vec_ref, gluw_ref, wgate_ref, bgate_ref, wbr_ref, wout_ref, n2g_ref,
     wr_ref, br_ref, h_out, f_out, wt_out) = rest
    t = pl.program_id(1)
    d = h_ref.shape[2]
    mv = mv_ref[0]
    vec = vec_ref[...]
    ones64 = _block_ones(256, 6)

    def mix(k):
        sl = slice(k * tm, (k + 1) * tm)
        z = dnz_ref[0, sl, :]
        ya = _head_rms(dnf_ref[0, sl, :] + dnr_ref[0, sl, :], vec[0:1], NORM_EPS, ones64) * (z * jax.nn.sigmoid(z))
        u = s5u_ref[0, sl, :].astype(F32)
        zz = jax.nn.gelu(s5y_ref[0, sl, :].astype(F32) + s5yr_ref[0, sl, :].astype(F32) + vec[2:3] * u)
        yb = zz * jax.nn.sigmoid(_bdot(zz, gluw_ref[...]) + vec[3:4])
        r = glz_ref[0, sl, :]
        yd = _head_rms(glf_ref[0, sl, :] + glr_ref[0, sl, :], vec[1:2], NORM_EPS, ones64) * (r * jax.nn.sigmoid(r))
        a = a_ref[0, sl, :]
        acc = jnp.zeros((tm, d), F32)
        yc = jnp.where((t * nsub + k) * tm < n_ctx, dac_ref[0], da_refs[k][0])
        for i, y in enumerate((ya, yb, yc, yd)):
            gate = jax.nn.sigmoid(_dot(a, wgate_ref[:, i * d:(i + 1) * d]) + bgate_ref[:, i * d:(i + 1) * d])
            acc = acc + gate * _dot(y.astype(BF16), wbr_ref[i])
        return _dot(acc.astype(BF16), wout_ref[...])

    def finish(k, out):
        sl = slice(k * tm, (k + 1) * tm)
        is_ctx = ((t * nsub + k) * tm + _iota((tm, 1), 0)) < n_ctx
        gate2 = jnp.where(is_ctx, mv[10:11], mv[2:3])
        h_new = h_ref[0, sl, :] + gate2 * out
        h_out[0, sl, :] = h_new
        f = _modulated_norm(h_new, n2g_ref[...], mv, is_ctx, 3, 4)
        f_out[0, sl, :] = f.astype(BF16)
        f_hi = f.astype(BF16)
        f_lo = (f - f_hi.astype(F32)).astype(BF16)
        wr = wr_ref[...]
        wr_hi = wr.astype(BF16)
        wr_lo = (wr - wr_hi.astype(F32)).astype(BF16)
        lg = _nt(jnp.concatenate([wr_hi, wr_lo], axis=0), f_hi)
        sc = jax.nn.sigmoid(lg[:N_EXPERTS] + lg[N_EXPERTS:] + _nt(wr_hi, f_lo))
        gr = sc + br_ref[...]
        rows_ = [gr[e:e + 1, :] for e in range(N_EXPERTS)]
        best_v, best_g = None, None
        for g in range(N_GROUPS):
            x = rows_[g * EPG:(g + 1) * EPG]
            top2 = None
            for i in range(EPG):
                for j in range(i + 1, EPG):
                    s_ij = x[i] + x[j]
                    top2 = s_ij if top2 is None else jnp.maximum(top2, s_ij)
            if best_v is None:
                best_v, best_g = top2, jnp.zeros_like(top2, dtype=jnp.int32)
            else:
                take = top2 > best_v
                best_v = jnp.where(take, top2, best_v)
                best_g = jnp.where(take, g, best_g)
        cand = []
        for kk in range(EPG):
            ck = rows_[kk]
            for g in range(1, N_GROUPS):
                ck = jnp.where(best_g == g, rows_[g * EPG + kk], ck)
            cand.append(ck)
        v1, i1 = cand[0], jnp.zeros_like(best_g)
        for kk in range(1, EPG):
            take = cand[kk] > v1
            v1 = jnp.where(take, cand[kk], v1)
            i1 = jnp.where(take, kk, i1)
        v2, i2 = jnp.full_like(v1, -jnp.inf), jnp.zeros_like(best_g)
        for kk in range(EPG):
            take = jnp.logical_and(i1 != kk, cand[kk] > v2)
            v2 = jnp.where(take, cand[kk], v2)
            i2 = jnp.where(take, kk, i2)
        e1, e2 = best_g * EPG + i1, best_g * EPG + i2
        e_iota = _iota((N_EXPERTS, tm), 0)
        s1 = jnp.sum(jnp.where(e_iota == e1, sc, 0.0), axis=0, keepdims=True)
        s2 = jnp.sum(jnp.where(e_iota == e2, sc, 0.0), axis=0, keepdims=True)
        tot = s1 + s2
        wt_out[0, :, sl] = jnp.where(e_iota == e1, s1 / tot, jnp.where(e_iota == e2, s2 / tot, 0.0))

    outs = [mix(k) for k in range(nsub)]
    for k in range(nsub):
        finish(k, outs[k])


def _merge_call(a, h, modv, dn_f, dn_b, dn_z, s5_y, s5_u, da_lat, da_ctx, gl_f, gl_b, gl_r, vec, glu_w,
                w_gate, b_gate, w_branch, w_out, n2g, w_rt, b_rt, n_ctx):
    b, t, d = h.shape
    tm = 256
    nsub = next(c for c in (3, 2, 1) if (t // tm) % c == 0)
    ts = nsub * tm
    assert t % tm == 0 and n_ctx % tm == 0
    tok = lambda w: pl.BlockSpec((1, ts, w), lambda i, j: (i, j, 0))
    fwd = pl.BlockSpec((None, 1, ts, 256), lambda i, j: (0, i, j, 0))
    bwd = pl.BlockSpec((None, 1, ts, 256), lambda i, j: (1, i, j, 0))
    full = lambda *shape: pl.BlockSpec(shape, lambda i, j: (0,) * len(shape), pipeline_mode=pl.Buffered(1))
    da_specs = [pl.BlockSpec((1, tm, 256), (lambda i, j, k=k: (i, jnp.maximum(nsub * j + k - n_ctx // tm, 0), 0)))
                for k in range(nsub)]
    return pl.pallas_call(
        functools.partial(_merge_body, n_ctx, tm, nsub), grid=(b, t // ts),
        in_specs=[tok(d), tok(d), pl.BlockSpec((1, 16, d), lambda i, j: (i, 0, 0)),
                  tok(256), tok(256), tok(256), fwd, bwd, tok(256)] + da_specs + [
                  pl.BlockSpec((1, tm, 256), lambda i, j: (i, 0, 0)), tok(256), tok(256),
                  tok(256), full(8, 256), full(256, 256), full(d, 4 * d), full(1, 4 * d),
                  full(4, 256, d), full(d, d), full(1, d), full(N_EXPERTS, d), full(N_EXPERTS, 1)],
        out_specs=[tok(d), tok(d), pl.BlockSpec((1, N_EXPERTS, ts), lambda i, j: (i, 0, j))],
        out_shape=[SDS((b, t, d), F32), SDS((b, t, d), BF16), SDS((b, N_EXPERTS, t), F32)],
        compiler_params=_cp("parallel", "parallel", vmem=MERGE_VMEM_LIMIT), name="merge_router")(
            a, h, modv, dn_f, dn_b, dn_z, s5_y, s5_y, s5_u, *([da_lat] * nsub), da_ctx, gl_f, gl_b, gl_r, vec, glu_w,
            w_gate, b_gate, w_branch, w_out, n2g, w_rt, b_rt)


def _moe_body(n_ctx, tt, final, f_ref, wt_ref, h_ref, mv_ref, wg_ref, wu_ref, wd_ref, fg_ref, o_ref,
              acc_ref, rank_t_ref, rank_c_ref):
    e = pl.program_id(2)
    cap = MOE_CAP

    @pl.when(e == 0)
    def _():
        acc_ref[...] = jnp.zeros_like(acc_ref)
        routed = wt_ref[0] != 0.0
        earlier = (_iota((tt, tt), 0) < _iota((tt, tt), 1)).astype(BF16)
        rank = _dot(routed.astype(BF16), earlier)
        rank = jnp.where(routed, rank, -1.0)
        rank_t_ref[...] = rank
        eye = _iota((N_EXPERTS, N_EXPERTS), 0) == _iota((N_EXPERTS, N_EXPERTS), 1)
        rank_c_ref[...] = _dot_sel(rank, eye, dot=_tn)

    x = f_ref[0]
    rank_row = rank_t_ref[pl.ds(e, 1), :]
    w_row = wt_ref[0, pl.ds(e, 1), :]
    rc = rank_c_ref[...]
    rank_col = jnp.sum(jnp.where(_iota(rc.shape, 1) == e, rc, 0.0), axis=1, keepdims=True)
    count = jnp.sum((rank_row >= 0.0).astype(F32))

    for blk in range(-(-tt // cap)):
        @pl.when(count > blk * cap)
        def _(blk=blk):
            sel = rank_row == (_iota((cap, tt), 0) + blk * cap).astype(F32)
            xg = _dot(sel.astype(BF16), x).astype(BF16)
            w_g = jnp.sum(jnp.where(sel, w_row, 0.0), axis=1, keepdims=True)
            g = _dot(xg, wg_ref[0])
            hid = g * jax.nn.sigmoid(g) * _dot(xg, wu_ref[0]) * w_g
            y = _dot(hid.astype(BF16), wd_ref[0]).astype(BF16)
            back = rank_col == (_iota((tt, cap), 1) + blk * cap).astype(F32)
            acc_ref[...] += _dot(back.astype(BF16), y)

    @pl.when(e == N_EXPERTS - 1)
    def _():
        row = pl.program_id(1) * tt + _iota((tt, 1), 0)
        mv = mv_ref[0]
        gate = jnp.where(row < n_ctx, mv[13:14], mv[5:6])
        hn = h_ref[0] + gate * acc_ref[...]
        if final:
            hn = hn * lax.rsqrt(jnp.mean(hn * hn, axis=-1, keepdims=True) + NORM_EPS) * fg_ref[...]
        o_ref[0] = hn


def _moe_call(f, wc, h, modv, weg, weu, wed, final_g, n_ctx, final):
    b, t, d = h.shape
    fe = weg.shape[2]
    tt = next(c for c in (1408, 1024, 768, 512, 256, 128, 64, 32, 16, 8) if t % c == 0)
    tok = lambda w: pl.BlockSpec((1, tt, w), lambda i, j, e: (i, j, 0))
    return pl.pallas_call(
        functools.partial(_moe_body, n_ctx, tt, final), grid=(b, t // tt, N_EXPERTS),
        in_specs=[tok(d), pl.BlockSpec((1, N_EXPERTS, tt), lambda i, j, e: (i, 0, j)), tok(d),
                  pl.BlockSpec((1, 16, d), lambda i, j, e: (i, 0, 0)),
                  pl.BlockSpec((1, d, fe), lambda i, j, e: (e, 0, 0)),
                  pl.BlockSpec((1, d, fe), lambda i, j, e: (e, 0, 0)),
                  pl.BlockSpec((1, fe, d), lambda i, j, e: (e, 0, 0)),
                  pl.BlockSpec((1, d), lambda i, j, e: (0, 0))],
        out_specs=tok(d), out_shape=SDS((b, t, d), F32),
        scratch_shapes=[pltpu.VMEM((tt, d), F32), pltpu.VMEM((N_EXPERTS, tt), F32), pltpu.VMEM((tt, N_EXPERTS), F32)],
        compiler_params=_cp("parallel", "parallel", "arbitrary"), name="moe_experts")(
            f, wc, h, modv, weg, weu, wed, final_g)


def _rope_tables(n_ctx, s):
    t = jnp.arange(s)
    rowp, colp = (t // GRID_W).astype(F32), (t % GRID_W).astype(F32)
    n_freq = DA_DK // 4
    inv = ROPE_BASE ** (-jnp.arange(n_freq, dtype=F32) / n_freq)
    ang = jnp.concatenate([rowp[:, None] * inv, colp[:, None] * inv], axis=-1)
    cos, sin = jnp.cos(ang), jnp.sin(ang)
    zero = jnp.zeros_like(sin)
    cos32 = jnp.concatenate([cos, cos], axis=-1)
    sa32 = jnp.concatenate([-sin, zero], axis=-1)
    sb32 = jnp.concatenate([zero, sin], axis=-1)
    def full(tab, ctx_val):
        tab = jnp.tile(tab, (1, 8))
        return jnp.concatenate([jnp.full((n_ctx, 256), ctx_val, F32), tab], axis=0)
    return full(cos32, 1.0), full(sa32, 0.0), full(sb32, 0.0)


def _perm_w_in(w_in):
    offs, o = {}, 0
    for name, width in (("dn_qkv", 768), ("dn_z", 256), ("dn_a", 8), ("dn_b", 8), ("s5_u", 256), ("da_q", 256),
                        ("da_k", 256), ("da_v", 256), ("gla_q", 128), ("gla_k", 128), ("gla_v", 256),
                        ("gla_r", 256), ("gla_gate", 32)):
        offs[name] = (o, width)
        o += width
    col = lambda n: w_in[:, offs[n][0]:offs[n][0] + offs[n][1]]
    pad = jnp.zeros((w_in.shape[0], 128 - 48), w_in.dtype)
    return jnp.concatenate([col(n) for n in ("dn_qkv", "dn_z", "s5_u", "da_q", "da_k", "da_v", "gla_q", "gla_k",
                                              "gla_v", "gla_r", "dn_a", "dn_b", "gla_gate")] + [pad], axis=1)


def kernel(x, c, ctx, c_ctx, w_mod, b_mod, norm1_g, norm2_g, w_in, dn_conv, dn_a_log, dn_dt_bias, dn_norm_g,
           s5_a_re, s5_a_im, s5_log_dt, s5_b_re, s5_b_im, s5_c_re, s5_c_im, s5_d, s5_glu_w, s5_glu_b,
           da_lambda, da_norm_g, gla_w_gate, gla_b_gate, gla_norm_g, w_branch, w_gate, b_gate, w_out,
           w_router, b_router, w_e_gate, w_e_up, w_e_down, final_g):
    b, s, d = x.shape
    n_ctx = ctx.shape[1]
    t = n_ctx + s
    depth = w_mod.shape[0]
    assert b + 1 <= 8 and b == 4

    cvec = jnp.zeros((8, d), F32).at[:b].set(c).at[b].set(c_ctx)
    mod_all = _mod_call(cvec, w_mod, b_mod).reshape(depth, 8, 6, d)
    cos_t, sa_t, sb_t = _rope_tables(n_ctx, s)
    h = jnp.concatenate([ctx, x], axis=1)

    lane256 = jnp.arange(256) // 64
    eg = jnp.stack([(jnp.arange(128)[:, None] == (SM_DN_A + dd * 4 + lane256)[None, :]).astype(F32) for dd in range(2)])
    eb = jnp.stack([(jnp.arange(128)[:, None] == (SM_DN_B + dd * 4 + lane256)[None, :]).astype(F32) for dd in range(2)])

    for layer in range(depth):
        lam_init = 0.8 - 0.6 * math.exp(-0.3 * layer)
        ml = mod_all[layer]
        modv = jnp.zeros((b, 16, d), F32).at[:, 0:6].set(ml[:b]).at[:, 8:14].set(jnp.broadcast_to(ml[b], (b, 6, d)))

        a, dn_qkv, dn_z, s5_u, da_qkv, gla_qkv, gla_r, small = _proj_call(
            h, modv, norm1_g[layer][None, :], _perm_w_in(w_in[layer]).astype(BF16), n_ctx)

        dq, dk_, dv, gb = _dn_prep_call(dn_qkv, small, dn_conv[layer], dn_a_log[layer], dn_dt_bias[layer], n_ctx)
        dn_f, dn_b = [_dn_scan_call(dq, dk_, dv, gb, eg[dd:dd + 1], eb[dd:dd + 1], n_ctx, dd == 1)
                      for dd in range(2)]

        m_b, m_c, kbd, al = _s5_operators(
            s5_a_re[layer], s5_a_im[layer], s5_log_dt[layer], s5_b_re[layer], s5_b_im[layer],
            s5_c_re[layer], s5_c_im[layer])
        s5_y = _s5_call(s5_u, m_b, m_c, kbd, al, n_ctx)

        lf = da_lambda[layer]
        lam = jnp.exp(jnp.sum(lf[0] * lf[1])) - jnp.exp(jnp.sum(lf[2] * lf[3])) + lam_init
        lam_row = jnp.full((1, 128), lam, F32)
        da_gain = da_norm_g[layer][:, None]
        aq, ak, avt = _da_prep_call(da_qkv, cos_t, sa_t, sb_t)
        da_lat = _attn_call(aq, ak, avt, lam_row, da_gain, 1.0 - lam_init, n_ctx, s, t, 256, _da_kblock(t))
        if layer < depth - 1:
            da_ctx = _attn_call(aq, ak, avt, lam_row, da_gain, 1.0 - lam_init, 0, n_ctx, n_ctx, 256, n_ctx)
        else:
            da_ctx = da_lat

        wg2 = jnp.zeros((2, 128, 128), F32)
        for dd in range(2):
            wg2 = wg2.at[dd, SM_GLA + dd * GLA_RANK:SM_GLA + (dd + 1) * GLA_RANK].set(gla_w_gate[layer, dd])
        gl_f, gl_b = [_gla_call(gla_qkv, small, wg2[dd:dd + 1], gla_b_gate[layer][dd:dd + 1, None, :], n_ctx, dd == 1)
                      for dd in range(2)]

        vec = jnp.zeros((8, 256), F32)
        vec = vec.at[0].set(jnp.tile(dn_norm_g[layer], DN_HEADS)).at[1].set(jnp.tile(gla_norm_g[layer], GLA_HEADS))
        vec = vec.at[2].set(s5_d[layer]).at[3].set(s5_glu_b[layer])
        h, f, wc = _merge_call(
            a, h, modv, dn_f, dn_b, dn_z, s5_y, s5_u, da_lat, da_ctx, gl_f, gl_b, gla_r, vec,
            s5_glu_w[layer].astype(BF16), w_gate[layer].astype(BF16), b_gate[layer][None, :],
            w_branch[layer].astype(BF16), w_out[layer].astype(BF16), norm2_g[layer][None, :],
            w_router.T, b_router[:, None], n_ctx)

        h = _moe_call(f, wc, h, modv, w_e_gate[layer].astype(BF16), w_e_up[layer].astype(BF16),
                      w_e_down[layer].astype(BF16), final_g[None, :], n_ctx, layer == depth - 1)
    return h[:, n_ctx:]
```
